```python
import jax, jax.numpy as jnp
from jax import lax
import numpy as np

D_MODEL = 2048
BATCH = 4
SEQ = 4096
DEPTH = 1

N_MEM = 256
EPS = 1e-6
CHUNK = 64

GLA_HEADS = 4
GLA_DK = 128
GLA_DV = 256
GLA_K = GLA_HEADS * GLA_DK
GLA_V = GLA_HEADS * GLA_DV
GLA_RANK = 16
GLA_TAU = 16.0

MLSTM_HEADS = 4
MLSTM_DH = 256
MLSTM_W = MLSTM_HEADS * MLSTM_DH
CONV_WIDTH = 5

IN_SIZES = (GLA_K, GLA_K, GLA_V, GLA_V, GLA_RANK, GLA_RANK,
            MLSTM_W, MLSTM_W, 4 * MLSTM_HEADS, D_MODEL, D_MODEL)
IN_TOTAL = sum(IN_SIZES)

XATTN_HEADS = 4
XATTN_DH = D_MODEL // XATTN_HEADS

N_GROUPS = 4
EXPERTS_PER_GROUP = 8
N_EXPERTS = N_GROUPS * EXPERTS_PER_GROUP
TOP_K_IN_GROUP = 2
D_EXPERT = 512
MOE_BLOCK = 128

kernel_name = 'hybrid_gla_mlstm_xattn_hmoe_encoder'


def rmsnorm(x, g):
    xf = x.astype(jnp.float32)
    y = xf * lax.rsqrt(jnp.mean(xf * xf, axis=-1, keepdims=True) + EPS)
    return (y * g.astype(jnp.float32)).astype(x.dtype)


def _flip(t):
    return jnp.flip(t, axis=1)


def _chunk(t):
    b, s, h, d = t.shape
    return t.reshape(b, s // CHUNK, CHUNK, h, d).transpose(1, 0, 3, 2, 4)


def _unchunk(t):
    n, b, h, c, d = t.shape
    return t.transpose(1, 0, 3, 2, 4).reshape(b, n * c, h, d)


def _chunk_gate(t):
    b, s, h = t.shape
    return t.reshape(b, s // CHUNK, CHUNK, h).transpose(1, 0, 3, 2)


def gla_scan(q, k, v, log_a):
    bsz, seq, nh, dk = q.shape
    dv = v.shape[-1]
    qc, kc, vc = _chunk(q), _chunk(k), _chunk(v)
    bc = jnp.cumsum(_chunk(log_a), axis=3)
    mask = jnp.tril(jnp.ones((CHUNK, CHUNK), bool))[:, :, None]

    def step(state, xs):
        q_, k_, v_, b_ = xs
        inter = jnp.einsum('bhtd,bhde->bhte', q_ * jnp.exp(b_), state)
        diff = jnp.where(mask, b_[:, :, :, None, :] - b_[:, :, None, :, :], -jnp.inf)
        scores = jnp.einsum('bhtd,bhsd,bhtsd->bhts', q_, k_, jnp.exp(diff))
        intra = jnp.einsum('bhts,bhse->bhte', scores, v_)
        b_last = b_[:, :, -1:, :]
        k_dec = k_ * jnp.exp(b_last - b_)
        new_state = (jnp.exp(b_last[:, :, 0, :, None]) * state
                     + jnp.einsum('bhsd,bhse->bhde', k_dec, v_))
        return new_state, inter + intra

    s0 = jnp.zeros((bsz, nh, dk, dv), jnp.float32)
    _, out = lax.scan(step, s0, (qc, kc, vc, bc))
    return _unchunk(out)


def mlstm_scan(q, k, v, i_pre, f_pre):
    bsz, seq, nh, dh = q.shape
    qc, kc, vc = _chunk(q), _chunk(k), _chunk(v)
    ic = _chunk_gate(i_pre)
    bc = jnp.cumsum(_chunk_gate(jax.nn.log_sigmoid(f_pre)), axis=-1)
    mask = jnp.tril(jnp.ones((CHUNK, CHUNK), bool))

    def step(carry, xs):
        c_st, n_st, m_st = carry
        q_, k_, v_, i_, b_ = xs
        dmat = jnp.where(mask, b_[..., :, None] - b_[..., None, :] + i_[..., None, :], -jnp.inf)
        inter_log = b_ + m_st[..., None]
        m_t = jnp.maximum(inter_log, jnp.max(dmat, axis=-1))
        w_intra = jnp.exp(dmat - m_t[..., None])
        w_inter = jnp.exp(inter_log - m_t)
        s = jnp.einsum('bhtd,bhsd->bhts', q_, k_) * w_intra
        num = (w_inter[..., None] * jnp.einsum('bhtd,bhde->bhte', q_, c_st)
               + jnp.einsum('bhts,bhse->bhte', s, v_))
        den = w_inter * jnp.einsum('bhtd,bhd->bht', q_, n_st) + jnp.sum(s, axis=-1)
        h = num / jnp.maximum(jnp.abs(den), jnp.exp(-m_t))[..., None]
        b_last = b_[..., -1]
        upd_log = b_last[..., None] - b_ + i_
        m_new = jnp.maximum(b_last + m_st, jnp.max(upd_log, axis=-1))
        w_old = jnp.exp(b_last + m_st - m_new)
        kw = k_ * jnp.exp(upd_log - m_new[..., None])[..., None]
        c_new = w_old[..., None, None] * c_st + jnp.einsum('bhsd,bhse->bhde', kw, v_)
        n_new = w_old[..., None] * n_st + jnp.sum(kw, axis=-2)
        return (c_new, n_new, m_new), h

    init = (jnp.zeros((bsz, nh, dh, dh), jnp.float32),
            jnp.zeros((bsz, nh, dh), jnp.float32),
            jnp.zeros((bsz, nh), jnp.float32))
    _, out = lax.scan(step, init, (qc, kc, vc, ic, bc))
    return _unchunk(out)


def hybrid_mixer(xn, w_in, gla_w_lr_f, gla_b_lr_f, gla_w_lr_b, gla_b_lr_b, gla_norm,
                 conv_w, conv_b, m_wq, m_wk, m_wv, m_gate_bias, m_norm,
                 w_branch_a, w_branch_b, w_mix_out):
    bsz, seq, _ = xn.shape
    dt = xn.dtype
    proj = xn @ w_in
    offsets = [int(o) for o in np.cumsum(IN_SIZES)[:-1]]
    (gq, gk, gv, gg, lr_f, lr_b, mx, mz, mgates, gate_a, gate_b) = jnp.split(proj, offsets, axis=-1)

    f32 = jnp.float32
    q = gq.astype(f32).reshape(bsz, seq, GLA_HEADS, GLA_DK) * (GLA_DK ** -0.5)
    k = gk.astype(f32).reshape(bsz, seq, GLA_HEADS, GLA_DK)
    v = gv.astype(f32).reshape(bsz, seq, GLA_HEADS, GLA_DV)
    la_f = (jax.nn.log_sigmoid((lr_f @ gla_w_lr_f + gla_b_lr_f).astype(f32)) / GLA_TAU
            ).reshape(bsz, seq, GLA_HEADS, GLA_DK)
    la_b = (jax.nn.log_sigmoid((lr_b @ gla_w_lr_b + gla_b_lr_b).astype(f32)) / GLA_TAU
            ).reshape(bsz, seq, GLA_HEADS, GLA_DK)
    o_a = gla_scan(q, k, v, la_f) + _flip(gla_scan(_flip(q), _flip(k), _flip(v), _flip(la_b)))
    y_a = rmsnorm(o_a, gla_norm.reshape(GLA_HEADS, GLA_DV)).reshape(bsz, seq, GLA_V)
    y_a = (y_a * jax.nn.silu(gg.astype(f32))).astype(dt)

    xc = lax.conv_general_dilated(mx, conv_w, window_strides=(1,),
                                  padding=[(CONV_WIDTH // 2, CONV_WIDTH // 2)],
                                  dimension_numbers=('NWC', 'WIO', 'NWC'),
                                  feature_group_count=MLSTM_W) + conv_b
    xc = jax.nn.silu(xc.astype(f32)).reshape(bsz, seq, MLSTM_HEADS, MLSTM_DH)
    qm = jnp.einsum('bshd,hde->bshe', xc, m_wq.astype(f32))
    km = jnp.einsum('bshd,hde->bshe', xc, m_wk.astype(f32)) * (MLSTM_DH ** -0.5)
    vm = jnp.einsum('bshd,hde->bshe', mx.astype(f32).reshape(bsz, seq, MLSTM_HEADS, MLSTM_DH),
                    m_wv.astype(f32))
    gates = mgates.astype(f32).reshape(bsz, seq, 4, MLSTM_HEADS) + m_gate_bias.astype(f32)
    h_f = mlstm_scan(qm, km, vm, gates[:, :, 0], gates[:, :, 1])
    h_b = _flip(mlstm_scan(_flip(qm), _flip(km), _flip(vm),
                           _flip(gates[:, :, 2]), _flip(gates[:, :, 3])))
    y_b = rmsnorm(h_f + h_b, m_norm.reshape(MLSTM_HEADS, MLSTM_DH)).reshape(bsz, seq, MLSTM_W)
    y_b = (y_b * jax.nn.sigmoid(mz.astype(f32))).astype(dt)

    merged = (jax.nn.sigmoid(gate_a) * (y_a @ w_branch_a)
              + jax.nn.sigmoid(gate_b) * (y_b @ w_branch_b))
    return merged @ w_mix_out


def cross_attention(xn, mem_n, w_xq, w_xkv, w_xo):
    bsz, seq, _ = xn.shape
    q = (xn @ w_xq).reshape(bsz, seq, XATTN_HEADS, XATTN_DH)
    k, v = jnp.split(mem_n @ w_xkv, 2, axis=-1)
    k = k.reshape(bsz, N_MEM, XATTN_HEADS, XATTN_DH)
    v = v.reshape(bsz, N_MEM, XATTN_HEADS, XATTN_DH)
    scores = jnp.einsum('bshd,bmhd->bhsm', q, k).astype(jnp.float32) * (XATTN_DH ** -0.5)
    probs = jax.nn.softmax(scores, axis=-1).astype(xn.dtype)
    out = jnp.einsum('bhsm,bmhd->bshd', probs, v).reshape(bsz, seq, D_MODEL)
    return out @ w_xo


def hier_moe(xn, w_group, b_group, w_router, b_router, w_gate, w_up, w_down):
    bsz, seq, d = xn.shape
    n_tok = bsz * seq
    xt = xn.reshape(n_tok, d)
    g_prob = jax.nn.softmax((xt @ w_group).astype(jnp.float32) + b_group.astype(jnp.float32), axis=-1)
    g_sel = jnp.argmax(g_prob, axis=-1)
    p_g = jnp.take_along_axis(g_prob, g_sel[:, None], axis=-1)[:, 0]
    e_all = jnp.einsum('td,gde->tge', xt, w_router).astype(jnp.float32) + b_router.astype(jnp.float32)
    e_logits = jnp.take_along_axis(e_all, g_sel[:, None, None], axis=1)[:, 0]
    e_prob = jax.nn.softmax(e_logits, axis=-1)
    top_p, top_i = lax.top_k(e_prob, TOP_K_IN_GROUP)
    weights = p_g[:, None] * top_p / jnp.sum(top_p, axis=-1, keepdims=True)
    expert = g_sel[:, None] * EXPERTS_PER_GROUP + top_i

    n_asg = n_tok * TOP_K_IN_GROUP
    flat_e = expert.reshape(-1)
    flat_w = weights.reshape(-1)
    flat_t = jnp.repeat(jnp.arange(n_tok, dtype=jnp.int32), TOP_K_IN_GROUP)
    order = jnp.argsort(flat_e)
    se = flat_e[order]
    counts = jnp.bincount(flat_e, length=N_EXPERTS)
    padded = ((counts + MOE_BLOCK - 1) // MOE_BLOCK) * MOE_BLOCK
    starts = jnp.cumsum(counts) - counts
    pends = jnp.cumsum(padded)
    pstarts = pends - padded
    dest = pstarts[se] + (jnp.arange(n_asg) - starts[se])
    n_rows = n_asg + N_EXPERTS * MOE_BLOCK
    n_blocks = n_rows // MOE_BLOCK
    row_tok = jnp.full((n_rows,), n_tok, jnp.int32).at[dest].set(flat_t[order])
    row_w = jnp.zeros((n_rows,), jnp.float32).at[dest].set(flat_w[order])
    blk_e = jnp.clip(jnp.searchsorted(pends, jnp.arange(n_blocks) * MOE_BLOCK, side='right'),
                     0, N_EXPERTS - 1)
    x_pad = jnp.concatenate([xt, jnp.zeros((1, d), xt.dtype)], axis=0)
    x_rows = x_pad[row_tok].reshape(n_blocks, MOE_BLOCK, d)

    def expert_block(args):
        xb, e = args
        return (jax.nn.silu(xb @ w_gate[e]) * (xb @ w_up[e])) @ w_down[e]

    y_rows = lax.map(expert_block, (x_rows, blk_e)).reshape(n_rows, d)
    y_rows = (y_rows.astype(jnp.float32) * row_w[:, None]).astype(xt.dtype)
    y = jax.ops.segment_sum(y_rows, row_tok, num_segments=n_tok + 1)[:n_tok]
    return y.reshape(bsz, seq, d)


def setup_inputs(seed: int = 0) -> dict:
    key = jax.random.key(seed)
    ks = jax.random.split(key, 40)
    f32 = jnp.float32
    L = DEPTH

    def nrm(k, shape, scale):
        return jax.random.normal(k, shape, f32) * scale

    def gain(k, shape):
        return 1.0 + 0.02 * jax.random.normal(k, shape, f32)

    fb = jnp.linspace(3.0, 6.0, MLSTM_HEADS, dtype=f32)
    zb = jnp.zeros((MLSTM_HEADS,), f32)
    gate_base = jnp.stack([zb, fb, zb, fb])[None]
    return {
        'x': jax.random.normal(ks[0], (BATCH, SEQ, D_MODEL), f32),
        'mem': jax.random.normal(ks[1], (BATCH, N_MEM, D_MODEL), f32),
        'norm_mix': gain(ks[2], (L, D_MODEL)),
        'w_in': nrm(ks[3], (L, D_MODEL, IN_TOTAL), D_MODEL ** -0.5),
        'gla_w_lr_f': nrm(ks[4], (L, GLA_RANK, GLA_K), GLA_RANK ** -0.5),
        'gla_b_lr_f': nrm(ks[5], (L, GLA_K), 0.1),
        'gla_w_lr_b': nrm(ks[6], (L, GLA_RANK, GLA_K), GLA_RANK ** -0.5),
        'gla_b_lr_b': nrm(ks[7], (L, GLA_K), 0.1),
        'gla_norm': gain(ks[8], (L, GLA_V)),
        'conv_w': nrm(ks[9], (L, CONV_WIDTH, 1, MLSTM_W), CONV_WIDTH ** -0.5),
        'conv_b': nrm(ks[10], (L, MLSTM_W), 0.02),
        'm_wq': nrm(ks[11], (L, MLSTM_HEADS, MLSTM_DH, MLSTM_DH), MLSTM_DH ** -0.5),
        'm_wk': nrm(ks[12], (L, MLSTM_HEADS, MLSTM_DH, MLSTM_DH), MLSTM_DH ** -0.5),
        'm_wv': nrm(ks[13], (L, MLSTM_HEADS, MLSTM_DH, MLSTM_DH), MLSTM_DH ** -0.5),
        'm_gate_bias': gate_base + nrm(ks[14], (L, 4, MLSTM_HEADS), 0.1),
        'm_norm': gain(ks[15], (L, MLSTM_W)),
        'w_branch_a': nrm(ks[16], (L, GLA_V, D_MODEL), GLA_V ** -0.5),
        'w_branch_b': nrm(ks[17], (L, MLSTM_W, D_MODEL), MLSTM_W ** -0.5),
        'w_mix_out': nrm(ks[18], (L, D_MODEL, D_MODEL), D_MODEL ** -0.5),
        'norm_xattn': gain(ks[19], (L, D_MODEL)),
        'norm_mem': gain(ks[20], (L, D_MODEL)),
        'w_xq': nrm(ks[21], (L, D_MODEL, D_MODEL), D_MODEL ** -0.5),
        'w_xkv': nrm(ks[22], (L, D_MODEL, 2 * D_MODEL), D_MODEL ** -0.5),
        'w_xo': nrm(ks[23], (L, D_MODEL, D_MODEL), D_MODEL ** -0.5),
        'norm_ffn': gain(ks[24], (L, D_MODEL)),
        'w_group': nrm(ks[25], (L, D_MODEL, N_GROUPS), D_MODEL ** -0.5),
        'b_group': nrm(ks[26], (L, N_GROUPS), 0.01),
        'w_router': nrm(ks[27], (L, N_GROUPS, D_MODEL, EXPERTS_PER_GROUP), D_MODEL ** -0.5),
        'b_router': nrm(ks[28], (L, N_GROUPS, EXPERTS_PER_GROUP), 0.01),
        'w_gate': nrm(ks[29], (L, N_EXPERTS, D_MODEL, D_EXPERT), D_MODEL ** -0.5),
        'w_up': nrm(ks[30], (L, N_EXPERTS, D_MODEL, D_EXPERT), D_MODEL ** -0.5),
        'w_down': nrm(ks[31], (L, N_EXPERTS, D_EXPERT, D_MODEL), D_EXPERT ** -0.5),
        'norm_final': gain(ks[32], (D_MODEL,)),
    }


def reference(x, mem, norm_mix, w_in, gla_w_lr_f, gla_b_lr_f, gla_w_lr_b, gla_b_lr_b, gla_norm,
              conv_w, conv_b, m_wq, m_wk, m_wv, m_gate_bias, m_norm,
              w_branch_a, w_branch_b, w_mix_out,
              norm_xattn, norm_mem, w_xq, w_xkv, w_xo,
              norm_ffn, w_group, b_group, w_router, b_router, w_gate, w_up, w_down,
              norm_final):
    h = x
    for l in range(DEPTH):
        h = h + hybrid_mixer(rmsnorm(h, norm_mix[l]), w_in[l],
                             gla_w_lr_f[l], gla_b_lr_f[l], gla_w_lr_b[l], gla_b_lr_b[l], gla_norm[l],
                             conv_w[l], conv_b[l], m_wq[l], m_wk[l], m_wv[l], m_gate_bias[l], m_norm[l],
                             w_branch_a[l], w_branch_b[l], w_mix_out[l])
        h = h + cross_attention(rmsnorm(h, norm_xattn[l]), rmsnorm(mem, norm_mem[l]),
                                w_xq[l], w_xkv[l], w_xo[l])
        h = h + hier_moe(rmsnorm(h, norm_ffn[l]), w_group[l], b_group[l], w_router[l], b_router[l],
                         w_gate[l], w_up[l], w_down[l])
    return rmsnorm(h, norm_final)
```

```python
import functools

import jax
import jax.numpy as jnp
from jax import lax
from jax.experimental import pallas as pl
from jax.experimental.pallas import tpu as pltpu

F32 = jnp.float32
BF16 = jnp.bfloat16

EPS = 1e-6
D_MODEL = 2048

GLA_HEADS = 4
GLA_DK = 128
GLA_DV = 256
GLA_K = GLA_HEADS * GLA_DK
GLA_V = GLA_HEADS * GLA_DV
GLA_RANK = 16
GLA_TAU = 16.0
GLA_CHUNK = 64
GLA_SUB = 16

MLSTM_HEADS = 4
MLSTM_DH = 256
MLSTM_W = MLSTM_HEADS * MLSTM_DH
CONV_WIDTH = 5
MLSTM_CHUNK = 256
CONV_HALO = 16

XATTN_HEADS = 4
XATTN_DH = D_MODEL // XATTN_HEADS

N_GROUPS = 4
EXPERTS_PER_GROUP = 8
N_EXPERTS = N_GROUPS * EXPERTS_PER_GROUP
D_EXPERT = 512
MOE_ROWS = 256
COMBINE_TOK = 256

LANES = 128

OFF_Q, OFF_K, OFF_V, OFF_GG = 0, 512, 1024, 2048
OFF_MX, OFF_MZ, OFF_GA, OFF_GB = 3072, 4096, 5120, 7168
N_BIG = 9216
SM_LRF, SM_LRB, SM_GATES = 0, 16, 32

VMEM_LIMIT = 56 * 1024 * 1024


def _cparams(sem):
    return pltpu.CompilerParams(dimension_semantics=sem, vmem_limit_bytes=VMEM_LIMIT)


def _rms(x, g):
    return x * lax.rsqrt(jnp.mean(x * x, axis=-1, keepdims=True) + EPS) * g


def _log_sigmoid(x):
    return jnp.minimum(x, 0.0) - jnp.log(1.0 + jnp.exp(-jnp.abs(x)))


def _dot(a, b):
    return jnp.dot(a, b, preferred_element_type=F32)


def _dot_nt(a, b):
    return lax.dot_general(a, b, (((1,), (1,)), ((), ())), preferred_element_type=F32)


def _dot_tn(a, b):
    return lax.dot_general(a, b, (((0,), (0,)), ((), ())), preferred_element_type=F32)


def _order_mask(c, rev):
    t = lax.broadcasted_iota(jnp.int32, (c, c), 0)
    s = lax.broadcasted_iota(jnp.int32, (c, c), 1)
    return (s >= t) if rev else (s <= t)


def _cumsum_mm(mask_bf16, x):
    hi = x.astype(BF16)
    r1 = x - hi.astype(F32)
    mid = r1.astype(BF16)
    lo = (r1 - mid.astype(F32)).astype(BF16)
    return _dot(mask_bf16, hi) + _dot(mask_bf16, mid) + _dot(mask_bf16, lo)


def _in_proj_kernel(x_ref, g_ref, wbig_ref, wsm_ref, big_ref, sm_ref, xn_ref):
    @pl.when(pl.program_id(1) == 0)
    def _():
        xn = _rms(x_ref[...], g_ref[...]).astype(BF16)
        xn_ref[...] = xn
        sm_ref[...] = _dot(xn, wsm_ref[...])
    big_ref[...] = _dot(xn_ref[...], wbig_ref[...]).astype(BF16)


def _in_proj(x2, g, w_big, w_small, tm, tn):
    t, d = x2.shape
    n = w_big.shape[1]
    return pl.pallas_call(
        _in_proj_kernel,
        grid=(t // tm, n // tn),
        in_specs=[
            pl.BlockSpec((tm, d), lambda i, j: (i, 0)),
            pl.BlockSpec((1, d), lambda i, j: (0, 0)),
            pl.BlockSpec((d, tn), lambda i, j: (0, j)),
            pl.BlockSpec((d, LANES), lambda i, j: (0, 0)),
        ],
        out_specs=[
            pl.BlockSpec((tm, tn), lambda i, j: (i, j)),
            pl.BlockSpec((tm, LANES), lambda i, j: (i, 0)),
        ],
        out_shape=[jax.ShapeDtypeStruct((t, n), BF16), jax.ShapeDtypeStruct((t, LANES), F32)],
        scratch_shapes=[pltpu.VMEM((tm, d), BF16)],
        compiler_params=_cparams(("parallel", "arbitrary")),
        name="in_proj",
    )(x2, g, w_big, w_small)


def _gla_kernel(q_ref, k_ref, v_ref, sm_ref, wlr_ref, blr_ref, *rest, rev, final, c):
    if final:
        oprev_ref, gg_ref, gn_ref, o_ref, st_ref = rest
    else:
        o_ref, st_ref = rest
    sb = GLA_SUB

    @pl.when(pl.program_id(1) == 0)
    def _():
        st_ref[...] = jnp.zeros_like(st_ref)

    x = _dot(sm_ref[0].astype(BF16), wlr_ref[...]) + blr_ref[...]
    la = _log_sigmoid(x) * (1.0 / GLA_TAU)
    mask_b = jnp.where(_order_mask(c, rev), 1.0, 0.0).astype(BF16)
    b = _cumsum_mm(mask_b, la)
    last = 0 if rev else c - 1
    tot = b[last:last + 1, :]
    q = q_ref[0].astype(F32) * (GLA_DK ** -0.5)
    k = k_ref[0].astype(F32)
    q_in = (q * jnp.exp(b)).astype(BF16)
    k_dec = (k * jnp.exp(tot - b)).astype(BF16)
    e_tot = jnp.exp(tot)

    col = lax.broadcasted_iota(jnp.int32, (sb, c), 1)
    trow = lax.broadcasted_iota(jnp.int32, (sb, c), 0)

    for h in range(GLA_HEADS):
        ks = slice(h * GLA_DK, (h + 1) * GLA_DK)
        vs = slice(h * GLA_DV, (h + 1) * GLA_DV)
        vh = v_ref[0, :, vs]
        kh_b = k_ref[0, :, ks]
        st = st_ref[h]
        o_inter = _dot_nt(q_in[:, ks], st.astype(BF16))
        st_ref[h] = st * e_tot[:, ks] + _dot_tn(vh, k_dec[:, ks])

        bh, qh, kh = b[:, ks], q[:, ks], k[:, ks]
        rows = []
        for i in range(c // sb):
            r0 = i * sb
            ref_row = r0 + (sb - 1 if rev else 0)
            beta = bh[ref_row:ref_row + 1, :]
            bb, qb = bh[r0:r0 + sb], qh[r0:r0 + sb]
            qt = (qb * jnp.exp(bb - beta)).astype(BF16)
            kt = (kh * jnp.exp(jnp.minimum(beta - bh, 0.0))).astype(BF16)
            a_off = _dot_nt(qt, kt)
            slabs = [(qb * jnp.exp(jnp.minimum(bb - bb[s:s + 1, :], 0.0))).astype(BF16) for s in range(sb)]
            g = _dot_nt(jnp.concatenate(slabs, axis=0), kh_b)
            a_diag = jnp.zeros((sb, c), F32)
            for s in range(sb):
                a_diag = a_diag + jnp.where(col == r0 + s, g[s * sb:(s + 1) * sb], 0.0)
            tr = trow + r0
            if rev:
                off_mask = col >= r0 + sb
                diag_mask = (col >= tr) & (col < r0 + sb)
            else:
                off_mask = col < r0
                diag_mask = (col <= tr) & (col >= r0)
            rows.append(jnp.where(off_mask, a_off, 0.0) + jnp.where(diag_mask, a_diag, 0.0))
        a = jnp.concatenate(rows, axis=0).astype(BF16)
        o = o_inter + _dot(a, vh)
        if final:
            o = o + oprev_ref[0, :, vs]
            y = _rms(o, gn_ref[:, vs])
            gg = gg_ref[0, :, vs].astype(F32)
            o_ref[0, :, vs] = (y * (gg * jax.nn.sigmoid(gg))).astype(o_ref.dtype)
        else:
            o_ref[0, :, vs] = o.astype(o_ref.dtype)


def _gla_scan(big3, sm3, wlr, blr, rev, prev=None, gn=None):
    bsz, seq, _ = big3.shape
    c = GLA_CHUNK
    n = seq // c
    final = prev is not None

    def cm(ci):
        return (n - 1 - ci) if rev else ci

    in_specs = [
        pl.BlockSpec((1, c, GLA_K), lambda b, ci: (b, cm(ci), OFF_Q // GLA_K)),
        pl.BlockSpec((1, c, GLA_K), lambda b, ci: (b, cm(ci), OFF_K // GLA_K)),
        pl.BlockSpec((1, c, GLA_V), lambda b, ci: (b, cm(ci), OFF_V // GLA_V)),
        pl.BlockSpec((1, c, LANES), lambda b, ci: (b, cm(ci), 0)),
        pl.BlockSpec((LANES, GLA_K), lambda b, ci: (0, 0)),
        pl.BlockSpec((1, GLA_K), lambda b, ci: (0, 0)),
    ]
    args = [big3, big3, big3, sm3, wlr, blr]
    if final:
        in_specs += [
            pl.BlockSpec((1, c, GLA_V), lambda b, ci: (b, cm(ci), 0)),
            pl.BlockSpec((1, c, GLA_V), lambda b, ci: (b, cm(ci), OFF_GG // GLA_V)),
            pl.BlockSpec((1, GLA_V), lambda b, ci: (0, 0)),
        ]
        args += [prev, big3, gn]
    return pl.pallas_call(
        functools.partial(_gla_kernel, rev=rev, final=final, c=c),
        grid=(bsz, n),
        in_specs=in_specs,
        out_specs=pl.BlockSpec((1, c, GLA_V), lambda b, ci: (b, cm(ci), 0)),
        out_shape=jax.ShapeDtypeStruct((bsz, seq, GLA_V), BF16 if final else F32),
        scratch_shapes=[pltpu.VMEM((GLA_HEADS, GLA_DV, GLA_DK), F32)],
        compiler_params=_cparams(("parallel", "arbitrary")),
        name="gla_bwd" if rev else "gla_fwd",
    )(*args)


def _mlstm_pre_kernel(cur_ref, prev_ref, next_ref, cw_ref, cb_ref, wq_ref, wk_ref, wv_ref,
                      q_ref, k_ref, v_ref, *, tm):
    i = pl.program_id(2)
    cur_b = cur_ref[0]
    cur = cur_b.astype(F32)
    halo = CONV_WIDTH // 2
    prev = jnp.where(i > 0, prev_ref[0].astype(F32), 0.0)
    nxt = jnp.where(i < pl.num_programs(2) - 1, next_ref[0].astype(F32), 0.0)
    ext = jnp.concatenate([prev[CONV_HALO - 8:], cur, nxt[:8]], axis=0)
    acc = jnp.zeros_like(cur) + cb_ref[...]
    for w in range(CONV_WIDTH):
        off = 8 - halo + w
        acc = acc + ext[off:off + tm] * cw_ref[w:w + 1, :]
    xc = (acc * jax.nn.sigmoid(acc)).astype(BF16)
    q_ref[0] = _dot(xc, wq_ref[0]).astype(BF16)
    k_ref[0] = (_dot(xc, wk_ref[0]) * (MLSTM_DH ** -0.5)).astype(BF16)
    v_ref[0] = _dot(cur_b, wv_ref[0]).astype(BF16)


def _mlstm_pre(big3, cw, cb, wq, wk, wv, tm):
    bsz, seq, _ = big3.shape
    dh = MLSTM_DH
    nh = tm // CONV_HALO
    n_halo = seq // CONV_HALO
    c0 = OFF_MX // dh
    out = jax.ShapeDtypeStruct((bsz, seq, MLSTM_W), BF16)
    ospec = pl.BlockSpec((1, tm, dh), lambda b, h, i: (b, i, h))
    wspec = pl.BlockSpec((1, dh, dh), lambda b, h, i: (h, 0, 0))
    return pl.pallas_call(
        functools.partial(_mlstm_pre_kernel, tm=tm),
        grid=(bsz, MLSTM_HEADS, seq // tm),
        in_specs=[
            pl.BlockSpec((1, tm, dh), lambda b, h, i: (b, i, c0 + h)),
            pl.BlockSpec((1, CONV_HALO, dh), lambda b, h, i: (b, jnp.maximum(i * nh - 1, 0), c0 + h)),
            pl.BlockSpec((1, CONV_HALO, dh), lambda b, h, i: (b, jnp.minimum((i + 1) * nh, n_halo - 1), c0 + h)),
            pl.BlockSpec((8, dh), lambda b, h, i: (0, h)),
            pl.BlockSpec((1, dh), lambda b, h, i: (0, h)),
            wspec, wspec, wspec,
        ],
        out_specs=[ospec, ospec, ospec],
        out_shape=[out, out, out],
        compiler_params=_cparams(("parallel", "parallel", "parallel")),
        name="mlstm_pre",
    )(big3, big3, big3, cw, cb, wq, wk, wv)


def _mlstm_kernel(q_ref, k_ref, v_ref, sm_ref, gb_ref, *rest, rev, final, c):
    if final:
        hprev_ref, mz_ref, mn_ref, o_ref, c_ref, n_ref, m_ref = rest
    else:
        o_ref, c_ref, n_ref, m_ref = rest

    @pl.when(pl.program_id(1) == 0)
    def _():
        c_ref[...] = jnp.zeros_like(c_ref)
        n_ref[...] = jnp.zeros_like(n_ref)
        m_ref[...] = jnp.zeros_like(m_ref)

    g = sm_ref[0] + gb_ref[...]
    mask = _order_mask(c, rev)
    bcum = _cumsum_mm(jnp.where(mask, 1.0, 0.0).astype(BF16), _log_sigmoid(g))
    g_t = g.T
    b_t = bcum.T
    last = 0 if rev else c - 1
    dh = MLSTM_DH

    for h in range(MLSTM_HEADS):
        ci = SM_GATES + (2 * MLSTM_HEADS if rev else 0) + h
        cf = ci + MLSTM_HEADS
        hs = slice(h * dh, (h + 1) * dh)
        i_col, b_col = g[:, ci:ci + 1], bcum[:, cf:cf + 1]
        i_row, b_row = g_t[ci:ci + 1, :], b_t[cf:cf + 1, :]
        tot = b_col[last:last + 1, :]
        m_prev = m_ref[h, 0:1, 0:1]
        qh, kh, vh = q_ref[0, :, hs], k_ref[0, :, hs], v_ref[0, :, hs]
        c_st = c_ref[h]
        n_st = n_ref[h, 0:1, :]

        dmat = jnp.where(mask, b_col - b_row + i_row, -jnp.inf)
        inter_log = b_col + m_prev
        m_t = jnp.maximum(inter_log, jnp.max(dmat, axis=-1, keepdims=True))
        w_intra = jnp.exp(dmat - m_t)
        w_inter = jnp.exp(inter_log - m_t)
        s = _dot_nt(qh, kh) * w_intra
        num = w_inter * _dot(qh, c_st.astype(BF16)) + _dot(s.astype(BF16), vh)
        den = (w_inter * jnp.sum(qh.astype(F32) * n_st, axis=-1, keepdims=True)
               + jnp.sum(s, axis=-1, keepdims=True))
        hh = num / jnp.maximum(jnp.abs(den), jnp.exp(-m_t))

        upd_col = tot - b_col + i_col
        upd_row = tot - b_row + i_row
        m_new = jnp.maximum(tot + m_prev, jnp.max(upd_row, axis=-1, keepdims=True))
        w_old = jnp.exp(tot + m_prev - m_new)
        kw = kh.astype(F32) * jnp.exp(upd_col - m_new)
        c_ref[h] = w_old * c_st + _dot_tn(kw.astype(BF16), vh)
        n_ref[h] = jnp.broadcast_to(w_old * n_st + jnp.sum(kw, axis=0, keepdims=True), (8, dh))
        m_ref[h] = jnp.broadcast_to(m_new, (8, LANES))

        if final:
            hh = hh + hprev_ref[0, :, hs]
            y = _rms(hh, mn_ref[:, hs])
            o_ref[0, :, hs] = (y * jax.nn.sigmoid(mz_ref[0, :, hs].astype(F32))).astype(o_ref.dtype)
        else:
            o_ref[0, :, hs] = hh.astype(o_ref.dtype)


def _mlstm_scan(qm, km, vm, sm3, gbias, rev, prev=None, big3=None, mn=None):
    bsz, seq, w = qm.shape
    c = min(MLSTM_CHUNK, seq)
    n = seq // c
    final = prev is not None

    def cm(ci):
        return (n - 1 - ci) if rev else ci

    xspec = pl.BlockSpec((1, c, w), lambda b, ci: (b, cm(ci), 0))
    in_specs = [xspec, xspec, xspec,
                pl.BlockSpec((1, c, LANES), lambda b, ci: (b, cm(ci), 0)),
                pl.BlockSpec((1, LANES), lambda b, ci: (0, 0))]
    args = [qm, km, vm, sm3, gbias]
    if final:
        in_specs += [xspec,
                     pl.BlockSpec((1, c, w), lambda b, ci: (b, cm(ci), OFF_MZ // MLSTM_W)),
                     pl.BlockSpec((1, w), lambda b, ci: (0, 0))]
        args += [prev, big3, mn]
    return pl.pallas_call(
        functools.partial(_mlstm_kernel, rev=rev, final=final, c=c),
        grid=(bsz, n),
        in_specs=in_specs,
        out_specs=xspec,
        out_shape=jax.ShapeDtypeStruct((bsz, seq, w), BF16 if final else F32),
        scratch_shapes=[pltpu.VMEM((MLSTM_HEADS, MLSTM_DH, MLSTM_DH), F32),
                        pltpu.VMEM((MLSTM_HEADS, 8, MLSTM_DH), F32),
                        pltpu.VMEM((MLSTM_HEADS, 8, LANES), F32)],
        compiler_params=_cparams(("parallel", "arbitrary")),
        name="mlstm_bwd" if rev else "mlstm_fwd",
    )(*args)


def _merge_kernel(ya_ref, yb_ref, wa_ref, wb_ref, ga_ref, gb_ref, o_ref):
    a = _dot(ya_ref[...], wa_ref[...])
    b = _dot(yb_ref[...], wb_ref[...])
    ga = jax.nn.sigmoid(ga_ref[...].astype(F32))
    gb = jax.nn.sigmoid(gb_ref[...].astype(F32))
    o_ref[...] = (ga * a + gb * b).astype(o_ref.dtype)


def _merge(ya, yb, wa, wb, big, tm, tn):
    t, kdim = ya.shape
    n = wa.shape[1]
    return pl.pallas_call(
        _merge_kernel,
        grid=(t // tm, n // tn),
        in_specs=[
            pl.BlockSpec((tm, kdim), lambda i, j: (i, 0)),
            pl.BlockSpec((tm, kdim), lambda i, j: (i, 0)),
            pl.BlockSpec((kdim, tn), lambda i, j: (0, j)),
            pl.BlockSpec((kdim, tn), lambda i, j: (0, j)),
            pl.BlockSpec((tm, tn), lambda i, j: (i, OFF_GA // tn + j)),
            pl.BlockSpec((tm, tn), lambda i, j: (i, OFF_GB // tn + j)),
        ],
        out_specs=pl.BlockSpec((tm, tn), lambda i, j: (i, j)),
        out_shape=jax.ShapeDtypeStruct((t, n), BF16),
        compiler_params=_cparams(("parallel", "parallel")),
        name="merge",
    )(ya, yb, wa, wb, big, big)


def _matmul_res_kernel(a_ref, w_ref, r_ref, o_ref):
    o_ref[...] = r_ref[...] + _dot(a_ref[...], w_ref[...])


def _matmul_res(a, w, res, tm, tn):
    t, kdim = a.shape
    n = w.shape[1]
    return pl.pallas_call(
        _matmul_res_kernel,
        grid=(t // tm, n // tn),
        in_specs=[
            pl.BlockSpec((tm, kdim), lambda i, j: (i, 0)),
            pl.BlockSpec((kdim, tn), lambda i, j: (0, j)),
            pl.BlockSpec((tm, tn), lambda i, j: (i, j)),
        ],
        out_specs=pl.BlockSpec((tm, tn), lambda i, j: (i, j)),
        out_shape=jax.ShapeDtypeStruct((t, n), F32),
        compiler_params=_cparams(("parallel", "parallel")),
        name="mix_out",
    )(a, w, res)


def _kv_kernel(mem_ref, g_ref, w_ref, o_ref):
    mn = _rms(mem_ref[...], g_ref[...]).astype(BF16)
    o_ref[...] = _dot(mn, w_ref[...]).astype(BF16)


def _kv_proj(mem2, g, w, tn):
    r, d = mem2.shape
    n = w.shape[1]
    return pl.pallas_call(
        _kv_kernel,
        grid=(n // tn,),
        in_specs=[
            pl.BlockSpec((r, d), lambda j: (0, 0)),
            pl.BlockSpec((1, d), lambda j: (0, 0)),
            pl.BlockSpec((d, tn), lambda j: (0, j)),
        ],
        out_specs=pl.BlockSpec((r, tn), lambda j: (0, j)),
        out_shape=jax.ShapeDtypeStruct((r, n), BF16),
        compiler_params=_cparams(("parallel",)),
        name="kv_proj",
    )(mem2, g, w)


def _xattn_kernel(h_ref, g_ref, wq_ref, k_ref, v_ref, wo_ref, o_ref, xn_ref):
    @pl.when(pl.program_id(1) == 0)
    def _():
        x = h_ref[...]
        xn_ref[...] = _rms(x, g_ref[...]).astype(BF16)
        o_ref[...] = x
    q = _dot(xn_ref[...], wq_ref[...]).astype(BF16)
    s = _dot_nt(q, k_ref[0]) * (XATTN_DH ** -0.5)
    p = jnp.exp(s - jnp.max(s, axis=-1, keepdims=True))
    p = p / jnp.sum(p, axis=-1, keepdims=True)
    o = _dot(p.astype(BF16), v_ref[0]).astype(BF16)
    o_ref[...] += _dot(o, wo_ref[...])


def _xattn(h1, g, wq, kv3, wo, seq, tm):
    t, d = h1.shape
    n_mem = kv3.shape[1]
    dh = XATTN_DH
    per_b = seq // tm
    return pl.pallas_call(
        _xattn_kernel,
        grid=(t // tm, XATTN_HEADS),
        in_specs=[
            pl.BlockSpec((tm, d), lambda i, h: (i, 0)),
            pl.BlockSpec((1, d), lambda i, h: (0, 0)),
            pl.BlockSpec((d, dh), lambda i, h: (0, h)),
            pl.BlockSpec((1, n_mem, dh), lambda i, h: (i // per_b, 0, h)),
            pl.BlockSpec((1, n_mem, dh), lambda i, h: (i // per_b, 0, XATTN_HEADS + h)),
            pl.BlockSpec((dh, d), lambda i, h: (h, 0)),
        ],
        out_specs=pl.BlockSpec((tm, d), lambda i, h: (i, 0)),
        out_shape=jax.ShapeDtypeStruct((t, d), F32),
        scratch_shapes=[pltpu.VMEM((tm, d), BF16)],
        compiler_params=_cparams(("parallel", "arbitrary")),
        name="xattn",
    )(h1, g, wq, kv3, kv3, wo)


def _router_kernel(h_ref, g_ref, wr_ref, br_ref, xn_ref, ids_ref, wts_ref):
    xn = _rms(h_ref[...], g_ref[...])
    xn_ref[...] = xn
    hi = xn.astype(BF16)
    lo = (xn - hi.astype(F32)).astype(BF16)
    lg = _dot(hi, wr_ref[0]) + _dot(hi, wr_ref[1]) + _dot(lo, wr_ref[0]) + br_ref[...]
    lane = lax.broadcasted_iota(jnp.int32, lg.shape, 1).astype(F32)
    ninf = -jnp.inf
    big_lane = float(LANES)

    def first_max(v):
        mx = jnp.max(v, axis=-1, keepdims=True)
        return mx, jnp.min(jnp.where(v == mx, lane, big_lane), axis=-1, keepdims=True)

    gl = jnp.where(lane < N_GROUPS, lg, ninf)
    gmax, g_sel = first_max(gl)
    p_g = 1.0 / jnp.sum(jnp.exp(gl - gmax), axis=-1, keepdims=True)
    lo_l = N_GROUPS + EXPERTS_PER_GROUP * g_sel
    el = jnp.where((lane >= lo_l) & (lane < lo_l + EXPERTS_PER_GROUP), lg, ninf)
    emax, i1 = first_max(el)
    max2, i2 = first_max(jnp.where(lane == i1, ninf, el))
    e2 = jnp.exp(max2 - emax)
    w1 = p_g / (1.0 + e2)
    w2 = p_g * e2 / (1.0 + e2)
    ids = jnp.where(lane == 0.0, i1 - N_GROUPS, jnp.where(lane == 1.0, i2 - N_GROUPS, 0.0))
    ids_ref[...] = ids.astype(jnp.int32)
    wts_ref[...] = jnp.where(lane == 0.0, w1, jnp.where(lane == 1.0, w2, 0.0))


def _router(h2, g, wr, br, tm):
    t, d = h2.shape
    return pl.pallas_call(
        _router_kernel,
        grid=(t // tm,),
        in_specs=[
            pl.BlockSpec((tm, d), lambda i: (i, 0)),
            pl.BlockSpec((1, d), lambda i: (0, 0)),
            pl.BlockSpec((2, d, LANES), lambda i: (0, 0, 0)),
            pl.BlockSpec((1, LANES), lambda i: (0, 0)),
        ],
        out_specs=[
            pl.BlockSpec((tm, d), lambda i: (i, 0)),
            pl.BlockSpec((tm, LANES), lambda i: (i, 0)),
            pl.BlockSpec((tm, LANES), lambda i: (i, 0)),
        ],
        out_shape=[jax.ShapeDtypeStruct((t, d), F32),
                   jax.ShapeDtypeStruct((t, LANES), jnp.int32),
                   jax.ShapeDtypeStruct((t, LANES), F32)],
        compiler_params=_cparams(("parallel",)),
        name="router",
    )(h2, g, wr, br)


def _row_copy(src_hbm, dst_vmem, sem, src_row, dst_row):
    return pltpu.make_async_copy(src_hbm.at[pl.ds(src_row, 1)], dst_vmem.at[pl.ds(dst_row, 1)], sem)


def _experts_kernel(blk_e_ref, n_used_ref, tok_ref, x_hbm, wg_ref, wu_ref, wd_ref, y_ref, xbuf, sem, *, rows):
    del blk_e_ref
    used = pl.program_id(0) < n_used_ref[0]

    @pl.when(jnp.logical_not(used))
    def _():
        y_ref[...] = jnp.zeros_like(y_ref)

    @pl.when(used)
    def _():
        def start(r, carry):
            _row_copy(x_hbm, xbuf, sem, tok_ref[0, 0, r], r).start()
            return carry

        def wait(r, carry):
            _row_copy(x_hbm, xbuf, sem, 0, r).wait()
            return carry

        lax.fori_loop(0, rows, start, 0)
        lax.fori_loop(0, rows, wait, 0)
        xb = xbuf[...].astype(BF16)
        gate = _dot(xb, wg_ref[0])
        up = _dot(xb, wu_ref[0])
        act = (gate * jax.nn.sigmoid(gate) * up).astype(BF16)
        y_ref[...] = _dot(act, wd_ref[0])


def _experts(blk_e, n_used, row_tok3, xn, wg, wu, wd):
    n_blocks, _, rows = row_tok3.shape
    d = xn.shape[1]
    de = wg.shape[2]
    grid_spec = pltpu.PrefetchScalarGridSpec(
        num_scalar_prefetch=2,
        grid=(n_blocks,),
        in_specs=[
            pl.BlockSpec((1, 1, rows), lambda j, be, nu: (j, 0, 0), memory_space=pltpu.SMEM),
            pl.BlockSpec(memory_space=pl.ANY),
            pl.BlockSpec((1, d, de), lambda j, be, nu: (be[j], 0, 0)),
            pl.BlockSpec((1, d, de), lambda j, be, nu: (be[j], 0, 0)),
            pl.BlockSpec((1, de, d), lambda j, be, nu: (be[j], 0, 0)),
        ],
        out_specs=pl.BlockSpec((rows, d), lambda j, be, nu: (j, 0)),
        scratch_shapes=[pltpu.VMEM((rows, d), F32), pltpu.SemaphoreType.DMA(())],
    )
    return pl.pallas_call(
        functools.partial(_experts_kernel, rows=rows),
        grid_spec=grid_spec,
        out_shape=jax.ShapeDtypeStruct((n_blocks * rows, d), F32),
        compiler_params=_cparams(("arbitrary",)),
        name="experts",
    )(blk_e, n_used, row_tok3, xn, wg, wu, wd)


def _combine_kernel(pos_ref, y_hbm, wts_ref, h_ref, g_ref, o_ref, ybuf, sem, *, tc):
    def start(r, carry):
        _row_copy(y_hbm, ybuf, sem, pos_ref[0, 0, 2 * r], r).start()
        _row_copy(y_hbm, ybuf, sem, pos_ref[0, 0, 2 * r + 1], tc + r).start()
        return carry

    def wait(r, carry):
        _row_copy(y_hbm, ybuf, sem, 0, r).wait()
        return carry

    lax.fori_loop(0, tc, start, 0)
    lax.fori_loop(0, 2 * tc, wait, 0)
    w = wts_ref[...]
    y = ybuf[0:tc, :] * w[:, 0:1] + ybuf[tc:2 * tc, :] * w[:, 1:2]
    o_ref[...] = _rms(h_ref[...] + y, g_ref[...])


def _combine(pos3, y_rows, wts, h2, g, tc):
    t, d = h2.shape
    return pl.pallas_call(
        functools.partial(_combine_kernel, tc=tc),
        grid=(t // tc,),
        in_specs=[
            pl.BlockSpec((1, 1, 2 * tc), lambda i: (i, 0, 0), memory_space=pltpu.SMEM),
            pl.BlockSpec(memory_space=pl.ANY),
            pl.BlockSpec((tc, LANES), lambda i: (i, 0)),
            pl.BlockSpec((tc, d), lambda i: (i, 0)),
            pl.BlockSpec((1, d), lambda i: (0, 0)),
        ],
        out_specs=pl.BlockSpec((tc, d), lambda i: (i, 0)),
        out_shape=jax.ShapeDtypeStruct((t, d), F32),
        scratch_shapes=[pltpu.VMEM((2 * tc, d), F32), pltpu.SemaphoreType.DMA(())],
        compiler_params=_cparams(("arbitrary",)),
        name="combine",
    )(pos3, y_rows, wts, h2, g)


def _dispatch_plan(ids, n_tok, rows):
    flat_e = ids.reshape(-1)
    n_asg = flat_e.shape[0]
    onehot = (flat_e[:, None] == jnp.arange(N_EXPERTS, dtype=jnp.int32)[None, :]).astype(jnp.int32)
    csum = jnp.cumsum(onehot, axis=0)
    rank = jnp.take_along_axis(csum, flat_e[:, None], axis=1)[:, 0] - 1
    counts = csum[-1]
    padded = ((counts + rows - 1) // rows) * rows
    pends = jnp.cumsum(padded)
    pos = (pends - padded)[flat_e] + rank
    n_rows = n_asg + N_EXPERTS * rows
    n_blocks = n_rows // rows
    row_tok = jnp.zeros((n_rows,), jnp.int32).at[pos].set(jnp.arange(n_asg, dtype=jnp.int32) // 2)
    blk_e = jnp.clip(jnp.searchsorted(pends, jnp.arange(n_blocks, dtype=jnp.int32) * rows, side='right'),
                     0, N_EXPERTS - 1).astype(jnp.int32)
    n_used = (pends[-1] // rows).astype(jnp.int32).reshape(1)
    return pos.astype(jnp.int32), row_tok.reshape(n_blocks, 1, rows), blk_e, n_used


def _tile(n, pref):
    return pref if n % pref == 0 else n


def _layer(h, mem, p, l):
    bsz, seq, d = h.shape
    t = bsz * seq
    x2 = h.reshape(t, d)
    row = lambda v: v.reshape(1, -1)
    tm = _tile(t, 1024)

    w_in = p['w_in'][l]
    w_big = jnp.concatenate([w_in[:, 0:3072], w_in[:, 3104:5152], w_in[:, 5168:9264]], axis=1).astype(BF16)
    w_small = jnp.concatenate([w_in[:, 3072:3104], w_in[:, 5152:5168],
                               jnp.zeros((d, LANES - 48), F32)], axis=1).astype(BF16)
    big, small = _in_proj(x2, row(p['norm_mix'][l]), w_big, w_small, tm, 1024)
    big3 = big.reshape(bsz, seq, N_BIG)
    sm3 = small.reshape(bsz, seq, LANES)

    def lr_pad(w, off):
        return jnp.zeros((LANES, GLA_K), F32).at[off:off + GLA_RANK].set(w).astype(BF16)

    o_f = _gla_scan(big3, sm3, lr_pad(p['gla_w_lr_f'][l], SM_LRF), row(p['gla_b_lr_f'][l]), rev=False)
    y_a = _gla_scan(big3, sm3, lr_pad(p['gla_w_lr_b'][l], SM_LRB), row(p['gla_b_lr_b'][l]), rev=True,
                    prev=o_f, gn=row(p['gla_norm'][l]))

    cw = jnp.zeros((8, MLSTM_W), F32).at[:CONV_WIDTH].set(p['conv_w'][l].reshape(CONV_WIDTH, MLSTM_W))
    qm, km, vm = _mlstm_pre(big3, cw, row(p['conv_b'][l]), p['m_wq'][l].astype(BF16),
                            p['m_wk'][l].astype(BF16), p['m_wv'][l].astype(BF16), _tile(seq, 1024))
    gbias = jnp.zeros((1, LANES), F32).at[0, SM_GATES:SM_GATES + 4 * MLSTM_HEADS].set(
        p['m_gate_bias'][l].reshape(-1))
    h_f = _mlstm_scan(qm, km, vm, sm3, gbias, rev=False)
    y_b = _mlstm_scan(qm, km, vm, sm3, gbias, rev=True, prev=h_f, big3=big3, mn=row(p['m_norm'][l]))

    merged = _merge(y_a.reshape(t, GLA_V), y_b.reshape(t, MLSTM_W), p['w_branch_a'][l].astype(BF16),
                    p['w_branch_b'][l].astype(BF16), big, tm, 1024)
    h1 = _matmul_res(merged, p['w_mix_out'][l].astype(BF16), x2, tm, 1024)

    n_mem = mem.shape[1]
    kv = _kv_proj(mem.reshape(bsz * n_mem, d), row(p['norm_mem'][l]), p['w_xkv'][l].astype(BF16), 1024)
    h2 = _xattn(h1, row(p['norm_xattn'][l]), p['w_xq'][l].astype(BF16), kv.reshape(bsz, n_mem, 2 * d),
                p['w_xo'][l].astype(BF16), seq, _tile(seq, 512))

    wr = jnp.concatenate([p['w_group'][l], p['w_router'][l].transpose(1, 0, 2).reshape(d, N_EXPERTS),
                          jnp.zeros((d, LANES - N_GROUPS - N_EXPERTS), F32)], axis=1)
    wr_hi = wr.astype(BF16)
    wr2 = jnp.stack([wr_hi, (wr - wr_hi.astype(F32)).astype(BF16)])
    br = jnp.concatenate([p['b_group'][l], p['b_router'][l].reshape(-1),
                          jnp.zeros((LANES - N_GROUPS - N_EXPERTS,), F32)]).reshape(1, LANES)
    xn3, ids, wts = _router(h2, row(p['norm_ffn'][l]), wr2, br, tm)
    pos, row_tok3, blk_e, n_used = _dispatch_plan(ids[:, :2], t, MOE_ROWS)
    y_rows = _experts(blk_e, n_used, row_tok3, xn3, p['w_gate'][l].astype(BF16), p['w_up'][l].astype(BF16),
                      p['w_down'][l].astype(BF16))
    tc = _tile(t, COMBINE_TOK)
    return pos.reshape(t // tc, 1, 2 * tc), y_rows, wts, h2, tc


def kernel(x, mem, norm_mix, w_in, gla_w_lr_f, gla_b_lr_f, gla_w_lr_b, gla_b_lr_b, gla_norm, conv_w, conv_b, m_wq, m_wk, m_wv, m_gate_bias, m_norm, w_branch_a, w_branch_b, w_mix_out, norm_xattn, norm_mem, w_xq, w_xkv, w_xo, norm_ffn, w_group, b_group, w_router, b_router, w_gate, w_up, w_down, norm_final):
    p = dict(norm_mix=norm_mix, w_in=w_in, gla_w_lr_f=gla_w_lr_f, gla_b_lr_f=gla_b_lr_f, gla_w_lr_b=gla_w_lr_b,
             gla_b_lr_b=gla_b_lr_b, gla_norm=gla_norm, conv_w=conv_w, conv_b=conv_b, m_wq=m_wq, m_wk=m_wk,
             m_wv=m_wv, m_gate_bias=m_gate_bias, m_norm=m_norm, w_branch_a=w_branch_a, w_branch_b=w_branch_b,
             w_mix_out=w_mix_out, norm_xattn=norm_xattn, norm_mem=norm_mem, w_xq=w_xq, w_xkv=w_xkv, w_xo=w_xo,
             norm_ffn=norm_ffn, w_group=w_group, b_group=b_group, w_router=w_router, b_router=b_router,
             w_gate=w_gate, w_up=w_up, w_down=w_down)
    bsz, seq, d = x.shape
    depth = norm_mix.shape[0]
    assert depth == 1, "the final norm is fused into the last layer's combine step"
    pos3, y_rows, wts, h2, tc = _layer(x, mem, p, 0)
    out = _combine(pos3, y_rows, wts, h2, norm_final.reshape(1, d), tc)
    return out.reshape(bsz, seq, d)
```

```python
import functools

import jax
import jax.numpy as jnp
from jax import lax
from jax.experimental import pallas as pl
from jax.experimental.pallas import tpu as pltpu

F32 = jnp.float32
BF16 = jnp.bfloat16

EPS = 1e-6
D_MODEL = 2048

GLA_HEADS = 4
GLA_DK = 128
GLA_DV = 256
GLA_K = GLA_HEADS * GLA_DK
GLA_V = GLA_HEADS * GLA_DV
GLA_RANK = 16
GLA_TAU = 16.0
GLA_CHUNK = 64
GLA_SUB = 16

MLSTM_HEADS = 4
MLSTM_DH = 256
MLSTM_W = MLSTM_HEADS * MLSTM_DH
CONV_WIDTH = 5
MLSTM_CHUNK = 256
CONV_HALO = 16

XATTN_HEADS = 4
XATTN_DH = D_MODEL // XATTN_HEADS

N_GROUPS = 4
EXPERTS_PER_GROUP = 8
N_EXPERTS = N_GROUPS * EXPERTS_PER_GROUP
D_EXPERT = 512
MOE_ROWS = 256
COMBINE_TOK = 256

LANES = 128
SLAB_W = LANES
SLAB = D_MODEL // SLAB_W
SLAB_PITCH = SLAB + 1

OFF_Q, OFF_K, OFF_V, OFF_GG = 0, 512, 1024, 2048
OFF_MX, OFF_MZ, OFF_GA, OFF_GB = 3072, 4096, 5120, 7168
N_BIG = 9216
SM_LRF, SM_LRB, SM_GATES = 0, 16, 32

VMEM_LIMIT = 56 * 1024 * 1024


def _cparams(sem):
    return pltpu.CompilerParams(dimension_semantics=sem, vmem_limit_bytes=VMEM_LIMIT)


def _rms(x, g):
    return x * lax.rsqrt(jnp.mean(x * x, axis=-1, keepdims=True) + EPS) * g


def _log_sigmoid(x):
    return jnp.minimum(x, 0.0) - jnp.log(1.0 + jnp.exp(-jnp.abs(x)))


def _dot(a, b):
    return jnp.dot(a, b, preferred_element_type=F32)


def _dot_nt(a, b):
    return lax.dot_general(a, b, (((1,), (1,)), ((), ())), preferred_element_type=F32)


def _dot_tn(a, b):
    return lax.dot_general(a, b, (((0,), (0,)), ((), ())), preferred_element_type=F32)


def _order_mask(c, rev):
    t = lax.broadcasted_iota(jnp.int32, (c, c), 0)
    s = lax.broadcasted_iota(jnp.int32, (c, c), 1)
    return (s >= t) if rev else (s <= t)


def _cumsum_mm(mask_bf16, x):
    hi = x.astype(BF16)
    r1 = x - hi.astype(F32)
    mid = r1.astype(BF16)
    lo = (r1 - mid.astype(F32)).astype(BF16)
    return _dot(mask_bf16, hi) + _dot(mask_bf16, mid) + _dot(mask_bf16, lo)


def _in_proj_kernel(x_ref, g_ref, wbig_ref, wsm_ref, big_ref, sm_ref, xn_ref):
    @pl.when(pl.program_id(1) == 0)
    def _():
        xn = _rms(x_ref[...], g_ref[...]).astype(BF16)
        xn_ref[...] = xn
        sm_ref[...] = _dot(xn, wsm_ref[...])
    big_ref[...] = _dot(xn_ref[...], wbig_ref[...]).astype(BF16)


def _in_proj(x2, g, w_big, w_small, tm, tn):
    t, d = x2.shape
    n = w_big.shape[1]
    return pl.pallas_call(
        _in_proj_kernel,
        grid=(t // tm, n // tn),
        in_specs=[
            pl.BlockSpec((tm, d), lambda i, j: (i, 0)),
            pl.BlockSpec((1, d), lambda i, j: (0, 0)),
            pl.BlockSpec((d, tn), lambda i, j: (0, j)),
            pl.BlockSpec((d, LANES), lambda i, j: (0, 0)),
        ],
        out_specs=[
            pl.BlockSpec((tm, tn), lambda i, j: (i, j)),
            pl.BlockSpec((tm, LANES), lambda i, j: (i, 0)),
        ],
        out_shape=[jax.ShapeDtypeStruct((t, n), BF16), jax.ShapeDtypeStruct((t, LANES), F32)],
        scratch_shapes=[pltpu.VMEM((tm, d), BF16)],
        compiler_params=_cparams(("parallel", "arbitrary")),
        name="in_proj",
    )(x2, g, w_big, w_small)


def _gla_kernel(q_ref, k_ref, v_ref, sm_ref, wlr_ref, blr_ref, *rest, rev, final, c):
    if final:
        oprev_ref, gg_ref, gn_ref, o_ref, st_ref = rest
    else:
        o_ref, st_ref = rest
    sb = GLA_SUB

    @pl.when(pl.program_id(1) == 0)
    def _():
        st_ref[...] = jnp.zeros_like(st_ref)

    x = _dot(sm_ref[0].astype(BF16), wlr_ref[...]) + blr_ref[...]
    la = _log_sigmoid(x) * (1.0 / GLA_TAU)
    mask_b = jnp.where(_order_mask(c, rev), 1.0, 0.0).astype(BF16)
    b = _cumsum_mm(mask_b, la)
    last = 0 if rev else c - 1
    tot = b[last:last + 1, :]
    q = q_ref[0].astype(F32) * (GLA_DK ** -0.5)
    k = k_ref[0].astype(F32)
    q_in = (q * jnp.exp(b)).astype(BF16)
    k_dec = (k * jnp.exp(tot - b)).astype(BF16)
    e_tot = jnp.exp(tot)

    col = lax.broadcasted_iota(jnp.int32, (sb, c), 1)
    trow = lax.broadcasted_iota(jnp.int32, (sb, c), 0)

    for h in range(GLA_HEADS):
        ks = slice(h * GLA_DK, (h + 1) * GLA_DK)
        vs = slice(h * GLA_DV, (h + 1) * GLA_DV)
        vh = v_ref[0, :, vs]
        kh_b = k_ref[0, :, ks]
        st = st_ref[h]
        o_inter = _dot_nt(q_in[:, ks], st.astype(BF16))
        st_ref[h] = st * e_tot[:, ks] + _dot_tn(vh, k_dec[:, ks])

        bh, qh, kh = b[:, ks], q[:, ks], k[:, ks]
        rows = []
        for i in range(c // sb):
            r0 = i * sb
            ref_row = r0 + (sb - 1 if rev else 0)
            beta = bh[ref_row:ref_row + 1, :]
            bb, qb = bh[r0:r0 + sb], qh[r0:r0 + sb]
            qt = (qb * jnp.exp(bb - beta)).astype(BF16)
            kt = (kh * jnp.exp(jnp.minimum(beta - bh, 0.0))).astype(BF16)
            a_off = _dot_nt(qt, kt)
            slabs = [(qb * jnp.exp(jnp.minimum(bb - bb[s:s + 1, :], 0.0))).astype(BF16) for s in range(sb)]
            g = _dot_nt(jnp.concatenate(slabs, axis=0), kh_b)
            a_diag = jnp.zeros((sb, c), F32)
            for s in range(sb):
                a_diag = a_diag + jnp.where(col == r0 + s, g[s * sb:(s + 1) * sb], 0.0)
            tr = trow + r0
            if rev:
                off_mask = col >= r0 + sb
                diag_mask = (col >= tr) & (col < r0 + sb)
            else:
                off_mask = col < r0
                diag_mask = (col <= tr) & (col >= r0)
            rows.append(jnp.where(off_mask, a_off, 0.0) + jnp.where(diag_mask, a_diag, 0.0))
        a = jnp.concatenate(rows, axis=0).astype(BF16)
        o = o_inter + _dot(a, vh)
        if final:
            o = o + oprev_ref[0, :, vs]
            y = _rms(o, gn_ref[:, vs])
            gg = gg_ref[0, :, vs].astype(F32)
            o_ref[0, :, vs] = (y * (gg * jax.nn.sigmoid(gg))).astype(o_ref.dtype)
        else:
            o_ref[0, :, vs] = o.astype(o_ref.dtype)


def _gla_scan(big3, sm3, wlr, blr, rev, prev=None, gn=None):
    bsz, seq, _ = big3.shape
    c = GLA_CHUNK
    n = seq // c
    final = prev is not None

    def cm(ci):
        return (n - 1 - ci) if rev else ci

    in_specs = [
        pl.BlockSpec((1, c, GLA_K), lambda b, ci: (b, cm(ci), OFF_Q // GLA_K)),
        pl.BlockSpec((1, c, GLA_K), lambda b, ci: (b, cm(ci), OFF_K // GLA_K)),
        pl.BlockSpec((1, c, GLA_V), lambda b, ci: (b, cm(ci), OFF_V // GLA_V)),
        pl.BlockSpec((1, c, LANES), lambda b, ci: (b, cm(ci), 0)),
        pl.BlockSpec((LANES, GLA_K), lambda b, ci: (0, 0)),
        pl.BlockSpec((1, GLA_K), lambda b, ci: (0, 0)),
    ]
    args = [big3, big3, big3, sm3, wlr, blr]
    if final:
        in_specs += [
            pl.BlockSpec((1, c, GLA_V), lambda b, ci: (b, cm(ci), 0)),
            pl.BlockSpec((1, c, GLA_V), lambda b, ci: (b, cm(ci), OFF_GG // GLA_V)),
            pl.BlockSpec((1, GLA_V), lambda b, ci: (0, 0)),
        ]
        args += [prev, big3, gn]
    return pl.pallas_call(
        functools.partial(_gla_kernel, rev=rev, final=final, c=c),
        grid=(bsz, n),
        in_specs=in_specs,
        out_specs=pl.BlockSpec((1, c, GLA_V), lambda b, ci: (b, cm(ci), 0)),
        out_shape=jax.ShapeDtypeStruct((bsz, seq, GLA_V), BF16 if final else F32),
        scratch_shapes=[pltpu.VMEM((GLA_HEADS, GLA_DV, GLA_DK), F32)],
        compiler_params=_cparams(("parallel", "arbitrary")),
        name="gla_bwd" if rev else "gla_fwd",
    )(*args)


def _mlstm_pre_kernel(cur_ref, prev_ref, next_ref, cw_ref, cb_ref, wq_ref, wk_ref, wv_ref,
                      q_ref, k_ref, v_ref, *, tm):
    i = pl.program_id(2)
    cur_b = cur_ref[0]
    cur = cur_b.astype(F32)
    halo = CONV_WIDTH // 2
    prev = jnp.where(i > 0, prev_ref[0].astype(F32), 0.0)
    nxt = jnp.where(i < pl.num_programs(2) - 1, next_ref[0].astype(F32), 0.0)
    ext = jnp.concatenate([prev[CONV_HALO - 8:], cur, nxt[:8]], axis=0)
    acc = jnp.zeros_like(cur) + cb_ref[...]
    for w in range(CONV_WIDTH):
        off = 8 - halo + w
        acc = acc + ext[off:off + tm] * cw_ref[w:w + 1, :]
    xc = (acc * jax.nn.sigmoid(acc)).astype(BF16)
    q_ref[0] = _dot(xc, wq_ref[0]).astype(BF16)
    k_ref[0] = (_dot(xc, wk_ref[0]) * (MLSTM_DH ** -0.5)).astype(BF16)
    v_ref[0] = _dot(cur_b, wv_ref[0]).astype(BF16)


def _mlstm_pre(big3, cw, cb, wq, wk, wv, tm):
    bsz, seq, _ = big3.shape
    dh = MLSTM_DH
    nh = tm // CONV_HALO
    n_halo = seq // CONV_HALO
    c0 = OFF_MX // dh
    out = jax.ShapeDtypeStruct((bsz, seq, MLSTM_W), BF16)
    ospec = pl.BlockSpec((1, tm, dh), lambda b, h, i: (b, i, h))
    wspec = pl.BlockSpec((1, dh, dh), lambda b, h, i: (h, 0, 0))
    return pl.pallas_call(
        functools.partial(_mlstm_pre_kernel, tm=tm),
        grid=(bsz, MLSTM_HEADS, seq // tm),
        in_specs=[
            pl.BlockSpec((1, tm, dh), lambda b, h, i: (b, i, c0 + h)),
            pl.BlockSpec((1, CONV_HALO, dh), lambda b, h, i: (b, jnp.maximum(i * nh - 1, 0), c0 + h)),
            pl.BlockSpec((1, CONV_HALO, dh), lambda b, h, i: (b, jnp.minimum((i + 1) * nh, n_halo - 1), c0 + h)),
            pl.BlockSpec((8, dh), lambda b, h, i: (0, h)),
            pl.BlockSpec((1, dh), lambda b, h, i: (0, h)),
            wspec, wspec, wspec,
        ],
        out_specs=[ospec, ospec, ospec],
        out_shape=[out, out, out],
        compiler_params=_cparams(("parallel", "parallel", "parallel")),
        name="mlstm_pre",
    )(big3, big3, big3, cw, cb, wq, wk, wv)


def _mlstm_kernel(q_ref, k_ref, v_ref, sm_ref, gb_ref, *rest, rev, final, c):
    if final:
        hprev_ref, mz_ref, mn_ref, o_ref, c_ref, n_ref, m_ref = rest
    else:
        o_ref, c_ref, n_ref, m_ref = rest

    @pl.when(pl.program_id(1) == 0)
    def _():
        c_ref[...] = jnp.zeros_like(c_ref)
        n_ref[...] = jnp.zeros_like(n_ref)
        m_ref[...] = jnp.zeros_like(m_ref)

    g = sm_ref[0] + gb_ref[...]
    mask = _order_mask(c, rev)
    bcum = _cumsum_mm(jnp.where(mask, 1.0, 0.0).astype(BF16), _log_sigmoid(g))
    g_t = g.T
    b_t = bcum.T
    last = 0 if rev else c - 1
    dh = MLSTM_DH

    for h in range(MLSTM_HEADS):
        ci = SM_GATES + (2 * MLSTM_HEADS if rev else 0) + h
        cf = ci + MLSTM_HEADS
        hs = slice(h * dh, (h + 1) * dh)
        i_col, b_col = g[:, ci:ci + 1], bcum[:, cf:cf + 1]
        i_row, b_row = g_t[ci:ci + 1, :], b_t[cf:cf + 1, :]
        tot = b_col[last:last + 1, :]
        m_prev = m_ref[h, 0:1, 0:1]
        qh, kh, vh = q_ref[0, :, hs], k_ref[0, :, hs], v_ref[0, :, hs]
        c_st = c_ref[h]
        n_st = n_ref[h, 0:1, :]

        dmat = jnp.where(mask, b_col - b_row + i_row, -jnp.inf)
        inter_log = b_col + m_prev
        m_t = jnp.maximum(inter_log, jnp.max(dmat, axis=-1, keepdims=True))
        w_intra = jnp.exp(dmat - m_t)
        w_inter = jnp.exp(inter_log - m_t)
        s = _dot_nt(qh, kh) * w_intra
        num = w_inter * _dot(qh, c_st.astype(BF16)) + _dot(s.astype(BF16), vh)
        den = (w_inter * jnp.sum(qh.astype(F32) * n_st, axis=-1, keepdims=True)
               + jnp.sum(s, axis=-1, keepdims=True))
        hh = num / jnp.maximum(jnp.abs(den), jnp.exp(-m_t))

        upd_col = tot - b_col + i_col
        upd_row = tot - b_row + i_row
        m_new = jnp.maximum(tot + m_prev, jnp.max(upd_row, axis=-1, keepdims=True))
        w_old = jnp.exp(tot + m_prev - m_new)
        kw = kh.astype(F32) * jnp.exp(upd_col - m_new)
        c_ref[h] = w_old * c_st + _dot_tn(kw.astype(BF16), vh)
        n_ref[h] = jnp.broadcast_to(w_old * n_st + jnp.sum(kw, axis=0, keepdims=True), (8, dh))
        m_ref[h] = jnp.broadcast_to(m_new, (8, LANES))

        if final:
            hh = hh + hprev_ref[0, :, hs]
            y = _rms(hh, mn_ref[:, hs])
            o_ref[0, :, hs] = (y * jax.nn.sigmoid(mz_ref[0, :, hs].astype(F32))).astype(o_ref.dtype)
        else:
            o_ref[0, :, hs] = hh.astype(o_ref.dtype)


def _mlstm_scan(qm, km, vm, sm3, gbias, rev, prev=None, big3=None, mn=None):
    bsz, seq, w = qm.shape
    c = min(MLSTM_CHUNK, seq)
    n = seq // c
    final = prev is not None

    def cm(ci):
        return (n - 1 - ci) if rev else ci

    xspec = pl.BlockSpec((1, c, w), lambda b, ci: (b, cm(ci), 0))
    in_specs = [xspec, xspec, xspec,
                pl.BlockSpec((1, c, LANES), lambda b, ci: (b, cm(ci), 0)),
                pl.BlockSpec((1, LANES), lambda b, ci: (0, 0))]
    args = [qm, km, vm, sm3, gbias]
    if final:
        in_specs += [xspec,
                     pl.BlockSpec((1, c, w), lambda b, ci: (b, cm(ci), OFF_MZ // MLSTM_W)),
                     pl.BlockSpec((1, w), lambda b, ci: (0, 0))]
        args += [prev, big3, mn]
    return pl.pallas_call(
        functools.partial(_mlstm_kernel, rev=rev, final=final, c=c),
        grid=(bsz, n),
        in_specs=in_specs,
        out_specs=xspec,
        out_shape=jax.ShapeDtypeStruct((bsz, seq, w), BF16 if final else F32),
        scratch_shapes=[pltpu.VMEM((MLSTM_HEADS, MLSTM_DH, MLSTM_DH), F32),
                        pltpu.VMEM((MLSTM_HEADS, 8, MLSTM_DH), F32),
                        pltpu.VMEM((MLSTM_HEADS, 8, LANES), F32)],
        compiler_params=_cparams(("parallel", "arbitrary")),
        name="mlstm_bwd" if rev else "mlstm_fwd",
    )(*args)


def _merge_kernel(ya_ref, yb_ref, wa_ref, wb_ref, ga_ref, gb_ref, o_ref):
    a = _dot(ya_ref[...], wa_ref[...])
    b = _dot(yb_ref[...], wb_ref[...])
    ga = jax.nn.sigmoid(ga_ref[...].astype(F32))
    gb = jax.nn.sigmoid(gb_ref[...].astype(F32))
    o_ref[...] = (ga * a + gb * b).astype(o_ref.dtype)


def _merge(ya, yb, wa, wb, big, tm, tn):
    t, kdim = ya.shape
    n = wa.shape[1]
    return pl.pallas_call(
        _merge_kernel,
        grid=(t // tm, n // tn),
        in_specs=[
            pl.BlockSpec((tm, kdim), lambda i, j: (i, 0)),
            pl.BlockSpec((tm, kdim), lambda i, j: (i, 0)),
            pl.BlockSpec((kdim, tn), lambda i, j: (0, j)),
            pl.BlockSpec((kdim, tn), lambda i, j: (0, j)),
            pl.BlockSpec((tm, tn), lambda i, j: (i, OFF_GA // tn + j)),
            pl.BlockSpec((tm, tn), lambda i, j: (i, OFF_GB // tn + j)),
        ],
        out_specs=pl.BlockSpec((tm, tn), lambda i, j: (i, j)),
        out_shape=jax.ShapeDtypeStruct((t, n), BF16),
        compiler_params=_cparams(("parallel", "parallel")),
        name="merge",
    )(ya, yb, wa, wb, big, big)


def _matmul_res_kernel(a_ref, w_ref, r_ref, o_ref):
    o_ref[...] = r_ref[...] + _dot(a_ref[...], w_ref[...])


def _matmul_res(a, w, res, tm, tn):
    t, kdim = a.shape
    n = w.shape[1]
    return pl.pallas_call(
        _matmul_res_kernel,
        grid=(t // tm, n // tn),
        in_specs=[
            pl.BlockSpec((tm, kdim), lambda i, j: (i, 0)),
            pl.BlockSpec((kdim, tn), lambda i, j: (0, j)),
            pl.BlockSpec((tm, tn), lambda i, j: (i, j)),
        ],
        out_specs=pl.BlockSpec((tm, tn), lambda i, j: (i, j)),
        out_shape=jax.ShapeDtypeStruct((t, n), F32),
        compiler_params=_cparams(("parallel", "parallel")),
        name="mix_out",
    )(a, w, res)


def _kv_kernel(mem_ref, g_ref, w_ref, o_ref):
    mn = _rms(mem_ref[...], g_ref[...]).astype(BF16)
    o_ref[...] = _dot(mn, w_ref[...]).astype(BF16)


def _kv_proj(mem2, g, w, tn):
    r, d = mem2.shape
    n = w.shape[1]
    return pl.pallas_call(
        _kv_kernel,
        grid=(n // tn,),
        in_specs=[
            pl.BlockSpec((r, d), lambda j: (0, 0)),
            pl.BlockSpec((1, d), lambda j: (0, 0)),
            pl.BlockSpec((d, tn), lambda j: (0, j)),
        ],
        out_specs=pl.BlockSpec((r, tn), lambda j: (0, j)),
        out_shape=jax.ShapeDtypeStruct((r, n), BF16),
        compiler_params=_cparams(("parallel",)),
        name="kv_proj",
    )(mem2, g, w)


def _xattn_kernel(h_ref, g_ref, wq_ref, k_ref, v_ref, wo_ref, o_ref, xn_ref):
    @pl.when(pl.program_id(1) == 0)
    def _():
        x = h_ref[...]
        xn_ref[...] = _rms(x, g_ref[...]).astype(BF16)
        o_ref[...] = x
    q = _dot(xn_ref[...], wq_ref[...]).astype(BF16)
    s = _dot_nt(q, k_ref[0]) * (XATTN_DH ** -0.5)
    p = jnp.exp(s - jnp.max(s, axis=-1, keepdims=True))
    p = p / jnp.sum(p, axis=-1, keepdims=True)
    o = _dot(p.astype(BF16), v_ref[0]).astype(BF16)
    o_ref[...] += _dot(o, wo_ref[...])


def _xattn(h1, g, wq, kv3, wo, seq, tm):
    t, d = h1.shape
    n_mem = kv3.shape[1]
    dh = XATTN_DH
    per_b = seq // tm
    return pl.pallas_call(
        _xattn_kernel,
        grid=(t // tm, XATTN_HEADS),
        in_specs=[
            pl.BlockSpec((tm, d), lambda i, h: (i, 0)),
            pl.BlockSpec((1, d), lambda i, h: (0, 0)),
            pl.BlockSpec((d, dh), lambda i, h: (0, h)),
            pl.BlockSpec((1, n_mem, dh), lambda i, h: (i // per_b, 0, h)),
            pl.BlockSpec((1, n_mem, dh), lambda i, h: (i // per_b, 0, XATTN_HEADS + h)),
            pl.BlockSpec((dh, d), lambda i, h: (h, 0)),
        ],
        out_specs=pl.BlockSpec((tm, d), lambda i, h: (i, 0)),
        out_shape=jax.ShapeDtypeStruct((t, d), F32),
        scratch_shapes=[pltpu.VMEM((tm, d), BF16)],
        compiler_params=_cparams(("parallel", "arbitrary")),
        name="xattn",
    )(h1, g, wq, kv3, kv3, wo)


def _slab_store(ref, val):
    n = val.shape[0]
    for c in range(SLAB):
        ref[pl.ds(c, n, stride=SLAB), :] = val[:, c * SLAB_W:(c + 1) * SLAB_W]


def _slab_load(buf, slot, first, n):
    return jnp.concatenate(
        [buf[slot, pl.ds(first * SLAB_PITCH + c, n, stride=SLAB_PITCH), :] for c in range(SLAB)], axis=1)


def _slab_copy(src_hbm, dst_buf, sem, slot, src_slab, dst_slab):
    return pltpu.make_async_copy(src_hbm.at[pl.ds(src_slab * SLAB, SLAB)],
                                 dst_buf.at[slot, pl.ds(dst_slab * SLAB_PITCH, SLAB)], sem.at[slot])


def _router_kernel(h_ref, g_ref, wr_ref, br_ref, xn_ref, ids_ref, wts_ref, cnt_ref, carry_ref):
    @pl.when(pl.program_id(0) == 0)
    def _():
        carry_ref[...] = jnp.zeros_like(carry_ref)

    xn = _rms(h_ref[...], g_ref[...])
    _slab_store(xn_ref, xn)
    hi = xn.astype(BF16)
    lo = (xn - hi.astype(F32)).astype(BF16)
    lg = _dot(hi, wr_ref[0]) + _dot(hi, wr_ref[1]) + _dot(lo, wr_ref[0]) + br_ref[...]
    lane = lax.broadcasted_iota(jnp.int32, lg.shape, 1).astype(F32)
    ninf = -jnp.inf
    big_lane = float(LANES)

    def first_max(v):
        mx = jnp.max(v, axis=-1, keepdims=True)
        return mx, jnp.min(jnp.where(v == mx, lane, big_lane), axis=-1, keepdims=True)

    gl = jnp.where(lane < N_GROUPS, lg, ninf)
    gmax, g_sel = first_max(gl)
    p_g = 1.0 / jnp.sum(jnp.exp(gl - gmax), axis=-1, keepdims=True)
    lo_l = N_GROUPS + EXPERTS_PER_GROUP * g_sel
    el = jnp.where((lane >= lo_l) & (lane < lo_l + EXPERTS_PER_GROUP), lg, ninf)
    emax, i1 = first_max(el)
    max2, i2 = first_max(jnp.where(lane == i1, ninf, el))
    e2 = jnp.exp(max2 - emax)
    w1 = p_g / (1.0 + e2)
    w2 = p_g * e2 / (1.0 + e2)
    e1, e2 = i1 - N_GROUPS, i2 - N_GROUPS
    oh1, oh2 = lane == e1, lane == e2
    oh = jnp.where(oh1, 1.0, 0.0) + jnp.where(oh2, 1.0, 0.0)
    tm = lg.shape[0]
    earlier = lax.broadcasted_iota(jnp.int32, (tm, tm), 1) < lax.broadcasted_iota(jnp.int32, (tm, tm), 0)
    carry = carry_ref[0:1, :]
    before = _dot(jnp.where(earlier, 1.0, 0.0).astype(BF16), oh.astype(BF16)) + carry
    r1 = jnp.sum(jnp.where(oh1, before, 0.0), axis=-1, keepdims=True)
    r2 = jnp.sum(jnp.where(oh2, before, 0.0), axis=-1, keepdims=True)
    counts = carry + jnp.sum(oh, axis=0, keepdims=True)
    carry_ref[...] = jnp.broadcast_to(counts, carry_ref.shape)
    cnt_ref[...] = jnp.broadcast_to(counts, cnt_ref.shape).astype(jnp.int32)
    ids = jnp.where(lane == 0.0, e1, jnp.where(lane == 1.0, e2, jnp.where(lane == 2.0, r1,
                                                                         jnp.where(lane == 3.0, r2, 0.0))))
    ids_ref[...] = ids.astype(jnp.int32)
    wts_ref[...] = jnp.where(lane == 0.0, w1, jnp.where(lane == 1.0, w2, 0.0))


def _router(h2, g, wr, br, tm):
    t, d = h2.shape
    return pl.pallas_call(
        _router_kernel,
        grid=(t // tm,),
        in_specs=[
            pl.BlockSpec((tm, d), lambda i: (i, 0)),
            pl.BlockSpec((1, d), lambda i: (0, 0)),
            pl.BlockSpec((2, d, LANES), lambda i: (0, 0, 0)),
            pl.BlockSpec((1, LANES), lambda i: (0, 0)),
        ],
        out_specs=[
            pl.BlockSpec((tm * SLAB, SLAB_W), lambda i: (i, 0)),
            pl.BlockSpec((tm, LANES), lambda i: (i, 0)),
            pl.BlockSpec((tm, LANES), lambda i: (i, 0)),
            pl.BlockSpec((8, LANES), lambda i: (0, 0)),
        ],
        out_shape=[jax.ShapeDtypeStruct((t * SLAB, SLAB_W), F32),
                   jax.ShapeDtypeStruct((t, LANES), jnp.int32),
                   jax.ShapeDtypeStruct((t, LANES), F32),
                   jax.ShapeDtypeStruct((8, LANES), jnp.int32)],
        scratch_shapes=[pltpu.VMEM((8, LANES), F32)],
        compiler_params=_cparams(("arbitrary",)),
        name="router",
    )(h2, g, wr, br)


def _experts_kernel(blk_e_ref, n_used_ref, tok_ref, tok_next_ref, x_hbm, wg_ref, wu_ref, wd_ref, y_ref,
                    xbuf, wg_b, wu_b, wd_b, sem, *, rows):
    j = pl.program_id(0)
    n_used = n_used_ref[0]
    used = j < n_used
    slot = j % 2

    def gather(idx_ref, into):
        def start(r, carry):
            _slab_copy(x_hbm, xbuf, sem, into, idx_ref[0, 0, r], r).start()
            return carry
        lax.fori_loop(0, rows, start, 0)

    @pl.when(j == 0)
    def _():
        gather(tok_ref, 0)

    @pl.when(jnp.logical_not(used))
    def _():
        y_ref[...] = jnp.zeros_like(y_ref)

    @pl.when(used)
    def _():
        @pl.when(j + 1 < n_used)
        def _():
            gather(tok_next_ref, 1 - slot)

        def wait(r, carry):
            _slab_copy(x_hbm, xbuf, sem, slot, 0, r).wait()
            return carry
        lax.fori_loop(0, rows, wait, 0)

        @pl.when((j == 0) | (blk_e_ref[j] != blk_e_ref[jnp.maximum(j - 1, 0)]))
        def _():
            wg_b[...] = wg_ref[0].astype(BF16)
            wu_b[...] = wu_ref[0].astype(BF16)
            wd_b[...] = wd_ref[0].astype(BF16)

        xb = _slab_load(xbuf, slot, 0, rows).astype(BF16)
        gate = _dot(xb, wg_b[...])
        up = _dot(xb, wu_b[...])
        act = (gate * jax.nn.sigmoid(gate) * up).astype(BF16)
        _slab_store(y_ref, _dot(act, wd_b[...]))


def _experts(blk_e, n_used, row_tok3, xn_slab, wg, wu, wd):
    n_blocks, _, rows = row_tok3.shape
    _, d, de = wg.shape
    grid_spec = pltpu.PrefetchScalarGridSpec(
        num_scalar_prefetch=2,
        grid=(n_blocks,),
        in_specs=[
            pl.BlockSpec((1, 1, rows), lambda j, be, nu: (j, 0, 0), memory_space=pltpu.SMEM),
            pl.BlockSpec((1, 1, rows), lambda j, be, nu: (jnp.minimum(j + 1, n_blocks - 1), 0, 0),
                         memory_space=pltpu.SMEM),
            pl.BlockSpec(memory_space=pl.ANY),
            pl.BlockSpec((1, d, de), lambda j, be, nu: (be[j], 0, 0)),
            pl.BlockSpec((1, d, de), lambda j, be, nu: (be[j], 0, 0)),
            pl.BlockSpec((1, de, d), lambda j, be, nu: (be[j], 0, 0)),
        ],
        out_specs=pl.BlockSpec((rows * SLAB, SLAB_W), lambda j, be, nu: (j, 0)),
        scratch_shapes=[pltpu.VMEM((2, rows * SLAB_PITCH, SLAB_W), F32),
                        pltpu.VMEM((d, de), BF16), pltpu.VMEM((d, de), BF16), pltpu.VMEM((de, d), BF16),
                        pltpu.SemaphoreType.DMA((2,))],
    )
    return pl.pallas_call(
        functools.partial(_experts_kernel, rows=rows),
        grid_spec=grid_spec,
        out_shape=jax.ShapeDtypeStruct((n_blocks * rows * SLAB, SLAB_W), F32),
        compiler_params=_cparams(("arbitrary",)),
        name="experts",
    )(blk_e, n_used, row_tok3, row_tok3, xn_slab, wg, wu, wd)


def _combine_kernel(pos_ref, pos_next_ref, y_hbm, wts_ref, h_ref, g_ref, o_ref, ybuf, sem, *, tc):
    j = pl.program_id(0)
    slot = j % 2

    def gather(idx_ref, into):
        def start(r, carry):
            _slab_copy(y_hbm, ybuf, sem, into, idx_ref[0, 0, 2 * r], r).start()
            _slab_copy(y_hbm, ybuf, sem, into, idx_ref[0, 0, 2 * r + 1], tc + r).start()
            return carry
        lax.fori_loop(0, tc, start, 0)

    @pl.when(j == 0)
    def _():
        gather(pos_ref, 0)

    @pl.when(j + 1 < pl.num_programs(0))
    def _():
        gather(pos_next_ref, 1 - slot)

    def wait(r, carry):
        _slab_copy(y_hbm, ybuf, sem, slot, 0, r).wait()
        return carry
    lax.fori_loop(0, 2 * tc, wait, 0)

    w = wts_ref[...]
    y = _slab_load(ybuf, slot, 0, tc) * w[:, 0:1] + _slab_load(ybuf, slot, tc, tc) * w[:, 1:2]
    o_ref[...] = _rms(h_ref[...] + y, g_ref[...])


def _combine(pos3, y_slab, wts, h2, g, tc):
    t, d = h2.shape
    n = t // tc
    return pl.pallas_call(
        functools.partial(_combine_kernel, tc=tc),
        grid=(n,),
        in_specs=[
            pl.BlockSpec((1, 1, 2 * tc), lambda i: (i, 0, 0), memory_space=pltpu.SMEM),
            pl.BlockSpec((1, 1, 2 * tc), lambda i: (jnp.minimum(i + 1, n - 1), 0, 0), memory_space=pltpu.SMEM),
            pl.BlockSpec(memory_space=pl.ANY),
            pl.BlockSpec((tc, LANES), lambda i: (i, 0)),
            pl.BlockSpec((tc, d), lambda i: (i, 0)),
            pl.BlockSpec((1, d), lambda i: (0, 0)),
        ],
        out_specs=pl.BlockSpec((tc, d), lambda i: (i, 0)),
        out_shape=jax.ShapeDtypeStruct((t, d), F32),
        scratch_shapes=[pltpu.VMEM((2, 2 * tc * SLAB_PITCH, SLAB_W), F32), pltpu.SemaphoreType.DMA((2,))],
        compiler_params=_cparams(("arbitrary",)),
        name="combine",
    )(pos3, pos3, y_slab, wts, h2, g)


def _dispatch_plan(ids, counts, rows):
    flat_e = ids[:, 0:2].reshape(-1)
    rank = ids[:, 2:4].reshape(-1)
    n_asg = flat_e.shape[0]
    experts = jnp.arange(N_EXPERTS, dtype=jnp.int32)
    padded = ((counts + rows - 1) // rows) * rows
    pends = jnp.cumsum(padded)
    pstarts = pends - padded
    pos = jnp.sum(jnp.where(flat_e[:, None] == experts[None, :], pstarts[None, :], 0), axis=1) + rank
    n_rows = n_asg + N_EXPERTS * rows
    n_blocks = n_rows // rows
    row_tok = jnp.zeros((n_rows,), jnp.int32).at[pos].set(jnp.arange(n_asg, dtype=jnp.int32) // 2)
    starts = jnp.arange(n_blocks, dtype=jnp.int32) * rows
    blk_e = jnp.minimum(jnp.sum((pends[None, :] <= starts[:, None]).astype(jnp.int32), axis=1), N_EXPERTS - 1)
    n_used = (pends[-1] // rows).astype(jnp.int32).reshape(1)
    return pos.astype(jnp.int32), row_tok.reshape(n_blocks, 1, rows), blk_e.astype(jnp.int32), n_used


def _tile(n, pref):
    return pref if n % pref == 0 else n


def _layer(h, mem, p, l):
    bsz, seq, d = h.shape
    t = bsz * seq
    x2 = h.reshape(t, d)
    row = lambda v: v.reshape(1, -1)
    tm = _tile(t, 1024)

    w_in = p['w_in'][l]
    w_big = jnp.concatenate([w_in[:, 0:3072], w_in[:, 3104:5152], w_in[:, 5168:9264]], axis=1).astype(BF16)
    w_small = jnp.concatenate([w_in[:, 3072:3104], w_in[:, 5152:5168],
                               jnp.zeros((d, LANES - 48), F32)], axis=1).astype(BF16)
    big, small = _in_proj(x2, row(p['norm_mix'][l]), w_big, w_small, tm, 1024)
    big3 = big.reshape(bsz, seq, N_BIG)
    sm3 = small.reshape(bsz, seq, LANES)

    def lr_pad(w, off):
        return jnp.zeros((LANES, GLA_K), F32).at[off:off + GLA_RANK].set(w).astype(BF16)

    o_f = _gla_scan(big3, sm3, lr_pad(p['gla_w_lr_f'][l], SM_LRF), row(p['gla_b_lr_f'][l]), rev=False)
    y_a = _gla_scan(big3, sm3, lr_pad(p['gla_w_lr_b'][l], SM_LRB), row(p['gla_b_lr_b'][l]), rev=True,
                    prev=o_f, gn=row(p['gla_norm'][l]))

    cw = jnp.zeros((8, MLSTM_W), F32).at[:CONV_WIDTH].set(p['conv_w'][l].reshape(CONV_WIDTH, MLSTM_W))
    qm, km, vm = _mlstm_pre(big3, cw, row(p['conv_b'][l]), p['m_wq'][l].astype(BF16),
                            p['m_wk'][l].astype(BF16), p['m_wv'][l].astype(BF16), _tile(seq, 1024))
    gbias = jnp.zeros((1, LANES), F32).at[0, SM_GATES:SM_GATES + 4 * MLSTM_HEADS].set(
        p['m_gate_bias'][l].reshape(-1))
    h_f = _mlstm_scan(qm, km, vm, sm3, gbias, rev=False)
    y_b = _mlstm_scan(qm, km, vm, sm3, gbias, rev=True, prev=h_f, big3=big3, mn=row(p['m_norm'][l]))

    merged = _merge(y_a.reshape(t, GLA_V), y_b.reshape(t, MLSTM_W), p['w_branch_a'][l].astype(BF16),
                    p['w_branch_b'][l].astype(BF16), big, tm, 1024)
    h1 = _matmul_res(merged, p['w_mix_out'][l].astype(BF16), x2, tm, 1024)

    n_mem = mem.shape[1]
    kv = _kv_proj(mem.reshape(bsz * n_mem, d), row(p['norm_mem'][l]), p['w_xkv'][l].astype(BF16), 1024)
    h2 = _xattn(h1, row(p['norm_xattn'][l]), p['w_xq'][l].astype(BF16), kv.reshape(bsz, n_mem, 2 * d),
                p['w_xo'][l].astype(BF16), seq, _tile(seq, 512))

    wr = jnp.concatenate([p['w_group'][l], p['w_router'][l].transpose(1, 0, 2).reshape(d, N_EXPERTS),
                          jnp.zeros((d, LANES - N_GROUPS - N_EXPERTS), F32)], axis=1)
    wr_hi = wr.astype(BF16)
    wr2 = jnp.stack([wr_hi, (wr - wr_hi.astype(F32)).astype(BF16)])
    br = jnp.concatenate([p['b_group'][l], p['b_router'][l].reshape(-1),
                          jnp.zeros((LANES - N_GROUPS - N_EXPERTS,), F32)]).reshape(1, LANES)
    xn3, ids, wts, cnt = _router(h2, row(p['norm_ffn'][l]), wr2, br, tm)
    pos, row_tok3, blk_e, n_used = _dispatch_plan(ids, cnt[0, :N_EXPERTS], MOE_ROWS)
    y_rows = _experts(blk_e, n_used, row_tok3, xn3, p['w_gate'][l], p['w_up'][l], p['w_down'][l])
    tc = _tile(t, COMBINE_TOK)
    return pos.reshape(t // tc, 1, 2 * tc), y_rows, wts, h2, tc


def kernel(x, mem, norm_mix, w_in, gla_w_lr_f, gla_b_lr_f, gla_w_lr_b, gla_b_lr_b, gla_norm, conv_w, conv_b, m_wq, m_wk, m_wv, m_gate_bias, m_norm, w_branch_a, w_branch_b, w_mix_out, norm_xattn, norm_mem, w_xq, w_xkv, w_xo, norm_ffn, w_group, b_group, w_router, b_router, w_gate, w_up, w_down, norm_final):
    p = dict(norm_mix=norm_mix, w_in=w_in, gla_w_lr_f=gla_w_lr_f, gla_b_lr_f=gla_b_lr_f, gla_w_lr_b=gla_w_lr_b,
             gla_b_lr_b=gla_b_lr_b, gla_norm=gla_norm, conv_w=conv_w, conv_b=conv_b, m_wq=m_wq, m_wk=m_wk,
             m_wv=m_wv, m_gate_bias=m_gate_bias, m_norm=m_norm, w_branch_a=w_branch_a, w_branch_b=w_branch_b,
             w_mix_out=w_mix_out, norm_xattn=norm_xattn, norm_mem=norm_mem, w_xq=w_xq, w_xkv=w_xkv, w_xo=w_xo,
             norm_ffn=norm_ffn, w_group=w_group, b_group=b_group, w_router=w_router, b_router=b_router,
             w_gate=w_gate, w_up=w_up, w_down=w_down)
    bsz, seq, d = x.shape
    depth = norm_mix.shape[0]
    assert depth == 1, "the final norm is fused into the last layer's combine step"
    pos3, y_rows, wts, h2, tc = _layer(x, mem, p, 0)
    out = _combine(pos3, y_rows, wts, h2, norm_final.reshape(1, d), tc)
    return out.reshape(bsz, seq, d)
```

```python
import functools

import jax
import jax.numpy as jnp
from jax import lax
from jax.experimental import pallas as pl
from jax.experimental.pallas import tpu as pltpu

F32 = jnp.float32
BF16 = jnp.bfloat16

EPS = 1e-6
LOG2_E = 1.4426950408889634
D_MODEL = 2048

GLA_HEADS = 4
GLA_DK = 128
GLA_DV = 256
GLA_K = GLA_HEADS * GLA_DK
GLA_V = GLA_HEADS * GLA_DV
GLA_RANK = 16
GLA_TAU = 16.0
GLA_CHUNK = 64
GLA_SUB = 16

MLSTM_HEADS = 4
MLSTM_DH = 256
MLSTM_W = MLSTM_HEADS * MLSTM_DH
CONV_WIDTH = 5
MLSTM_CHUNK = 256
CONV_HALO = 16

XATTN_HEADS = 4
XATTN_DH = D_MODEL // XATTN_HEADS

N_GROUPS = 4
EXPERTS_PER_GROUP = 8
N_EXPERTS = N_GROUPS * EXPERTS_PER_GROUP
D_EXPERT = 512
MOE_ROWS = 256
COMBINE_TOK = 256

LANES = 128
SLAB_W = LANES
SLAB = D_MODEL // SLAB_W
SLAB_PITCH = SLAB + 8

OFF_Q, OFF_K, OFF_V, OFF_GG = 0, 512, 1024, 2048
OFF_MX, OFF_MZ, OFF_GA, OFF_GB = 3072, 4096, 5120, 7168
N_BIG = 9216
SM_LRF, SM_LRB, SM_GATES = 0, 16, 32

VMEM_LIMIT = 56 * 1024 * 1024


def _cparams(sem):
    return pltpu.CompilerParams(dimension_semantics=sem, vmem_limit_bytes=VMEM_LIMIT)


def _rms(x, g):
    return x * lax.rsqrt(jnp.mean(x * x, axis=-1, keepdims=True) + EPS) * g


def _log_sigmoid(x):
    return jnp.minimum(x, 0.0) - jnp.log(1.0 + jnp.exp(-jnp.abs(x)))


def _dot(a, b):
    return jnp.dot(a, b, preferred_element_type=F32)


def _dot_nt(a, b):
    return lax.dot_general(a, b, (((1,), (1,)), ((), ())), preferred_element_type=F32)


def _dot_tn(a, b):
    return lax.dot_general(a, b, (((0,), (0,)), ((), ())), preferred_element_type=F32)


def _order_mask(c, rev):
    t = lax.broadcasted_iota(jnp.int32, (c, c), 0)
    s = lax.broadcasted_iota(jnp.int32, (c, c), 1)
    return (s >= t) if rev else (s <= t)


def _cumsum_mm(mask_bf16, x):
    hi = x.astype(BF16)
    r1 = x - hi.astype(F32)
    mid = r1.astype(BF16)
    lo = (r1 - mid.astype(F32)).astype(BF16)
    return _dot(mask_bf16, hi) + _dot(mask_bf16, mid) + _dot(mask_bf16, lo)


def _in_proj_kernel(x_ref, g_ref, wbig_ref, wsm_ref, big_ref, sm_ref, xn_ref):
    @pl.when(pl.program_id(1) == 0)
    def _():
        xn = _rms(x_ref[...], g_ref[...]).astype(BF16)
        xn_ref[...] = xn
        sm_ref[...] = _dot(xn, wsm_ref[...])
    big_ref[...] = _dot(xn_ref[...], wbig_ref[...]).astype(BF16)


def _in_proj(x2, g, w_big, w_small, tm, tn):
    t, d = x2.shape
    n = w_big.shape[1]
    return pl.pallas_call(
        _in_proj_kernel,
        grid=(t // tm, n // tn),
        in_specs=[
            pl.BlockSpec((tm, d), lambda i, j: (i, 0)),
            pl.BlockSpec((1, d), lambda i, j: (0, 0)),
            pl.BlockSpec((d, tn), lambda i, j: (0, j)),
            pl.BlockSpec((d, LANES), lambda i, j: (0, 0)),
        ],
        out_specs=[
            pl.BlockSpec((tm, tn), lambda i, j: (i, j)),
            pl.BlockSpec((tm, LANES), lambda i, j: (i, 0)),
        ],
        out_shape=[jax.ShapeDtypeStruct((t, n), BF16), jax.ShapeDtypeStruct((t, LANES), F32)],
        scratch_shapes=[pltpu.VMEM((tm, d), BF16)],
        compiler_params=_cparams(("parallel", "arbitrary")),
        name="in_proj",
    )(x2, g, w_big, w_small)


def _gla_kernel(q_ref, k_ref, v_ref, sm_ref, wlr_ref, blr_ref, *rest, rev, final, c):
    if final:
        oprev_ref, gg_ref, gn_ref, o_ref, st_ref = rest
    else:
        o_ref, st_ref = rest
    sb = GLA_SUB

    @pl.when(pl.program_id(1) == 0)
    def _():
        st_ref[...] = jnp.zeros_like(st_ref)

    x = _dot(sm_ref[0].astype(BF16), wlr_ref[...]) + blr_ref[...]
    la = _log_sigmoid(x) * (1.0 / GLA_TAU)
    mask_b = jnp.where(_order_mask(c, rev), 1.0, 0.0).astype(BF16)
    b = _cumsum_mm(mask_b, la) * LOG2_E
    last = 0 if rev else c - 1
    tot = b[last:last + 1, :]
    q = q_ref[0].astype(F32) * (GLA_DK ** -0.5)
    k = k_ref[0].astype(F32)
    q_in = (q * jnp.exp2(b)).astype(BF16)
    k_dec = (k * jnp.exp2(tot - b)).astype(BF16)
    e_tot = jnp.exp2(tot)

    col = lax.broadcasted_iota(jnp.int32, (sb, c), 1)
    trow = lax.broadcasted_iota(jnp.int32, (sb, c), 0)

    for h in range(GLA_HEADS):
        ks = slice(h * GLA_DK, (h + 1) * GLA_DK)
        vs = slice(h * GLA_DV, (h + 1) * GLA_DV)
        vh = v_ref[0, :, vs]
        kh_b = k_ref[0, :, ks]
        st = st_ref[h]
        o_inter = _dot_nt(q_in[:, ks], st.astype(BF16))
        st_ref[h] = st * e_tot[:, ks] + _dot_tn(vh, k_dec[:, ks])

        bh, qh, kh = b[:, ks], q[:, ks], k[:, ks]
        rows = []
        for i in range(c // sb):
            r0 = i * sb
            ref_row = r0 + (sb - 1 if rev else 0)
            beta = bh[ref_row:ref_row + 1, :]
            bb, qb = bh[r0:r0 + sb], qh[r0:r0 + sb]
            qt = (qb * jnp.exp2(bb - beta)).astype(BF16)
            kt = (kh * jnp.exp2(beta - bh)).astype(BF16)
            a_off = _dot_nt(qt, kt)
            slabs = [(qb * jnp.exp2(bb - bb[s:s + 1, :])).astype(BF16) for s in range(sb)]
            g = _dot_nt(jnp.concatenate(slabs, axis=0), kh_b)
            a_diag = jnp.zeros((sb, c), F32)
            for s in range(sb):
                a_diag = a_diag + jnp.where(col == r0 + s, g[s * sb:(s + 1) * sb], 0.0)
            tr = trow + r0
            if rev:
                off_mask = col >= r0 + sb
                diag_mask = (col >= tr) & (col < r0 + sb)
            else:
                off_mask = col < r0
                diag_mask = (col <= tr) & (col >= r0)
            rows.append(jnp.where(off_mask, a_off, 0.0) + jnp.where(diag_mask, a_diag, 0.0))
        a = jnp.concatenate(rows, axis=0).astype(BF16)
        o = o_inter + _dot(a, vh)
        if final:
            o = o + oprev_ref[0, :, vs]
            y = _rms(o, gn_ref[:, vs])
            gg = gg_ref[0, :, vs].astype(F32)
            o_ref[0, :, vs] = (y * (gg * jax.nn.sigmoid(gg))).astype(o_ref.dtype)
        else:
            o_ref[0, :, vs] = o.astype(o_ref.dtype)


def _gla_scan(big3, sm3, wlr, blr, rev, prev=None, gn=None):
    bsz, seq, _ = big3.shape
    c = GLA_CHUNK
    n = seq // c
    final = prev is not None

    def cm(ci):
        return (n - 1 - ci) if rev else ci

    in_specs = [
        pl.BlockSpec((1, c, GLA_K), lambda b, ci: (b, cm(ci), OFF_Q // GLA_K)),
        pl.BlockSpec((1, c, GLA_K), lambda b, ci: (b, cm(ci), OFF_K // GLA_K)),
        pl.BlockSpec((1, c, GLA_V), lambda b, ci: (b, cm(ci), OFF_V // GLA_V)),
        pl.BlockSpec((1, c, LANES), lambda b, ci: (b, cm(ci), 0)),
        pl.BlockSpec((LANES, GLA_K), lambda b, ci: (0, 0)),
        pl.BlockSpec((1, GLA_K), lambda b, ci: (0, 0)),
    ]
    args = [big3, big3, big3, sm3, wlr, blr]
    if final:
        in_specs += [
            pl.BlockSpec((1, c, GLA_V), lambda b, ci: (b, cm(ci), 0)),
            pl.BlockSpec((1, c, GLA_V), lambda b, ci: (b, cm(ci), OFF_GG // GLA_V)),
            pl.BlockSpec((1, GLA_V), lambda b, ci: (0, 0)),
        ]
        args += [prev, big3, gn]
    return pl.pallas_call(
        functools.partial(_gla_kernel, rev=rev, final=final, c=c),
        grid=(bsz, n),
        in_specs=in_specs,
        out_specs=pl.BlockSpec((1, c, GLA_V), lambda b, ci: (b, cm(ci), 0)),
        out_shape=jax.ShapeDtypeStruct((bsz, seq, GLA_V), BF16 if final else F32),
        scratch_shapes=[pltpu.VMEM((GLA_HEADS, GLA_DV, GLA_DK), F32)],
        compiler_params=_cparams(("parallel", "arbitrary")),
        name="gla_bwd" if rev else "gla_fwd",
    )(*args)


def _mlstm_pre_kernel(cur_ref, prev_ref, next_ref, cw_ref, cb_ref, wq_ref, wk_ref, wv_ref,
                      q_ref, k_ref, v_ref, *, tm):
    i = pl.program_id(2)
    cur_b = cur_ref[0]
    cur = cur_b.astype(F32)
    halo = CONV_WIDTH // 2
    prev = jnp.where(i > 0, prev_ref[0].astype(F32), 0.0)
    nxt = jnp.where(i < pl.num_programs(2) - 1, next_ref[0].astype(F32), 0.0)
    ext = jnp.concatenate([prev[CONV_HALO - 8:], cur, nxt[:8]], axis=0)
    acc = jnp.zeros_like(cur) + cb_ref[...]
    for w in range(CONV_WIDTH):
        off = 8 - halo + w
        acc = acc + ext[off:off + tm] * cw_ref[w:w + 1, :]
    xc = (acc * jax.nn.sigmoid(acc)).astype(BF16)
    q_ref[0] = _dot(xc, wq_ref[0]).astype(BF16)
    k_ref[0] = (_dot(xc, wk_ref[0]) * (MLSTM_DH ** -0.5)).astype(BF16)
    v_ref[0] = _dot(cur_b, wv_ref[0]).astype(BF16)


def _mlstm_pre(big3, cw, cb, wq, wk, wv, tm):
    bsz, seq, _ = big3.shape
    dh = MLSTM_DH
    nh = tm // CONV_HALO
    n_halo = seq // CONV_HALO
    c0 = OFF_MX // dh
    out = jax.ShapeDtypeStruct((bsz, seq, MLSTM_W), BF16)
    ospec = pl.BlockSpec((1, tm, dh), lambda b, h, i: (b, i, h))
    wspec = pl.BlockSpec((1, dh, dh), lambda b, h, i: (h, 0, 0))
    return pl.pallas_call(
        functools.partial(_mlstm_pre_kernel, tm=tm),
        grid=(bsz, MLSTM_HEADS, seq // tm),
        in_specs=[
            pl.BlockSpec((1, tm, dh), lambda b, h, i: (b, i, c0 + h)),
            pl.BlockSpec((1, CONV_HALO, dh), lambda b, h, i: (b, jnp.maximum(i * nh - 1, 0), c0 + h)),
            pl.BlockSpec((1, CONV_HALO, dh), lambda b, h, i: (b, jnp.minimum((i + 1) * nh, n_halo - 1), c0 + h)),
            pl.BlockSpec((8, dh), lambda b, h, i: (0, h)),
            pl.BlockSpec((1, dh), lambda b, h, i: (0, h)),
            wspec, wspec, wspec,
        ],
        out_specs=[ospec, ospec, ospec],
        out_shape=[out, out, out],
        compiler_params=_cparams(("parallel", "parallel", "parallel")),
        name="mlstm_pre",
    )(big3, big3, big3, cw, cb, wq, wk, wv)


def _mlstm_kernel(q_ref, k_ref, v_ref, sm_ref, gb_ref, *rest, rev, final, c):
    if final:
        hprev_ref, mz_ref, mn_ref, o_ref, c_ref, n_ref, m_ref = rest
    else:
        o_ref, c_ref, n_ref, m_ref = rest

    @pl.when(pl.program_id(1) == 0)
    def _():
        c_ref[...] = jnp.zeros_like(c_ref)
        n_ref[...] = jnp.zeros_like(n_ref)
        m_ref[...] = jnp.zeros_like(m_ref)

    g = sm_ref[0] + gb_ref[...]
    mask = _order_mask(c, rev)
    bcum = _cumsum_mm(jnp.where(mask, 1.0, 0.0).astype(BF16), _log_sigmoid(g))
    g_t = g.T
    b_t = bcum.T
    last = 0 if rev else c - 1
    dh = MLSTM_DH

    for h in range(MLSTM_HEADS):
        ci = SM_GATES + (2 * MLSTM_HEADS if rev else 0) + h
        cf = ci + MLSTM_HEADS
        hs = slice(h * dh, (h + 1) * dh)
        i_col, b_col = g[:, ci:ci + 1], bcum[:, cf:cf + 1]
        i_row, b_row = g_t[ci:ci + 1, :], b_t[cf:cf + 1, :]
        tot = b_col[last:last + 1, :]
        m_prev = m_ref[h, 0:1, 0:1]
        qh, kh, vh = q_ref[0, :, hs], k_ref[0, :, hs], v_ref[0, :, hs]
        c_st = c_ref[h]
        n_st = n_ref[h, 0:1, :]

        dmat = jnp.where(mask, b_col - b_row + i_row, -jnp.inf)
        inter_log = b_col + m_prev
        m_t = jnp.maximum(inter_log, jnp.max(dmat, axis=-1, keepdims=True))
        w_intra = jnp.exp(dmat - m_t)
        w_inter = jnp.exp(inter_log - m_t)
        s = _dot_nt(qh, kh) * w_intra
        num = w_inter * _dot(qh, c_st.astype(BF16)) + _dot(s.astype(BF16), vh)
        den = (w_inter * jnp.sum(qh.astype(F32) * n_st, axis=-1, keepdims=True)
               + jnp.sum(s, axis=-1, keepdims=True))
        hh = num / jnp.maximum(jnp.abs(den), jnp.exp(-m_t))

        upd_col = tot - b_col + i_col
        upd_row = tot - b_row + i_row
        m_new = jnp.maximum(tot + m_prev, jnp.max(upd_row, axis=-1, keepdims=True))
        w_old = jnp.exp(tot + m_prev - m_new)
        kw = kh.astype(F32) * jnp.exp(upd_col - m_new)
        c_ref[h] = w_old * c_st + _dot_tn(kw.astype(BF16), vh)
        n_ref[h] = jnp.broadcast_to(w_old * n_st + jnp.sum(kw, axis=0, keepdims=True), (8, dh))
        m_ref[h] = jnp.broadcast_to(m_new, (8, LANES))

        if final:
            hh = hh + hprev_ref[0, :, hs]
            y = _rms(hh, mn_ref[:, hs])
            o_ref[0, :, hs] = (y * jax.nn.sigmoid(mz_ref[0, :, hs].astype(F32))).astype(o_ref.dtype)
        else:
            o_ref[0, :, hs] = hh.astype(o_ref.dtype)


def _mlstm_scan(qm, km, vm, sm3, gbias, rev, prev=None, big3=None, mn=None):
    bsz, seq, w = qm.shape
    c = min(MLSTM_CHUNK, seq)
    n = seq // c
    final = prev is not None

    def cm(ci):
        return (n - 1 - ci) if rev else ci

    xspec = pl.BlockSpec((1, c, w), lambda b, ci: (b, cm(ci), 0))
    in_specs = [xspec, xspec, xspec,
                pl.BlockSpec((1, c, LANES), lambda b, ci: (b, cm(ci), 0)),
                pl.BlockSpec((1, LANES), lambda b, ci: (0, 0))]
    args = [qm, km, vm, sm3, gbias]
    if final:
        in_specs += [xspec,
                     pl.BlockSpec((1, c, w), lambda b, ci: (b, cm(ci), OFF_MZ // MLSTM_W)),
                     pl.BlockSpec((1, w), lambda b, ci: (0, 0))]
        args += [prev, big3, mn]
    return pl.pallas_call(
        functools.partial(_mlstm_kernel, rev=rev, final=final, c=c),
        grid=(bsz, n),
        in_specs=in_specs,
        out_specs=xspec,
        out_shape=jax.ShapeDtypeStruct((bsz, seq, w), BF16 if final else F32),
        scratch_shapes=[pltpu.VMEM((MLSTM_HEADS, MLSTM_DH, MLSTM_DH), F32),
                        pltpu.VMEM((MLSTM_HEADS, 8, MLSTM_DH), F32),
                        pltpu.VMEM((MLSTM_HEADS, 8, LANES), F32)],
        compiler_params=_cparams(("parallel", "arbitrary")),
        name="mlstm_bwd" if rev else "mlstm_fwd",
    )(*args)


def _merge_kernel(ya_ref, yb_ref, wa_ref, wb_ref, ga_ref, gb_ref, o_ref):
    a = _dot(ya_ref[...], wa_ref[...])
    b = _dot(yb_ref[...], wb_ref[...])
    ga = jax.nn.sigmoid(ga_ref[...].astype(F32))
    gb = jax.nn.sigmoid(gb_ref[...].astype(F32))
    o_ref[...] = (ga * a + gb * b).astype(o_ref.dtype)


def _merge(ya, yb, wa, wb, big, tm, tn):
    t, kdim = ya.shape
    n = wa.shape[1]
    return pl.pallas_call(
        _merge_kernel,
        grid=(t // tm, n // tn),
        in_specs=[
            pl.BlockSpec((tm, kdim), lambda i, j: (i, 0)),
            pl.BlockSpec((tm, kdim), lambda i, j: (i, 0)),
            pl.BlockSpec((kdim, tn), lambda i, j: (0, j)),
            pl.BlockSpec((kdim, tn), lambda i, j: (0, j)),
            pl.BlockSpec((tm, tn), lambda i, j: (i, OFF_GA // tn + j)),
            pl.BlockSpec((tm, tn), lambda i, j: (i, OFF_GB // tn + j)),
        ],
        out_specs=pl.BlockSpec((tm, tn), lambda i, j: (i, j)),
        out_shape=jax.ShapeDtypeStruct((t, n), BF16),
        compiler_params=_cparams(("parallel", "parallel")),
        name="merge",
    )(ya, yb, wa, wb, big, big)


def _matmul_res_kernel(a_ref, w_ref, r_ref, o_ref):
    o_ref[...] = r_ref[...] + _dot(a_ref[...], w_ref[...])


def _matmul_res(a, w, res, tm, tn):
    t, kdim = a.shape
    n = w.shape[1]
    return pl.pallas_call(
        _matmul_res_kernel,
        grid=(t // tm, n // tn),
        in_specs=[
            pl.BlockSpec((tm, kdim), lambda i, j: (i, 0)),
            pl.BlockSpec((kdim, tn), lambda i, j: (0, j)),
            pl.BlockSpec((tm, tn), lambda i, j: (i, j)),
        ],
        out_specs=pl.BlockSpec((tm, tn), lambda i, j: (i, j)),
        out_shape=jax.ShapeDtypeStruct((t, n), F32),
        compiler_params=_cparams(("parallel", "parallel")),
        name="mix_out",
    )(a, w, res)


def _kv_kernel(mem_ref, g_ref, w_ref, o_ref):
    mn = _rms(mem_ref[...], g_ref[...]).astype(BF16)
    o_ref[...] = _dot(mn, w_ref[...]).astype(BF16)


def _kv_proj(mem2, g, w, tn):
    r, d = mem2.shape
    n = w.shape[1]
    return pl.pallas_call(
        _kv_kernel,
        grid=(n // tn,),
        in_specs=[
            pl.BlockSpec((r, d), lambda j: (0, 0)),
            pl.BlockSpec((1, d), lambda j: (0, 0)),
            pl.BlockSpec((d, tn), lambda j: (0, j)),
        ],
        out_specs=pl.BlockSpec((r, tn), lambda j: (0, j)),
        out_shape=jax.ShapeDtypeStruct((r, n), BF16),
        compiler_params=_cparams(("parallel",)),
        name="kv_proj",
    )(mem2, g, w)


def _xattn_kernel(h_ref, g_ref, wq_ref, k_ref, v_ref, wo_ref, o_ref, xn_ref):
    @pl.when(pl.program_id(1) == 0)
    def _():
        x = h_ref[...]
        xn_ref[...] = _rms(x, g_ref[...]).astype(BF16)
        o_ref[...] = x
    q = _dot(xn_ref[...], wq_ref[...]).astype(BF16)
    s = _dot_nt(q, k_ref[0]) * (XATTN_DH ** -0.5)
    p = jnp.exp(s - jnp.max(s, axis=-1, keepdims=True))
    p = p / jnp.sum(p, axis=-1, keepdims=True)
    o = _dot(p.astype(BF16), v_ref[0]).astype(BF16)
    o_ref[...] += _dot(o, wo_ref[...])


def _xattn(h1, g, wq, kv3, wo, seq, tm):
    t, d = h1.shape
    n_mem = kv3.shape[1]
    dh = XATTN_DH
    per_b = seq // tm
    return pl.pallas_call(
        _xattn_kernel,
        grid=(t // tm, XATTN_HEADS),
        in_specs=[
            pl.BlockSpec((tm, d), lambda i, h: (i, 0)),
            pl.BlockSpec((1, d), lambda i, h: (0, 0)),
            pl.BlockSpec((d, dh), lambda i, h: (0, h)),
            pl.BlockSpec((1, n_mem, dh), lambda i, h: (i // per_b, 0, h)),
            pl.BlockSpec((1, n_mem, dh), lambda i, h: (i // per_b, 0, XATTN_HEADS + h)),
            pl.BlockSpec((dh, d), lambda i, h: (h, 0)),
        ],
        out_specs=pl.BlockSpec((tm, d), lambda i, h: (i, 0)),
        out_shape=jax.ShapeDtypeStruct((t, d), F32),
        scratch_shapes=[pltpu.VMEM((tm, d), BF16)],
        compiler_params=_cparams(("parallel", "arbitrary")),
        name="xattn",
    )(h1, g, wq, kv3, kv3, wo)


def _slab_store(ref, val):
    n = val.shape[0]
    for c in range(SLAB):
        ref[pl.ds(c, n, stride=SLAB), :] = val[:, c * SLAB_W:(c + 1) * SLAB_W]


def _slab_load(buf, slot, first, n):
    return jnp.concatenate(
        [buf[slot, pl.ds(first * SLAB_PITCH + c, n, stride=SLAB_PITCH), :] for c in range(SLAB)], axis=1)


def _slab_copy(src_hbm, dst_buf, sem, slot, src_slab, dst_slab):
    return pltpu.make_async_copy(src_hbm.at[pl.ds(src_slab * SLAB, SLAB)],
                                 dst_buf.at[slot, pl.ds(dst_slab * SLAB_PITCH, SLAB)], sem.at[slot])


def _slab_wait(src_hbm, dst_buf, sem, slot, n):
    pltpu.make_async_copy(src_hbm.at[pl.ds(0, n * SLAB)], dst_buf.at[slot, pl.ds(0, n * SLAB)], sem.at[slot]).wait()


def _router_kernel(h_ref, g_ref, wr_ref, br_ref, xn_ref, ids_ref, wts_ref, cnt_ref, carry_ref):
    @pl.when(pl.program_id(0) == 0)
    def _():
        carry_ref[...] = jnp.zeros_like(carry_ref)

    xn = _rms(h_ref[...], g_ref[...])
    _slab_store(xn_ref, xn)
    hi = xn.astype(BF16)
    lo = (xn - hi.astype(F32)).astype(BF16)
    lg = _dot(hi, wr_ref[0]) + _dot(hi, wr_ref[1]) + _dot(lo, wr_ref[0]) + br_ref[...]
    lane = lax.broadcasted_iota(jnp.int32, lg.shape, 1).astype(F32)
    ninf = -jnp.inf
    big_lane = float(LANES)

    def first_max(v):
        mx = jnp.max(v, axis=-1, keepdims=True)
        return mx, jnp.min(jnp.where(v == mx, lane, big_lane), axis=-1, keepdims=True)

    gl = jnp.where(lane < N_GROUPS, lg, ninf)
    gmax, g_sel = first_max(gl)
    p_g = 1.0 / jnp.sum(jnp.exp(gl - gmax), axis=-1, keepdims=True)
    lo_l = N_GROUPS + EXPERTS_PER_GROUP * g_sel
    el = jnp.where((lane >= lo_l) & (lane < lo_l + EXPERTS_PER_GROUP), lg, ninf)
    emax, i1 = first_max(el)
    max2, i2 = first_max(jnp.where(lane == i1, ninf, el))
    e2 = jnp.exp(max2 - emax)
    w1 = p_g / (1.0 + e2)
    w2 = p_g * e2 / (1.0 + e2)
    e1, e2 = i1 - N_GROUPS, i2 - N_GROUPS
    oh1, oh2 = lane == e1, lane == e2
    oh = jnp.where(oh1, 1.0, 0.0) + jnp.where(oh2, 1.0, 0.0)
    tm = lg.shape[0]
    earlier = lax.broadcasted_iota(jnp.int32, (tm, tm), 1) < lax.broadcasted_iota(jnp.int32, (tm, tm), 0)
    carry = carry_ref[0:1, :]
    before = _dot(jnp.where(earlier, 1.0, 0.0).astype(BF16), oh.astype(BF16)) + carry
    r1 = jnp.sum(jnp.where(oh1, before, 0.0), axis=-1, keepdims=True)
    r2 = jnp.sum(jnp.where(oh2, before, 0.0), axis=-1, keepdims=True)
    counts = carry + jnp.sum(oh, axis=0, keepdims=True)
    carry_ref[...] = jnp.broadcast_to(counts, carry_ref.shape)
    cnt_ref[...] = jnp.broadcast_to(counts, cnt_ref.shape).astype(jnp.int32)
    ids = jnp.where(lane == 0.0, e1, jnp.where(lane == 1.0, e2, jnp.where(lane == 2.0, r1,
                                                                         jnp.where(lane == 3.0, r2, 0.0))))
    ids_ref[...] = ids.astype(jnp.int32)
    wts_ref[...] = jnp.where(lane == 0.0, w1, jnp.where(lane == 1.0, w2, 0.0))


def _router(h2, g, wr, br, tm):
    t, d = h2.shape
    return pl.pallas_call(
        _router_kernel,
        grid=(t // tm,),
        in_specs=[
            pl.BlockSpec((tm, d), lambda i: (i, 0)),
            pl.BlockSpec((1, d), lambda i: (0, 0)),
            pl.BlockSpec((2, d, LANES), lambda i: (0, 0, 0)),
            pl.BlockSpec((1, LANES), lambda i: (0, 0)),
        ],
        out_specs=[
            pl.BlockSpec((tm * SLAB, SLAB_W), lambda i: (i, 0)),
            pl.BlockSpec((tm, LANES), lambda i: (i, 0)),
            pl.BlockSpec((tm, LANES), lambda i: (i, 0)),
            pl.BlockSpec((8, LANES), lambda i: (0, 0)),
        ],
        out_shape=[jax.ShapeDtypeStruct((t * SLAB, SLAB_W), F32),
                   jax.ShapeDtypeStruct((t, LANES), jnp.int32),
                   jax.ShapeDtypeStruct((t, LANES), F32),
                   jax.ShapeDtypeStruct((8, LANES), jnp.int32)],
        scratch_shapes=[pltpu.VMEM((8, LANES), F32)],
        compiler_params=_cparams(("arbitrary",)),
        name="router",
    )(h2, g, wr, br)


def _experts_kernel(blk_e_ref, n_used_ref, tok_ref, tok_next_ref, x_hbm, wg_ref, wu_ref, wd_ref, y_ref,
                    xbuf, wg_b, wu_b, wd_b, sem, *, rows):
    j = pl.program_id(0)
    n_used = n_used_ref[0]
    used = j < n_used
    slot = j % 2

    @pl.when(j == 0)
    def _():
        def start(r, carry):
            _slab_copy(x_hbm, xbuf, sem, 0, tok_ref[0, 0, r], r).start()
            return carry
        lax.fori_loop(0, rows, start, 0)

    @pl.when(jnp.logical_not(used))
    def _():
        @pl.when(j == n_used)
        def _():
            _slab_wait(x_hbm, xbuf, sem, slot, rows)
        y_ref[...] = jnp.zeros_like(y_ref)

    @pl.when(used)
    def _():
        _slab_wait(x_hbm, xbuf, sem, slot, rows)

        @pl.when((j == 0) | (blk_e_ref[j] != blk_e_ref[jnp.maximum(j - 1, 0)]))
        def _():
            wg_b[...] = wg_ref[0].astype(BF16)
            wu_b[...] = wu_ref[0].astype(BF16)
            wd_b[...] = wd_ref[0].astype(BF16)

        for r in range(rows):
            _slab_copy(x_hbm, xbuf, sem, 1 - slot, tok_next_ref[0, 0, r], r).start()
        xb = _slab_load(xbuf, slot, 0, rows).astype(BF16)
        gate = _dot(xb, wg_b[...])
        up = _dot(xb, wu_b[...])
        act = (gate * jax.nn.sigmoid(gate) * up).astype(BF16)
        _slab_store(y_ref, _dot(act, wd_b[...]))

        @pl.when(j == pl.num_programs(0) - 1)
        def _():
            _slab_wait(x_hbm, xbuf, sem, 1 - slot, rows)


def _experts(blk_e, n_used, row_tok3, xn_slab, wg, wu, wd):
    n_blocks, _, rows = row_tok3.shape
    _, d, de = wg.shape
    grid_spec = pltpu.PrefetchScalarGridSpec(
        num_scalar_prefetch=2,
        grid=(n_blocks,),
        in_specs=[
            pl.BlockSpec((1, 1, rows), lambda j, be, nu: (j, 0, 0), memory_space=pltpu.SMEM),
            pl.BlockSpec((1, 1, rows), lambda j, be, nu: (jnp.minimum(j + 1, n_blocks - 1), 0, 0),
                         memory_space=pltpu.SMEM),
            pl.BlockSpec(memory_space=pl.ANY),
            pl.BlockSpec((1, d, de), lambda j, be, nu: (be[j], 0, 0)),
            pl.BlockSpec((1, d, de), lambda j, be, nu: (be[j], 0, 0)),
            pl.BlockSpec((1, de, d), lambda j, be, nu: (be[j], 0, 0)),
        ],
        out_specs=pl.BlockSpec((rows * SLAB, SLAB_W), lambda j, be, nu: (j, 0)),
        scratch_shapes=[pltpu.VMEM((2, rows * SLAB_PITCH, SLAB_W), F32),
                        pltpu.VMEM((d, de), BF16), pltpu.VMEM((d, de), BF16), pltpu.VMEM((de, d), BF16),
                        pltpu.SemaphoreType.DMA((2,))],
    )
    return pl.pallas_call(
        functools.partial(_experts_kernel, rows=rows),
        grid_spec=grid_spec,
        out_shape=jax.ShapeDtypeStruct((n_blocks * rows * SLAB, SLAB_W), F32),
        compiler_params=_cparams(("arbitrary",)),
        name="experts",
    )(blk_e, n_used, row_tok3, row_tok3, xn_slab, wg, wu, wd)


def _combine_kernel(pos_ref, pos_next_ref, y_hbm, wts_ref, h_ref, g_ref, o_ref, ybuf, sem, *, tc):
    j = pl.program_id(0)
    slot = j % 2

    @pl.when(j == 0)
    def _():
        def start(r, carry):
            _slab_copy(y_hbm, ybuf, sem, 0, pos_ref[0, 0, 2 * r], r).start()
            _slab_copy(y_hbm, ybuf, sem, 0, pos_ref[0, 0, 2 * r + 1], tc + r).start()
            return carry
        lax.fori_loop(0, tc, start, 0)

    _slab_wait(y_hbm, ybuf, sem, slot, 2 * tc)
    for r in range(tc):
        _slab_copy(y_hbm, ybuf, sem, 1 - slot, pos_next_ref[0, 0, 2 * r], r).start()
        _slab_copy(y_hbm, ybuf, sem, 1 - slot, pos_next_ref[0, 0, 2 * r + 1], tc + r).start()
    w = wts_ref[...]
    y = _slab_load(ybuf, slot, 0, tc) * w[:, 0:1] + _slab_load(ybuf, slot, tc, tc) * w[:, 1:2]
    o_ref[...] = _rms(h_ref[...] + y, g_ref[...])

    @pl.when(j == pl.num_programs(0) - 1)
    def _():
        _slab_wait(y_hbm, ybuf, sem, 1 - slot, 2 * tc)


def _combine(pos3, y_slab, wts, h2, g, tc):
    t, d = h2.shape
    n = t // tc
    return pl.pallas_call(
        functools.partial(_combine_kernel, tc=tc),
        grid=(n,),
        in_specs=[
            pl.BlockSpec((1, 1, 2 * tc), lambda i: (i, 0, 0), memory_space=pltpu.SMEM),
            pl.BlockSpec((1, 1, 2 * tc), lambda i: (jnp.minimum(i + 1, n - 1), 0, 0), memory_space=pltpu.SMEM),
            pl.BlockSpec(memory_space=pl.ANY),
            pl.BlockSpec((tc, LANES), lambda i: (i, 0)),
            pl.BlockSpec((tc, d), lambda i: (i, 0)),
            pl.BlockSpec((1, d), lambda i: (0, 0)),
        ],
        out_specs=pl.BlockSpec((tc, d), lambda i: (i, 0)),
        out_shape=jax.ShapeDtypeStruct((t, d), F32),
        scratch_shapes=[pltpu.VMEM((2, 2 * tc * SLAB_PITCH, SLAB_W), F32), pltpu.SemaphoreType.DMA((2,))],
        compiler_params=_cparams(("arbitrary",)),
        name="combine",
    )(pos3, pos3, y_slab, wts, h2, g)


def _dispatch_plan(ids, counts, rows):
    flat_e = ids[:, 0:2].reshape(-1)
    rank = ids[:, 2:4].reshape(-1)
    n_asg = flat_e.shape[0]
    experts = jnp.arange(N_EXPERTS, dtype=jnp.int32)
    padded = ((counts + rows - 1) // rows) * rows
    pends = jnp.cumsum(padded)
    pstarts = pends - padded
    pos = jnp.sum(jnp.where(flat_e[:, None] == experts[None, :], pstarts[None, :], 0), axis=1) + rank
    n_rows = n_asg + N_EXPERTS * rows
    n_blocks = n_rows // rows
    row_tok = jnp.zeros((n_rows,), jnp.int32).at[pos].set(jnp.arange(n_asg, dtype=jnp.int32) // 2)
    starts = jnp.arange(n_blocks, dtype=jnp.int32) * rows
    blk_e = jnp.minimum(jnp.sum((pends[None, :] <= starts[:, None]).astype(jnp.int32), axis=1), N_EXPERTS - 1)
    n_used = (pends[-1] // rows).astype(jnp.int32).reshape(1)
    return pos.astype(jnp.int32), row_tok.reshape(n_blocks, 1, rows), blk_e.astype(jnp.int32), n_used


def _tile(n, pref):
    return pref if n % pref == 0 else n


def _layer(h, mem, p, l):
    bsz, seq, d = h.shape
    t = bsz * seq
    x2 = h.reshape(t, d)
    row = lambda v: v.reshape(1, -1)
    tm = _tile(t, 1024)

    w_in = p['w_in'][l]
    w_big = jnp.concatenate([w_in[:, 0:3072], w_in[:, 3104:5152], w_in[:, 5168:9264]], axis=1).astype(BF16)
    w_small = jnp.concatenate([w_in[:, 3072:3104], w_in[:, 5152:5168],
                               jnp.zeros((d, LANES - 48), F32)], axis=1).astype(BF16)
    big, small = _in_proj(x2, row(p['norm_mix'][l]), w_big, w_small, tm, 1024)
    big3 = big.reshape(bsz, seq, N_BIG)
    sm3 = small.reshape(bsz, seq, LANES)

    def lr_pad(w, off):
        return jnp.zeros((LANES, GLA_K), F32).at[off:off + GLA_RANK].set(w).astype(BF16)

    o_f = _gla_scan(big3, sm3, lr_pad(p['gla_w_lr_f'][l], SM_LRF), row(p['gla_b_lr_f'][l]), rev=False)
    y_a = _gla_scan(big3, sm3, lr_pad(p['gla_w_lr_b'][l], SM_LRB), row(p['gla_b_lr_b'][l]), rev=True,
                    prev=o_f, gn=row(p['gla_norm'][l]))

    cw = jnp.zeros((8, MLSTM_W), F32).at[:CONV_WIDTH].set(p['conv_w'][l].reshape(CONV_WIDTH, MLSTM_W))
    qm, km, vm = _mlstm_pre(big3, cw, row(p['conv_b'][l]), p['m_wq'][l].astype(BF16),
                            p['m_wk'][l].astype(BF16), p['m_wv'][l].astype(BF16), _tile(seq, 1024))
    gbias = jnp.zeros((1, LANES), F32).at[0, SM_GATES:SM_GATES + 4 * MLSTM_HEADS].set(
        p['m_gate_bias'][l].reshape(-1))
    h_f = _mlstm_scan(qm, km, vm, sm3, gbias, rev=False)
    y_b = _mlstm_scan(qm, km, vm, sm3, gbias, rev=True, prev=h_f, big3=big3, mn=row(p['m_norm'][l]))

    merged = _merge(y_a.reshape(t, GLA_V), y_b.reshape(t, MLSTM_W), p['w_branch_a'][l].astype(BF16),
                    p['w_branch_b'][l].astype(BF16), big, tm, 1024)
    h1 = _matmul_res(merged, p['w_mix_out'][l].astype(BF16), x2, tm, 1024)

    n_mem = mem.shape[1]
    kv = _kv_proj(mem.reshape(bsz * n_mem, d), row(p['norm_mem'][l]), p['w_xkv'][l].astype(BF16), 1024)
    h2 = _xattn(h1, row(p['norm_xattn'][l]), p['w_xq'][l].astype(BF16), kv.reshape(bsz, n_mem, 2 * d),
                p['w_xo'][l].astype(BF16), seq, _tile(seq, 512))

    wr = jnp.concatenate([p['w_group'][l], p['w_router'][l].transpose(1, 0, 2).reshape(d, N_EXPERTS),
                          jnp.zeros((d, LANES - N_GROUPS - N_EXPERTS), F32)], axis=1)
    wr_hi = wr.astype(BF16)
    wr2 = jnp.stack([wr_hi, (wr - wr_hi.astype(F32)).astype(BF16)])
    br = jnp.concatenate([p['b_group'][l], p['b_router'][l].reshape(-1),
                          jnp.zeros((LANES - N_GROUPS - N_EXPERTS,), F32)]).reshape(1, LANES)
    xn3, ids, wts, cnt = _router(h2, row(p['norm_ffn'][l]), wr2, br, tm)
    pos, row_tok3, blk_e, n_used = _dispatch_plan(ids, cnt[0, :N_EXPERTS], MOE_ROWS)
    y_rows = _experts(blk_e, n_used, row_tok3, xn3, p['w_gate'][l], p['w_up'][l], p['w_down'][l])
    tc = _tile(t, COMBINE_TOK)
    return pos.reshape(t // tc, 1, 2 * tc), y_rows, wts, h2, tc


def kernel(x, mem, norm_mix, w_in, gla_w_lr_f, gla_b_lr_f, gla_w_lr_b, gla_b_lr_b, gla_norm, conv_w, conv_b, m_wq, m_wk, m_wv, m_gate_bias, m_norm, w_branch_a, w_branch_b, w_mix_out, norm_xattn, norm_mem, w_xq, w_xkv, w_xo, norm_ffn, w_group, b_group, w_router, b_router, w_gate, w_up, w_down, norm_final):
    p = dict(norm_mix=norm_mix, w_in=w_in, gla_w_lr_f=gla_w_lr_f, gla_b_lr_f=gla_b_lr_f, gla_w_lr_b=gla_w_lr_b,
             gla_b_lr_b=gla_b_lr_b, gla_norm=gla_norm, conv_w=conv_w, conv_b=conv_b, m_wq=m_wq, m_wk=m_wk,
             m_wv=m_wv, m_gate_bias=m_gate_bias, m_norm=m_norm, w_branch_a=w_branch_a, w_branch_b=w_branch_b,
             w_mix_out=w_mix_out, norm_xattn=norm_xattn, norm_mem=norm_mem, w_xq=w_xq, w_xkv=w_xkv, w_xo=w_xo,
             norm_ffn=norm_ffn, w_group=w_group, b_group=b_group, w_router=w_router, b_router=b_router,
             w_gate=w_gate, w_up=w_up, w_down=w_down)
    bsz, seq, d = x.shape
    depth = norm_mix.shape[0]
    assert depth == 1, "the final norm is fused into the last layer's combine step"
    pos3, y_rows, wts, h2, tc = _layer(x, mem, p, 0)
    out = _combine(pos3, y_rows, wts, h2, norm_final.reshape(1, d), tc)
    return out.reshape(bsz, seq, d)
```

```python
import functools

import jax
import jax.numpy as jnp
from jax import lax
from jax.experimental import pallas as pl
from jax.experimental.pallas import tpu as pltpu

F32 = jnp.float32
BF16 = jnp.bfloat16

EPS = 1e-6
LOG2_E = 1.4426950408889634
D_MODEL = 2048

GLA_HEADS = 4
GLA_DK = 128
GLA_DV = 256
GLA_K = GLA_HEADS * GLA_DK
GLA_V = GLA_HEADS * GLA_DV
GLA_RANK = 16
GLA_TAU = 16.0
GLA_CHUNK = 64
GLA_SUB = 16

MLSTM_HEADS = 4
MLSTM_DH = 256
MLSTM_W = MLSTM_HEADS * MLSTM_DH
CONV_WIDTH = 5
MLSTM_CHUNK = 256
CONV_HALO = 16

XATTN_HEADS = 4
XATTN_DH = D_MODEL // XATTN_HEADS

N_GROUPS = 4
EXPERTS_PER_GROUP = 8
N_EXPERTS = N_GROUPS * EXPERTS_PER_GROUP
D_EXPERT = 512
MOE_ROWS = 256
COMBINE_TOK = 256

LANES = 128
SLAB_W = LANES
SLAB = D_MODEL // SLAB_W
SLAB_PITCH = SLAB + 8
SLAB_DTYPE = F32
GATHER_SLOTS = 3
GATHER_BATCHES = 8

OFF_Q, OFF_K, OFF_V, OFF_GG = 0, 512, 1024, 2048
OFF_MX, OFF_MZ, OFF_GA, OFF_GB = 3072, 4096, 5120, 7168
N_BIG = 9216
SM_LRF, SM_LRB, SM_GATES = 0, 16, 32

VMEM_LIMIT = 56 * 1024 * 1024


def _cparams(sem):
    return pltpu.CompilerParams(dimension_semantics=sem, vmem_limit_bytes=VMEM_LIMIT)


def _rms(x, g):
    return x * lax.rsqrt(jnp.mean(x * x, axis=-1, keepdims=True) + EPS) * g


def _log_sigmoid(x):
    return jnp.minimum(x, 0.0) - jnp.log(1.0 + jnp.exp(-jnp.abs(x)))


def _dot(a, b):
    return jnp.dot(a, b, preferred_element_type=F32)


def _dot_nt(a, b):
    return lax.dot_general(a, b, (((1,), (1,)), ((), ())), preferred_element_type=F32)


def _dot_tn(a, b):
    return lax.dot_general(a, b, (((0,), (0,)), ((), ())), preferred_element_type=F32)


def _order_mask(c, rev):
    t = lax.broadcasted_iota(jnp.int32, (c, c), 0)
    s = lax.broadcasted_iota(jnp.int32, (c, c), 1)
    return (s >= t) if rev else (s <= t)


def _cumsum_mm(mask_bf16, x):
    hi = x.astype(BF16)
    r1 = x - hi.astype(F32)
    mid = r1.astype(BF16)
    lo = (r1 - mid.astype(F32)).astype(BF16)
    return _dot(mask_bf16, hi) + _dot(mask_bf16, mid) + _dot(mask_bf16, lo)


def _in_proj_kernel(x_ref, g_ref, wbig_ref, wsm_ref, big_ref, sm_ref, xn_ref):
    @pl.when(pl.program_id(1) == 0)
    def _():
        xn = _rms(x_ref[...], g_ref[...]).astype(BF16)
        xn_ref[...] = xn
        sm_ref[...] = _dot(xn, wsm_ref[...])
    big_ref[...] = _dot(xn_ref[...], wbig_ref[...]).astype(BF16)


def _in_proj(x2, g, w_big, w_small, tm, tn):
    t, d = x2.shape
    n = w_big.shape[1]
    return pl.pallas_call(
        _in_proj_kernel,
        grid=(t // tm, n // tn),
        in_specs=[
            pl.BlockSpec((tm, d), lambda i, j: (i, 0)),
            pl.BlockSpec((1, d), lambda i, j: (0, 0)),
            pl.BlockSpec((d, tn), lambda i, j: (0, j)),
            pl.BlockSpec((d, LANES), lambda i, j: (0, 0)),
        ],
        out_specs=[
            pl.BlockSpec((tm, tn), lambda i, j: (i, j)),
            pl.BlockSpec((tm, LANES), lambda i, j: (i, 0)),
        ],
        out_shape=[jax.ShapeDtypeStruct((t, n), BF16), jax.ShapeDtypeStruct((t, LANES), F32)],
        scratch_shapes=[pltpu.VMEM((tm, d), BF16)],
        compiler_params=_cparams(("parallel", "arbitrary")),
        name="in_proj",
    )(x2, g, w_big, w_small)


def _gla_kernel(q_ref, k_ref, v_ref, sm_ref, wlr_ref, blr_ref, *rest, rev, final, c):
    if final:
        oprev_ref, gg_ref, gn_ref, o_ref, st_ref = rest
    else:
        o_ref, st_ref = rest
    sb = GLA_SUB

    @pl.when(pl.program_id(1) == 0)
    def _():
        st_ref[...] = jnp.zeros_like(st_ref)

    x = _dot(sm_ref[0].astype(BF16), wlr_ref[...]) + blr_ref[...]
    la = _log_sigmoid(x) * (1.0 / GLA_TAU)
    mask_b = jnp.where(_order_mask(c, rev), 1.0, 0.0).astype(BF16)
    b = _cumsum_mm(mask_b, la) * LOG2_E
    last = 0 if rev else c - 1
    tot = b[last:last + 1, :]
    q = q_ref[0].astype(F32) * (GLA_DK ** -0.5)
    k = k_ref[0].astype(F32)
    q_in = (q * jnp.exp2(b)).astype(BF16)
    k_dec = (k * jnp.exp2(tot - b)).astype(BF16)
    e_tot = jnp.exp2(tot)

    col = lax.broadcasted_iota(jnp.int32, (sb, c), 1)
    trow = lax.broadcasted_iota(jnp.int32, (sb, c), 0)

    for h in range(GLA_HEADS):
        ks = slice(h * GLA_DK, (h + 1) * GLA_DK)
        vs = slice(h * GLA_DV, (h + 1) * GLA_DV)
        vh = v_ref[0, :, vs]
        kh_b = k_ref[0, :, ks]
        st = st_ref[h]
        o_inter = _dot_nt(q_in[:, ks], st.astype(BF16))
        st_ref[h] = st * e_tot[:, ks] + _dot_tn(vh, k_dec[:, ks])

        bh, qh, kh = b[:, ks], q[:, ks], k[:, ks]
        rows = []
        for i in range(c // sb):
            r0 = i * sb
            ref_row = r0 + (sb - 1 if rev else 0)
            beta = bh[ref_row:ref_row + 1, :]
            bb, qb = bh[r0:r0 + sb], qh[r0:r0 + sb]
            qt = (qb * jnp.exp2(bb - beta)).astype(BF16)
            kt = (kh * jnp.exp2(beta - bh)).astype(BF16)
            a_off = _dot_nt(qt, kt)
            slabs = [(qb * jnp.exp2(bb - bb[s:s + 1, :])).astype(BF16) for s in range(sb)]
            g = _dot_nt(jnp.concatenate(slabs, axis=0), kh_b)
            a_diag = jnp.zeros((sb, c), F32)
            for s in range(sb):
                a_diag = a_diag + jnp.where(col == r0 + s, g[s * sb:(s + 1) * sb], 0.0)
            tr = trow + r0
            if rev:
                off_mask = col >= r0 + sb
                diag_mask = (col >= tr) & (col < r0 + sb)
            else:
                off_mask = col < r0
                diag_mask = (col <= tr) & (col >= r0)
            rows.append(jnp.where(off_mask, a_off, 0.0) + jnp.where(diag_mask, a_diag, 0.0))
        a = jnp.concatenate(rows, axis=0).astype(BF16)
        o = o_inter + _dot(a, vh)
        if final:
            o = o + oprev_ref[0, :, vs]
            y = _rms(o, gn_ref[:, vs])
            gg = gg_ref[0, :, vs].astype(F32)
            o_ref[0, :, vs] = (y * (gg * jax.nn.sigmoid(gg))).astype(o_ref.dtype)
        else:
            o_ref[0, :, vs] = o.astype(o_ref.dtype)


def _gla_scan(big3, sm3, wlr, blr, rev, prev=None, gn=None):
    bsz, seq, _ = big3.shape
    c = GLA_CHUNK
    n = seq // c
    final = prev is not None

    def cm(ci):
        return (n - 1 - ci) if rev else ci

    in_specs = [
        pl.BlockSpec((1, c, GLA_K), lambda b, ci: (b, cm(ci), OFF_Q // GLA_K)),
        pl.BlockSpec((1, c, GLA_K), lambda b, ci: (b, cm(ci), OFF_K // GLA_K)),
        pl.BlockSpec((1, c, GLA_V), lambda b, ci: (b, cm(ci), OFF_V // GLA_V)),
        pl.BlockSpec((1, c, LANES), lambda b, ci: (b, cm(ci), 0)),
        pl.BlockSpec((LANES, GLA_K), lambda b, ci: (0, 0)),
        pl.BlockSpec((1, GLA_K), lambda b, ci: (0, 0)),
    ]
    args = [big3, big3, big3, sm3, wlr, blr]
    if final:
        in_specs += [
            pl.BlockSpec((1, c, GLA_V), lambda b, ci: (b, cm(ci), 0)),
            pl.BlockSpec((1, c, GLA_V), lambda b, ci: (b, cm(ci), OFF_GG // GLA_V)),
            pl.BlockSpec((1, GLA_V), lambda b, ci: (0, 0)),
        ]
        args += [prev, big3, gn]
    return pl.pallas_call(
        functools.partial(_gla_kernel, rev=rev, final=final, c=c),
        grid=(bsz, n),
        in_specs=in_specs,
        out_specs=pl.BlockSpec((1, c, GLA_V), lambda b, ci: (b, cm(ci), 0)),
        out_shape=jax.ShapeDtypeStruct((bsz, seq, GLA_V), BF16 if final else F32),
        scratch_shapes=[pltpu.VMEM((GLA_HEADS, GLA_DV, GLA_DK), F32)],
        compiler_params=_cparams(("parallel", "arbitrary")),
        name="gla_bwd" if rev else "gla_fwd",
    )(*args)


def _mlstm_pre_kernel(cur_ref, prev_ref, next_ref, cw_ref, cb_ref, wq_ref, wk_ref, wv_ref,
                      q_ref, k_ref, v_ref, *, tm):
    i = pl.program_id(2)
    cur_b = cur_ref[0]
    cur = cur_b.astype(F32)
    halo = CONV_WIDTH // 2
    prev = jnp.where(i > 0, prev_ref[0].astype(F32), 0.0)
    nxt = jnp.where(i < pl.num_programs(2) - 1, next_ref[0].astype(F32), 0.0)
    ext = jnp.concatenate([prev[CONV_HALO - 8:], cur, nxt[:8]], axis=0)
    acc = jnp.zeros_like(cur) + cb_ref[...]
    for w in range(CONV_WIDTH):
        off = 8 - halo + w
        acc = acc + ext[off:off + tm] * cw_ref[w:w + 1, :]
    xc = (acc * jax.nn.sigmoid(acc)).astype(BF16)
    q_ref[0] = _dot(xc, wq_ref[0]).astype(BF16)
    k_ref[0] = (_dot(xc, wk_ref[0]) * (MLSTM_DH ** -0.5)).astype(BF16)
    v_ref[0] = _dot(cur_b, wv_ref[0]).astype(BF16)


def _mlstm_pre(big3, cw, cb, wq, wk, wv, tm):
    bsz, seq, _ = big3.shape
    dh = MLSTM_DH
    nh = tm // CONV_HALO
    n_halo = seq // CONV_HALO
    c0 = OFF_MX // dh
    out = jax.ShapeDtypeStruct((bsz, seq, MLSTM_W), BF16)
    ospec = pl.BlockSpec((1, tm, dh), lambda b, h, i: (b, i, h))
    wspec = pl.BlockSpec((1, dh, dh), lambda b, h, i: (h, 0, 0))
    return pl.pallas_call(
        functools.partial(_mlstm_pre_kernel, tm=tm),
        grid=(bsz, MLSTM_HEADS, seq // tm),
        in_specs=[
            pl.BlockSpec((1, tm, dh), lambda b, h, i: (b, i, c0 + h)),
            pl.BlockSpec((1, CONV_HALO, dh), lambda b, h, i: (b, jnp.maximum(i * nh - 1, 0), c0 + h)),
            pl.BlockSpec((1, CONV_HALO, dh), lambda b, h, i: (b, jnp.minimum((i + 1) * nh, n_halo - 1), c0 + h)),
            pl.BlockSpec((8, dh), lambda b, h, i: (0, h)),
            pl.BlockSpec((1, dh), lambda b, h, i: (0, h)),
            wspec, wspec, wspec,
        ],
        out_specs=[ospec, ospec, ospec],
        out_shape=[out, out, out],
        compiler_params=_cparams(("parallel", "parallel", "parallel")),
        name="mlstm_pre",
    )(big3, big3, big3, cw, cb, wq, wk, wv)


def _mlstm_kernel(q_ref, k_ref, v_ref, sm_ref, gb_ref, *rest, rev, final, c):
    if final:
        hprev_ref, mz_ref, mn_ref, o_ref, c_ref, n_ref, m_ref = rest
    else:
        o_ref, c_ref, n_ref, m_ref = rest

    @pl.when(pl.program_id(1) == 0)
    def _():
        c_ref[...] = jnp.zeros_like(c_ref)
        n_ref[...] = jnp.zeros_like(n_ref)
        m_ref[...] = jnp.zeros_like(m_ref)

    g = sm_ref[0] + gb_ref[...]
    mask = _order_mask(c, rev)
    bcum = _cumsum_mm(jnp.where(mask, 1.0, 0.0).astype(BF16), _log_sigmoid(g))
    g_t = g.T
    b_t = bcum.T
    last = 0 if rev else c - 1
    dh = MLSTM_DH

    for h in range(MLSTM_HEADS):
        ci = SM_GATES + (2 * MLSTM_HEADS if rev else 0) + h
        cf = ci + MLSTM_HEADS
        hs = slice(h * dh, (h + 1) * dh)
        i_col, b_col = g[:, ci:ci + 1], bcum[:, cf:cf + 1]
        i_row, b_row = g_t[ci:ci + 1, :], b_t[cf:cf + 1, :]
        tot = b_col[last:last + 1, :]
        m_prev = m_ref[h, 0:1, 0:1]
        qh, kh, vh = q_ref[0, :, hs], k_ref[0, :, hs], v_ref[0, :, hs]
        c_st = c_ref[h]
        n_st = n_ref[h, 0:1, :]

        dmat = jnp.where(mask, b_col - b_row + i_row, -jnp.inf)
        inter_log = b_col + m_prev
        m_t = jnp.maximum(inter_log, jnp.max(dmat, axis=-1, keepdims=True))
        w_intra = jnp.exp(dmat - m_t)
        w_inter = jnp.exp(inter_log - m_t)
        s = _dot_nt(qh, kh) * w_intra
        num = w_inter * _dot(qh, c_st.astype(BF16)) + _dot(s.astype(BF16), vh)
        den = (w_inter * jnp.sum(qh.astype(F32) * n_st, axis=-1, keepdims=True)
               + jnp.sum(s, axis=-1, keepdims=True))
        hh = num / jnp.maximum(jnp.abs(den), jnp.exp(-m_t))

        upd_col = tot - b_col + i_col
        upd_row = tot - b_row + i_row
        m_new = jnp.maximum(tot + m_prev, jnp.max(upd_row, axis=-1, keepdims=True))
        w_old = jnp.exp(tot + m_prev - m_new)
        kw = kh.astype(F32) * jnp.exp(upd_col - m_new)
        c_ref[h] = w_old * c_st + _dot_tn(kw.astype(BF16), vh)
        n_ref[h] = jnp.broadcast_to(w_old * n_st + jnp.sum(kw, axis=0, keepdims=True), (8, dh))
        m_ref[h] = jnp.broadcast_to(m_new, (8, LANES))

        if final:
            hh = hh + hprev_ref[0, :, hs]
            y = _rms(hh, mn_ref[:, hs])
            o_ref[0, :, hs] = (y * jax.nn.sigmoid(mz_ref[0, :, hs].astype(F32))).astype(o_ref.dtype)
        else:
            o_ref[0, :, hs] = hh.astype(o_ref.dtype)


def _mlstm_scan(qm, km, vm, sm3, gbias, rev, prev=None, big3=None, mn=None):
    bsz, seq, w = qm.shape
    c = min(MLSTM_CHUNK, seq)
    n = seq // c
    final = prev is not None

    def cm(ci):
        return (n - 1 - ci) if rev else ci

    xspec = pl.BlockSpec((1, c, w), lambda b, ci: (b, cm(ci), 0))
    in_specs = [xspec, xspec, xspec,
                pl.BlockSpec((1, c, LANES), lambda b, ci: (b, cm(ci), 0)),
                pl.BlockSpec((1, LANES), lambda b, ci: (0, 0))]
    args = [qm, km, vm, sm3, gbias]
    if final:
        in_specs += [xspec,
                     pl.BlockSpec((1, c, w), lambda b, ci: (b, cm(ci), OFF_MZ // MLSTM_W)),
                     pl.BlockSpec((1, w), lambda b, ci: (0, 0))]
        args += [prev, big3, mn]
    return pl.pallas_call(
        functools.partial(_mlstm_kernel, rev=rev, final=final, c=c),
        grid=(bsz, n),
        in_specs=in_specs,
        out_specs=xspec,
        out_shape=jax.ShapeDtypeStruct((bsz, seq, w), BF16 if final else F32),
        scratch_shapes=[pltpu.VMEM((MLSTM_HEADS, MLSTM_DH, MLSTM_DH), F32),
                        pltpu.VMEM((MLSTM_HEADS, 8, MLSTM_DH), F32),
                        pltpu.VMEM((MLSTM_HEADS, 8, LANES), F32)],
        compiler_params=_cparams(("parallel", "arbitrary")),
        name="mlstm_bwd" if rev else "mlstm_fwd",
    )(*args)


def _merge_kernel(ya_ref, yb_ref, wa_ref, wb_ref, ga_ref, gb_ref, o_ref):
    a = _dot(ya_ref[...], wa_ref[...])
    b = _dot(yb_ref[...], wb_ref[...])
    ga = jax.nn.sigmoid(ga_ref[...].astype(F32))
    gb = jax.nn.sigmoid(gb_ref[...].astype(F32))
    o_ref[...] = (ga * a + gb * b).astype(o_ref.dtype)


def _merge(ya, yb, wa, wb, big, tm, tn):
    t, kdim = ya.shape
    n = wa.shape[1]
    return pl.pallas_call(
        _merge_kernel,
        grid=(t // tm, n // tn),
        in_specs=[
            pl.BlockSpec((tm, kdim), lambda i, j: (i, 0)),
            pl.BlockSpec((tm, kdim), lambda i, j: (i, 0)),
            pl.BlockSpec((kdim, tn), lambda i, j: (0, j)),
            pl.BlockSpec((kdim, tn), lambda i, j: (0, j)),
            pl.BlockSpec((tm, tn), lambda i, j: (i, OFF_GA // tn + j)),
            pl.BlockSpec((tm, tn), lambda i, j: (i, OFF_GB // tn + j)),
        ],
        out_specs=pl.BlockSpec((tm, tn), lambda i, j: (i, j)),
        out_shape=jax.ShapeDtypeStruct((t, n), BF16),
        compiler_params=_cparams(("parallel", "parallel")),
        name="merge",
    )(ya, yb, wa, wb, big, big)


def _matmul_res_kernel(a_ref, w_ref, r_ref, o_ref):
    o_ref[...] = r_ref[...] + _dot(a_ref[...], w_ref[...])


def _matmul_res(a, w, res, tm, tn):
    t, kdim = a.shape
    n = w.shape[1]
    return pl.pallas_call(
        _matmul_res_kernel,
        grid=(t // tm, n // tn),
        in_specs=[
            pl.BlockSpec((tm, kdim), lambda i, j: (i, 0)),
            pl.BlockSpec((kdim, tn), lambda i, j: (0, j)),
            pl.BlockSpec((tm, tn), lambda i, j: (i, j)),
        ],
        out_specs=pl.BlockSpec((tm, tn), lambda i, j: (i, j)),
        out_shape=jax.ShapeDtypeStruct((t, n), F32),
        compiler_params=_cparams(("parallel", "parallel")),
        name="mix_out",
    )(a, w, res)


def _kv_kernel(mem_ref, g_ref, w_ref, o_ref):
    mn = _rms(mem_ref[...], g_ref[...]).astype(BF16)
    o_ref[...] = _dot(mn, w_ref[...]).astype(BF16)


def _kv_proj(mem2, g, w, tn):
    r, d = mem2.shape
    n = w.shape[1]
    return pl.pallas_call(
        _kv_kernel,
        grid=(n // tn,),
        in_specs=[
            pl.BlockSpec((r, d), lambda j: (0, 0)),
            pl.BlockSpec((1, d), lambda j: (0, 0)),
            pl.BlockSpec((d, tn), lambda j: (0, j)),
        ],
        out_specs=pl.BlockSpec((r, tn), lambda j: (0, j)),
        out_shape=jax.ShapeDtypeStruct((r, n), BF16),
        compiler_params=_cparams(("parallel",)),
        name="kv_proj",
    )(mem2, g, w)


def _xattn_kernel(h_ref, g_ref, wq_ref, k_ref, v_ref, wo_ref, o_ref, xn_ref):
    @pl.when(pl.program_id(1) == 0)
    def _():
        x = h_ref[...]
        xn_ref[...] = _rms(x, g_ref[...]).astype(BF16)
        o_ref[...] = x
    q = _dot(xn_ref[...], wq_ref[...]).astype(BF16)
    s = _dot_nt(q, k_ref[0]) * (XATTN_DH ** -0.5)
    p = jnp.exp(s - jnp.max(s, axis=-1, keepdims=True))
    p = p / jnp.sum(p, axis=-1, keepdims=True)
    o = _dot(p.astype(BF16), v_ref[0]).astype(BF16)
    o_ref[...] += _dot(o, wo_ref[...])


def _xattn(h1, g, wq, kv3, wo, seq, tm):
    t, d = h1.shape
    n_mem = kv3.shape[1]
    dh = XATTN_DH
    per_b = seq // tm
    return pl.pallas_call(
        _xattn_kernel,
        grid=(t // tm, XATTN_HEADS),
        in_specs=[
            pl.BlockSpec((tm, d), lambda i, h: (i, 0)),
            pl.BlockSpec((1, d), lambda i, h: (0, 0)),
            pl.BlockSpec((d, dh), lambda i, h: (0, h)),
            pl.BlockSpec((1, n_mem, dh), lambda i, h: (i // per_b, 0, h)),
            pl.BlockSpec((1, n_mem, dh), lambda i, h: (i // per_b, 0, XATTN_HEADS + h)),
            pl.BlockSpec((dh, d), lambda i, h: (h, 0)),
        ],
        out_specs=pl.BlockSpec((tm, d), lambda i, h: (i, 0)),
        out_shape=jax.ShapeDtypeStruct((t, d), F32),
        scratch_shapes=[pltpu.VMEM((tm, d), BF16)],
        compiler_params=_cparams(("parallel", "arbitrary")),
        name="xattn",
    )(h1, g, wq, kv3, kv3, wo)


def _slab_store(ref, val):
    n = val.shape[0]
    for c in range(SLAB):
        ref[pl.ds(c, n, stride=SLAB), :] = val[:, c * SLAB_W:(c + 1) * SLAB_W]


def _slab_load(buf, slot, first, n):
    return jnp.concatenate(
        [buf[slot, pl.ds(first * SLAB_PITCH + c, n, stride=SLAB_PITCH), :] for c in range(SLAB)], axis=1)


def _slab_copy(src_hbm, dst_buf, sem, slot, src_slab, dst_slab):
    return pltpu.make_async_copy(src_hbm.at[pl.ds(src_slab * SLAB, SLAB)],
                                 dst_buf.at[slot, pl.ds(dst_slab * SLAB_PITCH, SLAB)], sem.at[slot])


def _zero_after(v):
    bits = lax.bitcast_convert_type(v[-1:, -1:], jnp.uint32)
    return ((bits >> 16) >> 16).astype(jnp.int32)[0, 0]


def _slab_wait(src_hbm, dst_buf, sem, slot, n):
    pltpu.make_async_copy(src_hbm.at[pl.ds(0, n * SLAB)], dst_buf.at[slot, pl.ds(0, n * SLAB)], sem.at[slot]).wait()


def _router_kernel(h_ref, g_ref, wr_ref, br_ref, xn_ref, ids_ref, wts_ref, cnt_ref, carry_ref):
    @pl.when(pl.program_id(0) == 0)
    def _():
        carry_ref[...] = jnp.zeros_like(carry_ref)

    xn = _rms(h_ref[...], g_ref[...])
    _slab_store(xn_ref, xn)
    hi = xn.astype(BF16)
    lo = (xn - hi.astype(F32)).astype(BF16)
    lg = _dot(hi, wr_ref[0]) + _dot(hi, wr_ref[1]) + _dot(lo, wr_ref[0]) + br_ref[...]
    lane = lax.broadcasted_iota(jnp.int32, lg.shape, 1).astype(F32)
    ninf = -jnp.inf
    big_lane = float(LANES)

    def first_max(v):
        mx = jnp.max(v, axis=-1, keepdims=True)
        return mx, jnp.min(jnp.where(v == mx, lane, big_lane), axis=-1, keepdims=True)

    gl = jnp.where(lane < N_GROUPS, lg, ninf)
    gmax, g_sel = first_max(gl)
    p_g = 1.0 / jnp.sum(jnp.exp(gl - gmax), axis=-1, keepdims=True)
    lo_l = N_GROUPS + EXPERTS_PER_GROUP * g_sel
    el = jnp.where((lane >= lo_l) & (lane < lo_l + EXPERTS_PER_GROUP), lg, ninf)
    emax, i1 = first_max(el)
    max2, i2 = first_max(jnp.where(lane == i1, ninf, el))
    e2 = jnp.exp(max2 - emax)
    w1 = p_g / (1.0 + e2)
    w2 = p_g * e2 / (1.0 + e2)
    e1, e2 = i1 - N_GROUPS, i2 - N_GROUPS
    oh1, oh2 = lane == e1, lane == e2
    oh = jnp.where(oh1, 1.0, 0.0) + jnp.where(oh2, 1.0, 0.0)
    tm = lg.shape[0]
    earlier = lax.broadcasted_iota(jnp.int32, (tm, tm), 1) < lax.broadcasted_iota(jnp.int32, (tm, tm), 0)
    carry = carry_ref[0:1, :]
    before = _dot(jnp.where(earlier, 1.0, 0.0).astype(BF16), oh.astype(BF16)) + carry
    r1 = jnp.sum(jnp.where(oh1, before, 0.0), axis=-1, keepdims=True)
    r2 = jnp.sum(jnp.where(oh2, before, 0.0), axis=-1, keepdims=True)
    counts = carry + jnp.sum(oh, axis=0, keepdims=True)
    carry_ref[...] = jnp.broadcast_to(counts, carry_ref.shape)
    cnt_ref[...] = jnp.broadcast_to(counts, cnt_ref.shape).astype(jnp.int32)
    ids = jnp.where(lane == 0.0, e1, jnp.where(lane == 1.0, e2, jnp.where(lane == 2.0, r1,
                                                                         jnp.where(lane == 3.0, r2, 0.0))))
    ids_ref[...] = ids.astype(jnp.int32)
    wts_ref[...] = jnp.where(lane == 0.0, w1, jnp.where(lane == 1.0, w2, 0.0))


def _router(h2, g, wr, br, tm):
    t, d = h2.shape
    return pl.pallas_call(
        _router_kernel,
        grid=(t // tm,),
        in_specs=[
            pl.BlockSpec((tm, d), lambda i: (i, 0)),
            pl.BlockSpec((1, d), lambda i: (0, 0)),
            pl.BlockSpec((2, d, LANES), lambda i: (0, 0, 0)),
            pl.BlockSpec((1, LANES), lambda i: (0, 0)),
        ],
        out_specs=[
            pl.BlockSpec((tm * SLAB, SLAB_W), lambda i: (i, 0)),
            pl.BlockSpec((tm, LANES), lambda i: (i, 0)),
            pl.BlockSpec((tm, LANES), lambda i: (i, 0)),
            pl.BlockSpec((8, LANES), lambda i: (0, 0)),
        ],
        out_shape=[jax.ShapeDtypeStruct((t * SLAB, SLAB_W), SLAB_DTYPE),
                   jax.ShapeDtypeStruct((t, LANES), jnp.int32),
                   jax.ShapeDtypeStruct((t, LANES), F32),
                   jax.ShapeDtypeStruct((8, LANES), jnp.int32)],
        scratch_shapes=[pltpu.VMEM((8, LANES), F32)],
        compiler_params=_cparams(("arbitrary",)),
        name="router",
    )(h2, g, wr, br)


def _experts_kernel(blk_e_ref, n_used_ref, tok0_ref, tok1_ref, tok2_ref, x_hbm, wg_ref, wu_ref, wd_ref, y_ref,
                    xbuf, wg_b, wu_b, wd_b, sem, *, rows):
    j = pl.program_id(0)
    n_used = n_used_ref[0]
    used = j < n_used
    slot = j % GATHER_SLOTS
    slot1 = (j + 1) % GATHER_SLOTS
    slot2 = (j + 2) % GATHER_SLOTS

    @pl.when(j == 0)
    def _():
        def start(r, carry):
            _slab_copy(x_hbm, xbuf, sem, 0, tok0_ref[0, 0, r], r).start()
            _slab_copy(x_hbm, xbuf, sem, 1, tok1_ref[0, 0, r], r).start()
            return carry
        lax.fori_loop(0, rows, start, 0)

    @pl.when(jnp.logical_not(used))
    def _():
        @pl.when(j == n_used)
        def _():
            _slab_wait(x_hbm, xbuf, sem, slot, rows)
            _slab_wait(x_hbm, xbuf, sem, slot1, rows)
        y_ref[...] = jnp.zeros_like(y_ref)

    @pl.when(used)
    def _():
        _slab_wait(x_hbm, xbuf, sem, slot, rows)

        @pl.when((j == 0) | (blk_e_ref[j] != blk_e_ref[jnp.maximum(j - 1, 0)]))
        def _():
            wg_b[...] = wg_ref[0].astype(BF16)
            wu_b[...] = wu_ref[0].astype(BF16)
            wd_b[...] = wd_ref[0].astype(BF16)

        batch = rows // GATHER_BATCHES
        issued = [0]

        def gather_batch(after):
            zero = 0 if after is None else _zero_after(after)
            for r in range(issued[0] * batch, (issued[0] + 1) * batch):
                _slab_copy(x_hbm, xbuf, sem, slot2, tok2_ref[0, 0, r] + zero, r).start()
            issued[0] += 1

        xb = _slab_load(xbuf, slot, 0, rows).astype(BF16)
        gather_batch(None)
        de = wg_b.shape[1]
        halves = [slice(0, de // 2), slice(de // 2, de)]
        gate, up = [], []
        for cs in halves:
            gate.append(_dot(xb, wg_b[:, cs]))
            gather_batch(gate[-1])
        for cs in halves:
            up.append(_dot(xb, wu_b[:, cs]))
            gather_batch(up[-1])
        gate = jnp.concatenate(gate, axis=1)
        act = (gate * jax.nn.sigmoid(gate) * jnp.concatenate(up, axis=1)).astype(BF16)
        wo = de
        per = wo // SLAB_W
        for c in range(wd_b.shape[1] // wo):
            yc = _dot(act, wd_b[:, c * wo:(c + 1) * wo])
            for cc in range(per):
                y_ref[pl.ds(c * per + cc, rows, stride=SLAB), :] = yc[:, cc * SLAB_W:(cc + 1) * SLAB_W]
            if issued[0] < GATHER_BATCHES:
                gather_batch(yc)
        assert issued[0] == GATHER_BATCHES

        @pl.when(j == pl.num_programs(0) - 1)
        def _():
            _slab_wait(x_hbm, xbuf, sem, slot1, rows)
            _slab_wait(x_hbm, xbuf, sem, slot2, rows)


def _experts(blk_e, n_used, row_tok3, xn_slab, wg, wu, wd):
    n_blocks, _, rows = row_tok3.shape
    _, d, de = wg.shape

    def tok_spec(ahead):
        return pl.BlockSpec((1, 1, rows), lambda j, be, nu: (jnp.minimum(j + ahead, n_blocks - 1), 0, 0),
                            memory_space=pltpu.SMEM)

    grid_spec = pltpu.PrefetchScalarGridSpec(
        num_scalar_prefetch=2,
        grid=(n_blocks,),
        in_specs=[
            tok_spec(0), tok_spec(1), tok_spec(2),
            pl.BlockSpec(memory_space=pl.ANY),
            pl.BlockSpec((1, d, de), lambda j, be, nu: (be[j], 0, 0)),
            pl.BlockSpec((1, d, de), lambda j, be, nu: (be[j], 0, 0)),
            pl.BlockSpec((1, de, d), lambda j, be, nu: (be[j], 0, 0)),
        ],
        out_specs=pl.BlockSpec((rows * SLAB, SLAB_W), lambda j, be, nu: (j, 0)),
        scratch_shapes=[pltpu.VMEM((GATHER_SLOTS, rows * SLAB_PITCH, SLAB_W), SLAB_DTYPE),
                        pltpu.VMEM((d, de), BF16), pltpu.VMEM((d, de), BF16), pltpu.VMEM((de, d), BF16),
                        pltpu.SemaphoreType.DMA((GATHER_SLOTS,))],
    )
    return pl.pallas_call(
        functools.partial(_experts_kernel, rows=rows),
        grid_spec=grid_spec,
        out_shape=jax.ShapeDtypeStruct((n_blocks * rows * SLAB, SLAB_W), SLAB_DTYPE),
        compiler_params=_cparams(("arbitrary",)),
        name="experts",
    )(blk_e, n_used, row_tok3, row_tok3, row_tok3, xn_slab, wg, wu, wd)


def _combine_kernel(pos0_ref, pos1_ref, pos2_ref, y_hbm, wts_ref, h_ref, g_ref, o_ref, ybuf, sem, *, tc):
    j = pl.program_id(0)
    slot = j % GATHER_SLOTS
    slot1 = (j + 1) % GATHER_SLOTS
    slot2 = (j + 2) % GATHER_SLOTS

    @pl.when(j == 0)
    def _():
        def start(r, carry):
            for s, idx in ((0, pos0_ref), (1, pos1_ref)):
                _slab_copy(y_hbm, ybuf, sem, s, idx[0, 0, 2 * r], r).start()
                _slab_copy(y_hbm, ybuf, sem, s, idx[0, 0, 2 * r + 1], tc + r).start()
            return carry
        lax.fori_loop(0, tc, start, 0)

    _slab_wait(y_hbm, ybuf, sem, slot, 2 * tc)
    for r in range(tc):
        _slab_copy(y_hbm, ybuf, sem, slot2, pos2_ref[0, 0, 2 * r], r).start(priority=0)
        _slab_copy(y_hbm, ybuf, sem, slot2, pos2_ref[0, 0, 2 * r + 1], tc + r).start(priority=1)
    w = wts_ref[...]
    y = _slab_load(ybuf, slot, 0, tc) * w[:, 0:1] + _slab_load(ybuf, slot, tc, tc) * w[:, 1:2]
    o_ref[...] = _rms(h_ref[...] + y, g_ref[...])

    @pl.when(j == pl.num_programs(0) - 1)
    def _():
        _slab_wait(y_hbm, ybuf, sem, slot1, 2 * tc)
        _slab_wait(y_hbm, ybuf, sem, slot2, 2 * tc)


def _combine(pos3, y_slab, wts, h2, g, tc):
    t, d = h2.shape
    n = t // tc

    def pos_spec(ahead):
        return pl.BlockSpec((1, 1, 2 * tc), lambda i: (jnp.minimum(i + ahead, n - 1), 0, 0),
                            memory_space=pltpu.SMEM)

    return pl.pallas_call(
        functools.partial(_combine_kernel, tc=tc),
        grid=(n,),
        in_specs=[
            pos_spec(0), pos_spec(1), pos_spec(2),
            pl.BlockSpec(memory_space=pl.ANY),
            pl.BlockSpec((tc, LANES), lambda i: (i, 0)),
            pl.BlockSpec((tc, d), lambda i: (i, 0)),
            pl.BlockSpec((1, d), lambda i: (0, 0)),
        ],
        out_specs=pl.BlockSpec((tc, d), lambda i: (i, 0)),
        out_shape=jax.ShapeDtypeStruct((t, d), F32),
        scratch_shapes=[pltpu.VMEM((GATHER_SLOTS, 2 * tc * SLAB_PITCH, SLAB_W), SLAB_DTYPE),
                        pltpu.SemaphoreType.DMA((GATHER_SLOTS,))],
        compiler_params=_cparams(("arbitrary",)),
        name="combine",
    )(pos3, pos3, pos3, y_slab, wts, h2, g)


def _dispatch_plan(ids, counts, rows):
    flat_e = ids[:, 0:2].reshape(-1)
    rank = ids[:, 2:4].reshape(-1)
    n_asg = flat_e.shape[0]
    experts = jnp.arange(N_EXPERTS, dtype=jnp.int32)
    padded = ((counts + rows - 1) // rows) * rows
    pends = jnp.cumsum(padded)
    pstarts = pends - padded
    pos = jnp.sum(jnp.where(flat_e[:, None] == experts[None, :], pstarts[None, :], 0), axis=1) + rank
    n_rows = n_asg + N_EXPERTS * rows
    n_blocks = n_rows // rows
    row_tok = jnp.zeros((n_rows,), jnp.int32).at[pos].set(jnp.arange(n_asg, dtype=jnp.int32) // 2)
    starts = jnp.arange(n_blocks, dtype=jnp.int32) * rows
    blk_e = jnp.minimum(jnp.sum((pends[None, :] <= starts[:, None]).astype(jnp.int32), axis=1), N_EXPERTS - 1)
    n_used = (pends[-1] // rows).astype(jnp.int32).reshape(1)
    return pos.astype(jnp.int32), row_tok.reshape(n_blocks, 1, rows), blk_e.astype(jnp.int32), n_used


def _tile(n, pref):
    return pref if n % pref == 0 else n


def _layer(h, mem, p, l):
    bsz, seq, d = h.shape
    t = bsz * seq
    x2 = h.reshape(t, d)
    row = lambda v: v.reshape(1, -1)
    tm = _tile(t, 1024)

    w_in = p['w_in'][l]
    w_big = jnp.concatenate([w_in[:, 0:3072], w_in[:, 3104:5152], w_in[:, 5168:9264]], axis=1).astype(BF16)
    w_small = jnp.concatenate([w_in[:, 3072:3104], w_in[:, 5152:5168],
                               jnp.zeros((d, LANES - 48), F32)], axis=1).astype(BF16)
    big, small = _in_proj(x2, row(p['norm_mix'][l]), w_big, w_small, tm, 1024)
    big3 = big.reshape(bsz, seq, N_BIG)
    sm3 = small.reshape(bsz, seq, LANES)

    def lr_pad(w, off):
        return jnp.zeros((LANES, GLA_K), F32).at[off:off + GLA_RANK].set(w).astype(BF16)

    o_f = _gla_scan(big3, sm3, lr_pad(p['gla_w_lr_f'][l], SM_LRF), row(p['gla_b_lr_f'][l]), rev=False)
    y_a = _gla_scan(big3, sm3, lr_pad(p['gla_w_lr_b'][l], SM_LRB), row(p['gla_b_lr_b'][l]), rev=True,
                    prev=o_f, gn=row(p['gla_norm'][l]))

    cw = jnp.zeros((8, MLSTM_W), F32).at[:CONV_WIDTH].set(p['conv_w'][l].reshape(CONV_WIDTH, MLSTM_W))
    qm, km, vm = _mlstm_pre(big3, cw, row(p['conv_b'][l]), p['m_wq'][l].astype(BF16),
                            p['m_wk'][l].astype(BF16), p['m_wv'][l].astype(BF16), _tile(seq, 1024))
    gbias = jnp.zeros((1, LANES), F32).at[0, SM_GATES:SM_GATES + 4 * MLSTM_HEADS].set(
        p['m_gate_bias'][l].reshape(-1))
    h_f = _mlstm_scan(qm, km, vm, sm3, gbias, rev=False)
    y_b = _mlstm_scan(qm, km, vm, sm3, gbias, rev=True, prev=h_f, big3=big3, mn=row(p['m_norm'][l]))

    merged = _merge(y_a.reshape(t, GLA_V), y_b.reshape(t, MLSTM_W), p['w_branch_a'][l].astype(BF16),
                    p['w_branch_b'][l].astype(BF16), big, tm, 1024)
    h1 = _matmul_res(merged, p['w_mix_out'][l].astype(BF16), x2, tm, 1024)

    n_mem = mem.shape[1]
    kv = _kv_proj(mem.reshape(bsz * n_mem, d), row(p['norm_mem'][l]), p['w_xkv'][l].astype(BF16), 1024)
    h2 = _xattn(h1, row(p['norm_xattn'][l]), p['w_xq'][l].astype(BF16), kv.reshape(bsz, n_mem, 2 * d),
                p['w_xo'][l].astype(BF16), seq, _tile(seq, 512))

    wr = jnp.concatenate([p['w_group'][l], p['w_router'][l].transpose(1, 0, 2).reshape(d, N_EXPERTS),
                          jnp.zeros((d, LANES - N_GROUPS - N_EXPERTS), F32)], axis=1)
    wr_hi = wr.astype(BF16)
    wr2 = jnp.stack([wr_hi, (wr - wr_hi.astype(F32)).astype(BF16)])
    br = jnp.concatenate([p['b_group'][l], p['b_router'][l].reshape(-1),
                          jnp.zeros((LANES - N_GROUPS - N_EXPERTS,), F32)]).reshape(1, LANES)
    xn3, ids, wts, cnt = _router(h2, row(p['norm_ffn'][l]), wr2, br, tm)
    pos, row_tok3, blk_e, n_used = _dispatch_plan(ids, cnt[0, :N_EXPERTS], MOE_ROWS)
    y_rows = _experts(blk_e, n_used, row_tok3, xn3, p['w_gate'][l], p['w_up'][l], p['w_down'][l])
    tc = _tile(t, COMBINE_TOK)
    return pos.reshape(t // tc, 1, 2 * tc), y_rows, wts, h2, tc


def kernel(x, mem, norm_mix, w_in, gla_w_lr_f, gla_b_lr_f, gla_w_lr_b, gla_b_lr_b, gla_norm, conv_w, conv_b, m_wq, m_wk, m_wv, m_gate_bias, m_norm, w_branch_a, w_branch_b, w_mix_out, norm_xattn, norm_mem, w_xq, w_xkv, w_xo, norm_ffn, w_group, b_group, w_router, b_router, w_gate, w_up, w_down, norm_final):
    p = dict(norm_mix=norm_mix, w_in=w_in, gla_w_lr_f=gla_w_lr_f, gla_b_lr_f=gla_b_lr_f, gla_w_lr_b=gla_w_lr_b,
             gla_b_lr_b=gla_b_lr_b, gla_norm=gla_norm, conv_w=conv_w, conv_b=conv_b, m_wq=m_wq, m_wk=m_wk,
             m_wv=m_wv, m_gate_bias=m_gate_bias, m_norm=m_norm, w_branch_a=w_branch_a, w_branch_b=w_branch_b,
             w_mix_out=w_mix_out, norm_xattn=norm_xattn, norm_mem=norm_mem, w_xq=w_xq, w_xkv=w_xkv, w_xo=w_xo,
             norm_ffn=norm_ffn, w_group=w_group, b_group=b_group, w_router=w_router, b_router=b_router,
             w_gate=w_gate, w_up=w_up, w_down=w_down)
    bsz, seq, d = x.shape
    depth = norm_mix.shape[0]
    assert depth == 1, "the final norm is fused into the last layer's combine step"
    pos3, y_rows, wts, h2, tc = _layer(x, mem, p, 0)
    out = _combine(pos3, y_rows, wts, h2, norm_final.reshape(1, d), tc)
    return out.reshape(bsz, seq, d)
```

```python
import functools

import jax
import jax.numpy as jnp
from jax import lax
from jax.experimental import pallas as pl
from jax.experimental.pallas import tpu as pltpu

F32 = jnp.float32
BF16 = jnp.bfloat16

EPS = 1e-6
LOG2_E = 1.4426950408889634
D_MODEL = 2048

GLA_HEADS = 4
GLA_DK = 128
GLA_DV = 256
GLA_K = GLA_HEADS * GLA_DK
GLA_V = GLA_HEADS * GLA_DV
GLA_RANK = 16
GLA_TAU = 16.0
GLA_CHUNK = 64
GLA_SUB = 16
GLA_SEQS = 2
GLA_SAFE_LOG2 = 60.0

MLSTM_HEADS = 4
MLSTM_DH = 256
MLSTM_W = MLSTM_HEADS * MLSTM_DH
CONV_WIDTH = 5
MLSTM_CHUNK = 256
CONV_HALO = 16

XATTN_HEADS = 4
XATTN_DH = D_MODEL // XATTN_HEADS

N_GROUPS = 4
EXPERTS_PER_GROUP = 8
N_EXPERTS = N_GROUPS * EXPERTS_PER_GROUP
D_EXPERT = 512
MOE_ROWS = 256
COMBINE_TOK = 256

LANES = 128
SLAB_W = LANES
SLAB = D_MODEL // SLAB_W
SLAB_PITCH = SLAB + 8
SLAB_DTYPE = F32
GATHER_SLOTS = 3
GATHER_BATCHES = 8

OFF_Q, OFF_K, OFF_V, OFF_GG = 0, 512, 1024, 2048
OFF_MX, OFF_MZ, OFF_GA, OFF_GB = 3072, 4096, 5120, 7168
N_BIG = 9216
SM_LRF, SM_LRB, SM_GATES = 0, 16, 32

VMEM_LIMIT = 56 * 1024 * 1024


def _cparams(sem):
    return pltpu.CompilerParams(dimension_semantics=sem, vmem_limit_bytes=VMEM_LIMIT)


def _rms(x, g):
    return x * lax.rsqrt(jnp.mean(x * x, axis=-1, keepdims=True) + EPS) * g


def _log_sigmoid(x):
    return jnp.minimum(x, 0.0) - jnp.log(1.0 + jnp.exp(-jnp.abs(x)))


def _dot(a, b):
    return jnp.dot(a, b, preferred_element_type=F32)


def _dot_nt(a, b):
    return lax.dot_general(a, b, (((1,), (1,)), ((), ())), preferred_element_type=F32)


def _dot_tn(a, b):
    return lax.dot_general(a, b, (((0,), (0,)), ((), ())), preferred_element_type=F32)


def _order_mask(c, rev):
    t = lax.broadcasted_iota(jnp.int32, (c, c), 0)
    s = lax.broadcasted_iota(jnp.int32, (c, c), 1)
    return (s >= t) if rev else (s <= t)


def _split3(x):
    hi = x.astype(BF16)
    r1 = x - hi.astype(F32)
    mid = r1.astype(BF16)
    return hi, mid, (r1 - mid.astype(F32)).astype(BF16)


def _cumsum_mm(mask_bf16, x):
    hi, mid, lo = _split3(x)
    return _dot(mask_bf16, hi) + _dot(mask_bf16, mid) + _dot(mask_bf16, lo)


def _in_proj_kernel(x_ref, g_ref, wbig_ref, wsm_ref, big_ref, sm_ref, xn_ref):
    @pl.when(pl.program_id(1) == 0)
    def _():
        xn = _rms(x_ref[...], g_ref[...]).astype(BF16)
        xn_ref[...] = xn
        sm_ref[...] = _dot(xn, wsm_ref[...])
    big_ref[...] = _dot(xn_ref[...], wbig_ref[...]).astype(BF16)


def _in_proj(x2, g, w_big, w_small, tm, tn):
    t, d = x2.shape
    n = w_big.shape[1]
    return pl.pallas_call(
        _in_proj_kernel,
        grid=(t // tm, n // tn),
        in_specs=[
            pl.BlockSpec((tm, d), lambda i, j: (i, 0)),
            pl.BlockSpec((1, d), lambda i, j: (0, 0)),
            pl.BlockSpec((d, tn), lambda i, j: (0, j)),
            pl.BlockSpec((d, LANES), lambda i, j: (0, 0)),
        ],
        out_specs=[
            pl.BlockSpec((tm, tn), lambda i, j: (i, j)),
            pl.BlockSpec((tm, LANES), lambda i, j: (i, 0)),
        ],
        out_shape=[jax.ShapeDtypeStruct((t, n), BF16), jax.ShapeDtypeStruct((t, LANES), F32)],
        scratch_shapes=[pltpu.VMEM((tm, d), BF16)],
        compiler_params=_cparams(("parallel", "arbitrary")),
        name="in_proj",
    )(x2, g, w_big, w_small)


def _gla_scores_any_decay(bh, qh, kh, kh_b, rev, c):
    sb = GLA_SUB
    n_sub = c // sb
    col = lax.broadcasted_iota(jnp.int32, (sb, c), 1)
    trow = lax.broadcasted_iota(jnp.int32, (sb, c), 0)
    slabs = [(qh[i * sb:(i + 1) * sb]
              * jnp.exp2(bh[i * sb:(i + 1) * sb] - bh[i * sb + s:i * sb + s + 1, :])).astype(BF16)
             for i in range(n_sub) for s in range(sb)]
    g = _dot_nt(jnp.concatenate(slabs, axis=0), kh_b)
    rows = []
    for i in range(n_sub):
        r0 = i * sb
        ref_row = r0 + (sb - 1 if rev else 0)
        beta = bh[ref_row:ref_row + 1, :]
        qt = (qh[r0:r0 + sb] * jnp.exp2(bh[r0:r0 + sb] - beta)).astype(BF16)
        kt = (kh * jnp.exp2(beta - bh)).astype(BF16)
        a_off = _dot_nt(qt, kt)
        a_diag = jnp.zeros((sb, c), F32)
        for s in range(sb):
            g0 = (i * sb + s) * sb
            a_diag = jnp.where(col == r0 + s, g[g0:g0 + sb], a_diag)
        tr = trow + r0
        if rev:
            off_mask = col >= r0 + sb
            diag_mask = (col >= tr) & (col < r0 + sb)
        else:
            off_mask = col < r0
            diag_mask = (col <= tr) & (col >= r0)
        rows.append(jnp.where(off_mask, a_off, 0.0) + jnp.where(diag_mask, a_diag, 0.0))
    return jnp.concatenate(rows, axis=0).astype(BF16)


def _gla_kernel(q_ref, k_ref, v_ref, sm_ref, wlr_ref, blr_ref, *rest, rev, final, c, nb):
    if final:
        oprev_ref, gg_ref, gn_ref, o_ref, st_ref, a_ref = rest
    else:
        o_ref, st_ref, a_ref = rest

    @pl.when(pl.program_id(1) == 0)
    def _():
        st_ref[...] = jnp.zeros_like(st_ref)

    mask = _order_mask(c, rev)
    mask_b = jnp.where(mask, 1.0, 0.0).astype(BF16)
    last = 0 if rev else c - 1

    for bi in range(nb):
        x = _dot(sm_ref[bi].astype(BF16), wlr_ref[...]) + blr_ref[...]
        la = _log_sigmoid(x) * (1.0 / GLA_TAU)
        b = _cumsum_mm(mask_b, la) * LOG2_E
        tot = b[last:last + 1, :]
        q = q_ref[bi].astype(F32) * (GLA_DK ** -0.5)
        k = k_ref[bi].astype(F32)
        q_in = (q * jnp.exp2(b)).astype(BF16)
        k_dec = (k * jnp.exp2(tot - b)).astype(BF16)
        e_tot = jnp.exp2(tot)

        mild = jnp.min(tot) > -GLA_SAFE_LOG2

        @pl.when(mild)
        def _():
            q_up = (q * jnp.exp2(b - tot)).astype(BF16)
            for h in range(GLA_HEADS):
                ks = slice(h * GLA_DK, (h + 1) * GLA_DK)
                a_ref[bi, h] = jnp.where(mask, _dot_nt(q_up[:, ks], k_dec[:, ks]), 0.0).astype(BF16)

        @pl.when(jnp.logical_not(mild))
        def _():
            for h in range(GLA_HEADS):
                ks = slice(h * GLA_DK, (h + 1) * GLA_DK)
                a_ref[bi, h] = _gla_scores_any_decay(b[:, ks], q[:, ks], k[:, ks], k_ref[bi, :, ks], rev, c)

        for h in range(GLA_HEADS):
            ks = slice(h * GLA_DK, (h + 1) * GLA_DK)
            vs = slice(h * GLA_DV, (h + 1) * GLA_DV)
            vh = v_ref[bi, :, vs]
            st = st_ref[bi, h]
            o_inter = _dot_nt(q_in[:, ks], st.astype(BF16))
            st_ref[bi, h] = st * e_tot[:, ks] + _dot_tn(vh, k_dec[:, ks])
            o = o_inter + _dot(a_ref[bi, h], vh)
            if final:
                o = o + oprev_ref[bi, :, vs]
                y = _rms(o, gn_ref[:, vs])
                gg = gg_ref[bi, :, vs].astype(F32)
                o_ref[bi, :, vs] = (y * (gg * jax.nn.sigmoid(gg))).astype(o_ref.dtype)
            else:
                o_ref[bi, :, vs] = o.astype(o_ref.dtype)


def _gla_scan(big3, sm3, wlr, blr, rev, prev=None, gn=None):
    bsz, seq, _ = big3.shape
    c = GLA_CHUNK
    n = seq // c
    nb = GLA_SEQS if bsz % GLA_SEQS == 0 else 1
    final = prev is not None

    def cm(ci):
        return (n - 1 - ci) if rev else ci

    in_specs = [
        pl.BlockSpec((nb, c, GLA_K), lambda b, ci: (b, cm(ci), OFF_Q // GLA_K)),
        pl.BlockSpec((nb, c, GLA_K), lambda b, ci: (b, cm(ci), OFF_K // GLA_K)),
        pl.BlockSpec((nb, c, GLA_V), lambda b, ci: (b, cm(ci), OFF_V // GLA_V)),
        pl.BlockSpec((nb, c, LANES), lambda b, ci: (b, cm(ci), 0)),
        pl.BlockSpec((LANES, GLA_K), lambda b, ci: (0, 0)),
        pl.BlockSpec((1, GLA_K), lambda b, ci: (0, 0)),
    ]
    args = [big3, big3, big3, sm3, wlr, blr]
    if final:
        in_specs += [
            pl.BlockSpec((nb, c, GLA_V), lambda b, ci: (b, cm(ci), 0)),
            pl.BlockSpec((nb, c, GLA_V), lambda b, ci: (b, cm(ci), OFF_GG // GLA_V)),
            pl.BlockSpec((1, GLA_V), lambda b, ci: (0, 0)),
        ]
        args += [prev, big3, gn]
    return pl.pallas_call(
        functools.partial(_gla_kernel, rev=rev, final=final, c=c, nb=nb),
        grid=(bsz // nb, n),
        in_specs=in_specs,
        out_specs=pl.BlockSpec((nb, c, GLA_V), lambda b, ci: (b, cm(ci), 0)),
        out_shape=jax.ShapeDtypeStruct((bsz, seq, GLA_V), BF16 if final else F32),
        scratch_shapes=[pltpu.VMEM((nb, GLA_HEADS, GLA_DV, GLA_DK), F32),
                        pltpu.VMEM((nb, GLA_HEADS, c, c), BF16)],
        compiler_params=_cparams(("parallel", "arbitrary")),
        name="gla_bwd" if rev else "gla_fwd",
    )(*args)


def _mlstm_pre_kernel(cur_ref, prev_ref, next_ref, cw_ref, cb_ref, wq_ref, wk_ref, wv_ref,
                      q_ref, k_ref, v_ref, *, tm):
    i = pl.program_id(2)
    cur_b = cur_ref[0]
    cur = cur_b.astype(F32)
    halo = CONV_WIDTH // 2
    prev = jnp.where(i > 0, prev_ref[0].astype(F32), 0.0)
    nxt = jnp.where(i < pl.num_programs(2) - 1, next_ref[0].astype(F32), 0.0)
    ext = jnp.concatenate([prev[CONV_HALO - 8:], cur, nxt[:8]], axis=0)
    acc = jnp.zeros_like(cur) + cb_ref[...]
    for w in range(CONV_WIDTH):
        off = 8 - halo + w
        acc = acc + ext[off:off + tm] * cw_ref[w:w + 1, :]
    xc = (acc * jax.nn.sigmoid(acc)).astype(BF16)
    q_ref[0] = _dot(xc, wq_ref[0]).astype(BF16)
    k_ref[0] = (_dot(xc, wk_ref[0]) * (MLSTM_DH ** -0.5)).astype(BF16)
    v_ref[0] = _dot(cur_b, wv_ref[0]).astype(BF16)


def _mlstm_pre(big3, cw, cb, wq, wk, wv, tm):
    bsz, seq, _ = big3.shape
    dh = MLSTM_DH
    nh = tm // CONV_HALO
    n_halo = seq // CONV_HALO
    c0 = OFF_MX // dh
    out = jax.ShapeDtypeStruct((bsz, seq, MLSTM_W), BF16)
    ospec = pl.BlockSpec((1, tm, dh), lambda b, h, i: (b, i, h))
    wspec = pl.BlockSpec((1, dh, dh), lambda b, h, i: (h, 0, 0))
    return pl.pallas_call(
        functools.partial(_mlstm_pre_kernel, tm=tm),
        grid=(bsz, MLSTM_HEADS, seq // tm),
        in_specs=[
            pl.BlockSpec((1, tm, dh), lambda b, h, i: (b, i, c0 + h)),
            pl.BlockSpec((1, CONV_HALO, dh), lambda b, h, i: (b, jnp.maximum(i * nh - 1, 0), c0 + h)),
            pl.BlockSpec((1, CONV_HALO, dh), lambda b, h, i: (b, jnp.minimum((i + 1) * nh, n_halo - 1), c0 + h)),
            pl.BlockSpec((8, dh), lambda b, h, i: (0, h)),
            pl.BlockSpec((1, dh), lambda b, h, i: (0, h)),
            wspec, wspec, wspec,
        ],
        out_specs=[ospec, ospec, ospec],
        out_shape=[out, out, out],
        compiler_params=_cparams(("parallel", "parallel", "parallel")),
        name="mlstm_pre",
    )(big3, big3, big3, cw, cb, wq, wk, wv)


def _mlstm_kernel(q_ref, k_ref, v_ref, sm_ref, gb_ref, *rest, rev, final, c):
    if final:
        hprev_ref, mz_ref, mn_ref, o_ref, c_ref, n_ref, m_ref = rest
    else:
        o_ref, c_ref, n_ref, m_ref = rest

    @pl.when(pl.program_id(1) == 0)
    def _():
        c_ref[...] = jnp.zeros_like(c_ref)
        n_ref[...] = jnp.zeros_like(n_ref)
        m_ref[...] = jnp.zeros_like(m_ref)

    g = sm_ref[0] + gb_ref[...]
    mask = _order_mask(c, rev)
    bcum = _cumsum_mm(jnp.where(mask, 1.0, 0.0).astype(BF16), _log_sigmoid(g))
    g_t = g.T
    b_t = bcum.T
    last = 0 if rev else c - 1
    dh = MLSTM_DH

    for h in range(MLSTM_HEADS):
        ci = SM_GATES + (2 * MLSTM_HEADS if rev else 0) + h
        cf = ci + MLSTM_HEADS
        hs = slice(h * dh, (h + 1) * dh)
        i_col, b_col = g[:, ci:ci + 1], bcum[:, cf:cf + 1]
        i_row, b_row = g_t[ci:ci + 1, :], b_t[cf:cf + 1, :]
        tot = b_col[last:last + 1, :]
        m_prev = m_ref[h, 0:1, 0:1]
        qh, kh, vh = q_ref[0, :, hs], k_ref[0, :, hs], v_ref[0, :, hs]
        c_st = c_ref[h]
        n_st = n_ref[h, 0:1, :]

        dmat = jnp.where(mask, b_col - b_row + i_row, -jnp.inf)
        inter_log = b_col + m_prev
        m_t = jnp.maximum(inter_log, jnp.max(dmat, axis=-1, keepdims=True))
        w_intra = jnp.exp(dmat - m_t)
        w_inter = jnp.exp(inter_log - m_t)
        s = _dot_nt(qh, kh) * w_intra
        num = w_inter * _dot(qh, c_st.astype(BF16)) + _dot(s.astype(BF16), vh)
        den = (w_inter * jnp.sum(qh.astype(F32) * n_st, axis=-1, keepdims=True)
               + jnp.sum(s, axis=-1, keepdims=True))
        hh = num / jnp.maximum(jnp.abs(den), jnp.exp(-m_t))

        upd_col = tot - b_col + i_col
        upd_row = tot - b_row + i_row
        m_new = jnp.maximum(tot + m_prev, jnp.max(upd_row, axis=-1, keepdims=True))
        w_old = jnp.exp(tot + m_prev - m_new)
        kw = kh.astype(F32) * jnp.exp(upd_col - m_new)
        c_ref[h] = w_old * c_st + _dot_tn(kw.astype(BF16), vh)
        n_ref[h] = jnp.broadcast_to(w_old * n_st + jnp.sum(kw, axis=0, keepdims=True), (8, dh))
        m_ref[h] = jnp.broadcast_to(m_new, (8, LANES))

        if final:
            hh = hh + hprev_ref[0, :, hs]
            y = _rms(hh, mn_ref[:, hs])
            o_ref[0, :, hs] = (y * jax.nn.sigmoid(mz_ref[0, :, hs].astype(F32))).astype(o_ref.dtype)
        else:
            o_ref[0, :, hs] = hh.astype(o_ref.dtype)


def _mlstm_scan(qm, km, vm, sm3, gbias, rev, prev=None, big3=None, mn=None):
    bsz, seq, w = qm.shape
    c = min(MLSTM_CHUNK, seq)
    n = seq // c
    final = prev is not None

    def cm(ci):
        return (n - 1 - ci) if rev else ci

    xspec = pl.BlockSpec((1, c, w), lambda b, ci: (b, cm(ci), 0))
    in_specs = [xspec, xspec, xspec,
                pl.BlockSpec((1, c, LANES), lambda b, ci: (b, cm(ci), 0)),
                pl.BlockSpec((1, LANES), lambda b, ci: (0, 0))]
    args = [qm, km, vm, sm3, gbias]
    if final:
        in_specs += [xspec,
                     pl.BlockSpec((1, c, w), lambda b, ci: (b, cm(ci), OFF_MZ // MLSTM_W)),
                     pl.BlockSpec((1, w), lambda b, ci: (0, 0))]
        args += [prev, big3, mn]
    return pl.pallas_call(
        functools.partial(_mlstm_kernel, rev=rev, final=final, c=c),
        grid=(bsz, n),
        in_specs=in_specs,
        out_specs=xspec,
        out_shape=jax.ShapeDtypeStruct((bsz, seq, w), BF16 if final else F32),
        scratch_shapes=[pltpu.VMEM((MLSTM_HEADS, MLSTM_DH, MLSTM_DH), F32),
                        pltpu.VMEM((MLSTM_HEADS, 8, MLSTM_DH), F32),
                        pltpu.VMEM((MLSTM_HEADS, 8, LANES), F32)],
        compiler_params=_cparams(("parallel", "arbitrary")),
        name="mlstm_bwd" if rev else "mlstm_fwd",
    )(*args)


def _merge_kernel(ya_ref, yb_ref, wa_ref, wb_ref, ga_ref, gb_ref, o_ref):
    a = _dot(ya_ref[...], wa_ref[...])
    b = _dot(yb_ref[...], wb_ref[...])
    ga = jax.nn.sigmoid(ga_ref[...].astype(F32))
    gb = jax.nn.sigmoid(gb_ref[...].astype(F32))
    o_ref[...] = (ga * a + gb * b).astype(o_ref.dtype)


def _merge(ya, yb, wa, wb, big, tm, tn):
    t, kdim = ya.shape
    n = wa.shape[1]
    return pl.pallas_call(
        _merge_kernel,
        grid=(t // tm, n // tn),
        in_specs=[
            pl.BlockSpec((tm, kdim), lambda i, j: (i, 0)),
            pl.BlockSpec((tm, kdim), lambda i, j: (i, 0)),
            pl.BlockSpec((kdim, tn), lambda i, j: (0, j)),
            pl.BlockSpec((kdim, tn), lambda i, j: (0, j)),
            pl.BlockSpec((tm, tn), lambda i, j: (i, OFF_GA // tn + j)),
            pl.BlockSpec((tm, tn), lambda i, j: (i, OFF_GB // tn + j)),
        ],
        out_specs=pl.BlockSpec((tm, tn), lambda i, j: (i, j)),
        out_shape=jax.ShapeDtypeStruct((t, n), BF16),
        compiler_params=_cparams(("parallel", "parallel")),
        name="merge",
    )(ya, yb, wa, wb, big, big)


def _matmul_res_kernel(a_ref, w_ref, r_ref, o_ref):
    o_ref[...] = r_ref[...] + _dot(a_ref[...], w_ref[...])


def _matmul_res(a, w, res, tm, tn):
    t, kdim = a.shape
    n = w.shape[1]
    return pl.pallas_call(
        _matmul_res_kernel,
        grid=(t // tm, n // tn),
        in_specs=[
            pl.BlockSpec((tm, kdim), lambda i, j: (i, 0)),
            pl.BlockSpec((kdim, tn), lambda i, j: (0, j)),
            pl.BlockSpec((tm, tn), lambda i, j: (i, j)),
        ],
        out_specs=pl.BlockSpec((tm, tn), lambda i, j: (i, j)),
        out_shape=jax.ShapeDtypeStruct((t, n), F32),
        compiler_params=_cparams(("parallel", "parallel")),
        name="mix_out",
    )(a, w, res)


def _kv_kernel(mem_ref, g_ref, w_ref, o_ref):
    mn = _rms(mem_ref[...], g_ref[...]).astype(BF16)
    o_ref[...] = _dot(mn, w_ref[...]).astype(BF16)


def _kv_proj(mem2, g, w, tn):
    r, d = mem2.shape
    n = w.shape[1]
    return pl.pallas_call(
        _kv_kernel,
        grid=(n // tn,),
        in_specs=[
            pl.BlockSpec((r, d), lambda j: (0, 0)),
            pl.BlockSpec((1, d), lambda j: (0, 0)),
            pl.BlockSpec((d, tn), lambda j: (0, j)),
        ],
        out_specs=pl.BlockSpec((r, tn), lambda j: (0, j)),
        out_shape=jax.ShapeDtypeStruct((r, n), BF16),
        compiler_params=_cparams(("parallel",)),
        name="kv_proj",
    )(mem2, g, w)


def _xattn_kernel(h_ref, g_ref, wq_ref, k_ref, v_ref, wo_ref, o_ref, xn_ref):
    @pl.when(pl.program_id(1) == 0)
    def _():
        x = h_ref[...]
        xn_ref[...] = _rms(x, g_ref[...]).astype(BF16)
        o_ref[...] = x
    q = _dot(xn_ref[...], wq_ref[...]).astype(BF16)
    s = _dot_nt(q, k_ref[0]) * (XATTN_DH ** -0.5)
    p = jnp.exp(s - jnp.max(s, axis=-1, keepdims=True))
    p = p / jnp.sum(p, axis=-1, keepdims=True)
    o = _dot(p.astype(BF16), v_ref[0]).astype(BF16)
    o_ref[...] += _dot(o, wo_ref[...])


def _xattn(h1, g, wq, kv3, wo, seq, tm):
    t, d = h1.shape
    n_mem = kv3.shape[1]
    dh = XATTN_DH
    per_b = seq // tm
    return pl.pallas_call(
        _xattn_kernel,
        grid=(t // tm, XATTN_HEADS),
        in_specs=[
            pl.BlockSpec((tm, d), lambda i, h: (i, 0)),
            pl.BlockSpec((1, d), lambda i, h: (0, 0)),
            pl.BlockSpec((d, dh), lambda i, h: (0, h)),
            pl.BlockSpec((1, n_mem, dh), lambda i, h: (i // per_b, 0, h)),
            pl.BlockSpec((1, n_mem, dh), lambda i, h: (i // per_b, 0, XATTN_HEADS + h)),
            pl.BlockSpec((dh, d), lambda i, h: (h, 0)),
        ],
        out_specs=pl.BlockSpec((tm, d), lambda i, h: (i, 0)),
        out_shape=jax.ShapeDtypeStruct((t, d), F32),
        scratch_shapes=[pltpu.VMEM((tm, d), BF16)],
        compiler_params=_cparams(("parallel", "arbitrary")),
        name="xattn",
    )(h1, g, wq, kv3, kv3, wo)


def _slab_store(ref, val):
    n = val.shape[0]
    for c in range(SLAB):
        ref[pl.ds(c, n, stride=SLAB), :] = val[:, c * SLAB_W:(c + 1) * SLAB_W]


def _slab_load(buf, slot, first, n):
    return jnp.concatenate(
        [buf[slot, pl.ds(first * SLAB_PITCH + c, n, stride=SLAB_PITCH), :] for c in range(SLAB)], axis=1)


def _slab_copy(src_hbm, dst_buf, sem, slot, src_slab, dst_slab):
    return pltpu.make_async_copy(src_hbm.at[pl.ds(src_slab * SLAB, SLAB)],
                                 dst_buf.at[slot, pl.ds(dst_slab * SLAB_PITCH, SLAB)], sem.at[slot])


def _zero_after(v):
    bits = lax.bitcast_convert_type(v[-1:, -1:], jnp.uint32)
    return ((bits >> 16) >> 16).astype(jnp.int32)[0, 0]


def _slab_wait(src_hbm, dst_buf, sem, slot, n):
    pltpu.make_async_copy(src_hbm.at[pl.ds(0, n * SLAB)], dst_buf.at[slot, pl.ds(0, n * SLAB)], sem.at[slot]).wait()


def _router_kernel(h_ref, g_ref, wr_ref, br_ref, xn_ref, ids_ref, wts_ref, cnt_ref, carry_ref):
    @pl.when(pl.program_id(0) == 0)
    def _():
        carry_ref[...] = jnp.zeros_like(carry_ref)

    xn = _rms(h_ref[...], g_ref[...])
    _slab_store(xn_ref, xn)
    hi = xn.astype(BF16)
    lo = (xn - hi.astype(F32)).astype(BF16)
    lg = _dot(hi, wr_ref[0]) + _dot(hi, wr_ref[1]) + _dot(lo, wr_ref[0]) + br_ref[...]
    lane = lax.broadcasted_iota(jnp.int32, lg.shape, 1).astype(F32)
    ninf = -jnp.inf
    big_lane = float(LANES)

    def first_max(v):
        mx = jnp.max(v, axis=-1, keepdims=True)
        return mx, jnp.min(jnp.where(v == mx, lane, big_lane), axis=-1, keepdims=True)

    gl = jnp.where(lane < N_GROUPS, lg, ninf)
    gmax, g_sel = first_max(gl)
    p_g = 1.0 / jnp.sum(jnp.exp(gl - gmax), axis=-1, keepdims=True)
    lo_l = N_GROUPS + EXPERTS_PER_GROUP * g_sel
    el = jnp.where((lane >= lo_l) & (lane < lo_l + EXPERTS_PER_GROUP), lg, ninf)
    emax, i1 = first_max(el)
    max2, i2 = first_max(jnp.where(lane == i1, ninf, el))
    e2 = jnp.exp(max2 - emax)
    w1 = p_g / (1.0 + e2)
    w2 = p_g * e2 / (1.0 + e2)
    e1, e2 = i1 - N_GROUPS, i2 - N_GROUPS
    oh1, oh2 = lane == e1, lane == e2
    oh = jnp.where(oh1, 1.0, 0.0) + jnp.where(oh2, 1.0, 0.0)
    tm = lg.shape[0]
    earlier = lax.broadcasted_iota(jnp.int32, (tm, tm), 1) < lax.broadcasted_iota(jnp.int32, (tm, tm), 0)
    carry = carry_ref[0:1, :]
    before = _dot(jnp.where(earlier, 1.0, 0.0).astype(BF16), oh.astype(BF16)) + carry
    r1 = jnp.sum(jnp.where(oh1, before, 0.0), axis=-1, keepdims=True)
    r2 = jnp.sum(jnp.where(oh2, before, 0.0), axis=-1, keepdims=True)
    counts = carry + jnp.sum(oh, axis=0, keepdims=True)
    carry_ref[...] = jnp.broadcast_to(counts, carry_ref.shape)
    cnt_ref[...] = jnp.broadcast_to(counts, cnt_ref.shape).astype(jnp.int32)
    ids = jnp.where(lane == 0.0, e1, jnp.where(lane == 1.0, e2, jnp.where(lane == 2.0, r1,
                                                                         jnp.where(lane == 3.0, r2, 0.0))))
    ids_ref[...] = ids.astype(jnp.int32)
    wts_ref[...] = jnp.where(lane == 0.0, w1, jnp.where(lane == 1.0, w2, 0.0))


def _router(h2, g, wr, br, tm):
    t, d = h2.shape
    return pl.pallas_call(
        _router_kernel,
        grid=(t // tm,),
        in_specs=[
            pl.BlockSpec((tm, d), lambda i: (i, 0)),
            pl.BlockSpec((1, d), lambda i: (0, 0)),
            pl.BlockSpec((2, d, LANES), lambda i: (0, 0, 0)),
            pl.BlockSpec((1, LANES), lambda i: (0, 0)),
        ],
        out_specs=[
            pl.BlockSpec((tm * SLAB, SLAB_W), lambda i: (i, 0)),
            pl.BlockSpec((tm, LANES), lambda i: (i, 0)),
            pl.BlockSpec((tm, LANES), lambda i: (i, 0)),
            pl.BlockSpec((8, LANES), lambda i: (0, 0)),
        ],
        out_shape=[jax.ShapeDtypeStruct((t * SLAB, SLAB_W), SLAB_DTYPE),
                   jax.ShapeDtypeStruct((t, LANES), jnp.int32),
                   jax.ShapeDtypeStruct((t, LANES), F32),
                   jax.ShapeDtypeStruct((8, LANES), jnp.int32)],
        scratch_shapes=[pltpu.VMEM((8, LANES), F32)],
        compiler_params=_cparams(("arbitrary",)),
        name="router",
    )(h2, g, wr, br)


def _experts_kernel(blk_e_ref, n_used_ref, tok0_ref, tok1_ref, tok2_ref, x_hbm, wg_ref, wu_ref, wd_ref, y_ref,
                    xbuf, wg_b, wu_b, wd_b, sem, *, rows):
    j = pl.program_id(0)
    n_used = n_used_ref[0]
    used = j < n_used
    slot = j % GATHER_SLOTS
    slot1 = (j + 1) % GATHER_SLOTS
    slot2 = (j + 2) % GATHER_SLOTS

    @pl.when(j == 0)
    def _():
        def start(r, carry):
            _slab_copy(x_hbm, xbuf, sem, 0, tok0_ref[0, 0, r], r).start()
            _slab_copy(x_hbm, xbuf, sem, 1, tok1_ref[0, 0, r], r).start()
            return carry
        lax.fori_loop(0, rows, start, 0)

    @pl.when(jnp.logical_not(used))
    def _():
        @pl.when(j == n_used)
        def _():
            _slab_wait(x_hbm, xbuf, sem, slot, rows)
            _slab_wait(x_hbm, xbuf, sem, slot1, rows)
        y_ref[...] = jnp.zeros_like(y_ref)

    @pl.when(used)
    def _():
        _slab_wait(x_hbm, xbuf, sem, slot, rows)

        @pl.when((j == 0) | (blk_e_ref[j] != blk_e_ref[jnp.maximum(j - 1, 0)]))
        def _():
            wg_b[...] = wg_ref[0].astype(BF16)
            wu_b[...] = wu_ref[0].astype(BF16)
            wd_b[...] = wd_ref[0].astype(BF16)

        batch = rows // GATHER_BATCHES
        issued = [0]

        def gather_batch(after):
            zero = 0 if after is None else _zero_after(after)
            for r in range(issued[0] * batch, (issued[0] + 1) * batch):
                _slab_copy(x_hbm, xbuf, sem, slot2, tok2_ref[0, 0, r] + zero, r).start()
            issued[0] += 1

        xb = _slab_load(xbuf, slot, 0, rows).astype(BF16)
        gather_batch(None)
        de = wg_b.shape[1]
        halves = [slice(0, de // 2), slice(de // 2, de)]
        gate, up = [], []
        for cs in halves:
            gate.append(_dot(xb, wg_b[:, cs]))
            gather_batch(gate[-1])
        for cs in halves:
            up.append(_dot(xb, wu_b[:, cs]))
            gather_batch(up[-1])
        gate = jnp.concatenate(gate, axis=1)
        act = (gate * jax.nn.sigmoid(gate) * jnp.concatenate(up, axis=1)).astype(BF16)
        wo = de
        per = wo // SLAB_W
        for c in range(wd_b.shape[1] // wo):
            yc = _dot(act, wd_b[:, c * wo:(c + 1) * wo])
            for cc in range(per):
                y_ref[pl.ds(c * per + cc, rows, stride=SLAB), :] = yc[:, cc * SLAB_W:(cc + 1) * SLAB_W]
            if issued[0] < GATHER_BATCHES:
                gather_batch(yc)
        assert issued[0] == GATHER_BATCHES

        @pl.when(j == pl.num_programs(0) - 1)
        def _():
            _slab_wait(x_hbm, xbuf, sem, slot1, rows)
            _slab_wait(x_hbm, xbuf, sem, slot2, rows)


def _experts(blk_e, n_used, row_tok3, xn_slab, wg, wu, wd):
    n_blocks, _, rows = row_tok3.shape
    _, d, de = wg.shape

    def tok_spec(ahead):
        return pl.BlockSpec((1, 1, rows), lambda j, be, nu: (jnp.minimum(j + ahead, n_blocks - 1), 0, 0),
                            memory_space=pltpu.SMEM)

    grid_spec = pltpu.PrefetchScalarGridSpec(
        num_scalar_prefetch=2,
        grid=(n_blocks,),
        in_specs=[
            tok_spec(0), tok_spec(1), tok_spec(2),
            pl.BlockSpec(memory_space=pl.ANY),
            pl.BlockSpec((1, d, de), lambda j, be, nu: (be[j], 0, 0)),
            pl.BlockSpec((1, d, de), lambda j, be, nu: (be[j], 0, 0)),
            pl.BlockSpec((1, de, d), lambda j, be, nu: (be[j], 0, 0)),
        ],
        out_specs=pl.BlockSpec((rows * SLAB, SLAB_W), lambda j, be, nu: (j, 0)),
        scratch_shapes=[pltpu.VMEM((GATHER_SLOTS, rows * SLAB_PITCH, SLAB_W), SLAB_DTYPE),
                        pltpu.VMEM((d, de), BF16), pltpu.VMEM((d, de), BF16), pltpu.VMEM((de, d), BF16),
                        pltpu.SemaphoreType.DMA((GATHER_SLOTS,))],
    )
    return pl.pallas_call(
        functools.partial(_experts_kernel, rows=rows),
        grid_spec=grid_spec,
        out_shape=jax.ShapeDtypeStruct((n_blocks * rows * SLAB, SLAB_W), SLAB_DTYPE),
        compiler_params=_cparams(("arbitrary",)),
        name="experts",
    )(blk_e, n_used, row_tok3, row_tok3, row_tok3, xn_slab, wg, wu, wd)


def _combine_kernel(pos0_ref, pos1_ref, pos2_ref, y_hbm, wts_ref, h_ref, g_ref, o_ref, ybuf, sem, *, tc):
    j = pl.program_id(0)
    slot = j % GATHER_SLOTS
    slot1 = (j + 1) % GATHER_SLOTS
    slot2 = (j + 2) % GATHER_SLOTS

    @pl.when(j == 0)
    def _():
        def start(r, carry):
            for s, idx in ((0, pos0_ref), (1, pos1_ref)):
                _slab_copy(y_hbm, ybuf, sem, s, idx[0, 0, 2 * r], r).start()
                _slab_copy(y_hbm, ybuf, sem, s, idx[0, 0, 2 * r + 1], tc + r).start()
            return carry
        lax.fori_loop(0, tc, start, 0)

    _slab_wait(y_hbm, ybuf, sem, slot, 2 * tc)
    for r in range(tc):
        _slab_copy(y_hbm, ybuf, sem, slot2, pos2_ref[0, 0, 2 * r], r).start(priority=0)
        _slab_copy(y_hbm, ybuf, sem, slot2, pos2_ref[0, 0, 2 * r + 1], tc + r).start(priority=1)
    w = wts_ref[...]
    y = _slab_load(ybuf, slot, 0, tc) * w[:, 0:1] + _slab_load(ybuf, slot, tc, tc) * w[:, 1:2]
    o_ref[...] = _rms(h_ref[...] + y, g_ref[...])

    @pl.when(j == pl.num_programs(0) - 1)
    def _():
        _slab_wait(y_hbm, ybuf, sem, slot1, 2 * tc)
        _slab_wait(y_hbm, ybuf, sem, slot2, 2 * tc)


def _combine(pos3, y_slab, wts, h2, g, tc):
    t, d = h2.shape
    n = t // tc

    def pos_spec(ahead):
        return pl.BlockSpec((1, 1, 2 * tc), lambda i: (jnp.minimum(i + ahead, n - 1), 0, 0),
                            memory_space=pltpu.SMEM)

    return pl.pallas_call(
        functools.partial(_combine_kernel, tc=tc),
        grid=(n,),
        in_specs=[
            pos_spec(0), pos_spec(1), pos_spec(2),
            pl.BlockSpec(memory_space=pl.ANY),
            pl.BlockSpec((tc, LANES), lambda i: (i, 0)),
            pl.BlockSpec((tc, d), lambda i: (i, 0)),
            pl.BlockSpec((1, d), lambda i: (0, 0)),
        ],
        out_specs=pl.BlockSpec((tc, d), lambda i: (i, 0)),
        out_shape=jax.ShapeDtypeStruct((t, d), F32),
        scratch_shapes=[pltpu.VMEM((GATHER_SLOTS, 2 * tc * SLAB_PITCH, SLAB_W), SLAB_DTYPE),
                        pltpu.SemaphoreType.DMA((GATHER_SLOTS,))],
        compiler_params=_cparams(("arbitrary",)),
        name="combine",
    )(pos3, pos3, pos3, y_slab, wts, h2, g)


def _dispatch_plan(ids, counts, rows):
    flat_e = ids[:, 0:2].reshape(-1)
    rank = ids[:, 2:4].reshape(-1)
    n_asg = flat_e.shape[0]
    experts = jnp.arange(N_EXPERTS, dtype=jnp.int32)
    padded = ((counts + rows - 1) // rows) * rows
    pends = jnp.cumsum(padded)
    pstarts = pends - padded
    pos = jnp.sum(jnp.where(flat_e[:, None] == experts[None, :], pstarts[None, :], 0), axis=1) + rank
    n_rows = n_asg + N_EXPERTS * rows
    n_blocks = n_rows // rows
    row_tok = jnp.zeros((n_rows,), jnp.int32).at[pos].set(jnp.arange(n_asg, dtype=jnp.int32) // 2)
    starts = jnp.arange(n_blocks, dtype=jnp.int32) * rows
    blk_e = jnp.minimum(jnp.sum((pends[None, :] <= starts[:, None]).astype(jnp.int32), axis=1), N_EXPERTS - 1)
    n_used = (pends[-1] // rows).astype(jnp.int32).reshape(1)
    return pos.astype(jnp.int32), row_tok.reshape(n_blocks, 1, rows), blk_e.astype(jnp.int32), n_used


def _tile(n, pref):
    return pref if n % pref == 0 else n


def _layer(h, mem, p, l):
    bsz, seq, d = h.shape
    t = bsz * seq
    x2 = h.reshape(t, d)
    row = lambda v: v.reshape(1, -1)
    tm = _tile(t, 1024)

    w_in = p['w_in'][l]
    w_big = jnp.concatenate([w_in[:, 0:3072], w_in[:, 3104:5152], w_in[:, 5168:9264]], axis=1).astype(BF16)
    w_small = jnp.concatenate([w_in[:, 3072:3104], w_in[:, 5152:5168],
                               jnp.zeros((d, LANES - 48), F32)], axis=1).astype(BF16)
    big, small = _in_proj(x2, row(p['norm_mix'][l]), w_big, w_small, tm, 1024)
    big3 = big.reshape(bsz, seq, N_BIG)
    sm3 = small.reshape(bsz, seq, LANES)

    def lr_pad(w, off):
        return jnp.zeros((LANES, GLA_K), F32).at[off:off + GLA_RANK].set(w).astype(BF16)

    o_f = _gla_scan(big3, sm3, lr_pad(p['gla_w_lr_f'][l], SM_LRF), row(p['gla_b_lr_f'][l]), rev=False)
    y_a = _gla_scan(big3, sm3, lr_pad(p['gla_w_lr_b'][l], SM_LRB), row(p['gla_b_lr_b'][l]), rev=True,
                    prev=o_f, gn=row(p['gla_norm'][l]))

    cw = jnp.zeros((8, MLSTM_W), F32).at[:CONV_WIDTH].set(p['conv_w'][l].reshape(CONV_WIDTH, MLSTM_W))
    qm, km, vm = _mlstm_pre(big3, cw, row(p['conv_b'][l]), p['m_wq'][l].astype(BF16),
                            p['m_wk'][l].astype(BF16), p['m_wv'][l].astype(BF16), _tile(seq, 1024))
    gbias = jnp.zeros((1, LANES), F32).at[0, SM_GATES:SM_GATES + 4 * MLSTM_HEADS].set(
        p['m_gate_bias'][l].reshape(-1))
    h_f = _mlstm_scan(qm, km, vm, sm3, gbias, rev=False)
    y_b = _mlstm_scan(qm, km, vm, sm3, gbias, rev=True, prev=h_f, big3=big3, mn=row(p['m_norm'][l]))

    merged = _merge(y_a.reshape(t, GLA_V), y_b.reshape(t, MLSTM_W), p['w_branch_a'][l].astype(BF16),
                    p['w_branch_b'][l].astype(BF16), big, tm, 1024)
    h1 = _matmul_res(merged, p['w_mix_out'][l].astype(BF16), x2, tm, 1024)

    n_mem = mem.shape[1]
    kv = _kv_proj(mem.reshape(bsz * n_mem, d), row(p['norm_mem'][l]), p['w_xkv'][l].astype(BF16), 1024)
    h2 = _xattn(h1, row(p['norm_xattn'][l]), p['w_xq'][l].astype(BF16), kv.reshape(bsz, n_mem, 2 * d),
                p['w_xo'][l].astype(BF16), seq, _tile(seq, 1024))

    wr = jnp.concatenate([p['w_group'][l], p['w_router'][l].transpose(1, 0, 2).reshape(d, N_EXPERTS),
                          jnp.zeros((d, LANES - N_GROUPS - N_EXPERTS), F32)], axis=1)
    wr_hi = wr.astype(BF16)
    wr2 = jnp.stack([wr_hi, (wr - wr_hi.astype(F32)).astype(BF16)])
    br = jnp.concatenate([p['b_group'][l], p['b_router'][l].reshape(-1),
                          jnp.zeros((LANES - N_GROUPS - N_EXPERTS,), F32)]).reshape(1, LANES)
    xn3, ids, wts, cnt = _router(h2, row(p['norm_ffn'][l]), wr2, br, tm)
    pos, row_tok3, blk_e, n_used = _dispatch_plan(ids, cnt[0, :N_EXPERTS], MOE_ROWS)
    y_rows = _experts(blk_e, n_used, row_tok3, xn3, p['w_gate'][l], p['w_up'][l], p['w_down'][l])
    tc = _tile(t, COMBINE_TOK)
    return pos.reshape(t // tc, 1, 2 * tc), y_rows, wts, h2, tc


def kernel(x, mem, norm_mix, w_in, gla_w_lr_f, gla_b_lr_f, gla_w_lr_b, gla_b_lr_b, gla_norm, conv_w, conv_b, m_wq, m_wk, m_wv, m_gate_bias, m_norm, w_branch_a, w_branch_b, w_mix_out, norm_xattn, norm_mem, w_xq, w_xkv, w_xo, norm_ffn, w_group, b_group, w_router, b_router, w_gate, w_up, w_down, norm_final):
    p = dict(norm_mix=norm_mix, w_in=w_in, gla_w_lr_f=gla_w_lr_f, gla_b_lr_f=gla_b_lr_f, gla_w_lr_b=gla_w_lr_b,
             gla_b_lr_b=gla_b_lr_b, gla_norm=gla_norm, conv_w=conv_w, conv_b=conv_b, m_wq=m_wq, m_wk=m_wk,
             m_wv=m_wv, m_gate_bias=m_gate_bias, m_norm=m_norm, w_branch_a=w_branch_a, w_branch_b=w_branch_b,
             w_mix_out=w_mix_out, norm_xattn=norm_xattn, norm_mem=norm_mem, w_xq=w_xq, w_xkv=w_xkv, w_xo=w_xo,
             norm_ffn=norm_ffn, w_group=w_group, b_group=b_group, w_router=w_router, b_router=b_router,
             w_gate=w_gate, w_up=w_up, w_down=w_down)
    bsz, seq, d = x.shape
    depth = norm_mix.shape[0]
    assert depth == 1, "the final norm is fused into the last layer's combine step"
    pos3, y_rows, wts, h2, tc = _layer(x, mem, p, 0)
    out = _combine(pos3, y_rows, wts, h2, norm_final.reshape(1, d), tc)
    return out.reshape(bsz, seq, d)
```

```python
import functools

import jax
import jax.numpy as jnp
from jax import lax
from jax.experimental import pallas as pl
from jax.experimental.pallas import tpu as pltpu

F32 = jnp.float32
BF16 = jnp.bfloat16

EPS = 1e-6
LOG2_E = 1.4426950408889634
D_MODEL = 2048

GLA_HEADS = 4
GLA_DK = 128
GLA_DV = 256
GLA_K = GLA_HEADS * GLA_DK
GLA_V = GLA_HEADS * GLA_DV
GLA_RANK = 16
GLA_TAU = 16.0
GLA_CHUNK = 64
GLA_SUB = 16
GLA_SEQS = 4
GLA_SAFE_LOG2 = 60.0

MLSTM_HEADS = 4
MLSTM_DH = 256
MLSTM_W = MLSTM_HEADS * MLSTM_DH
CONV_WIDTH = 5
MLSTM_CHUNK = 256
CONV_HALO = 16

XATTN_HEADS = 4
XATTN_DH = D_MODEL // XATTN_HEADS

N_GROUPS = 4
EXPERTS_PER_GROUP = 8
N_EXPERTS = N_GROUPS * EXPERTS_PER_GROUP
D_EXPERT = 512
MOE_ROWS = 256
COMBINE_TOK = 256

LANES = 128
SLAB_W = LANES
SLAB = D_MODEL // SLAB_W
SLAB_PITCH = SLAB + 8
SLAB_DTYPE = F32
GATHER_SLOTS = 3
GATHER_BATCHES = 8

OFF_GA, OFF_GB = 0, 2048
OFF_Q, OFF_K, OFF_V, OFF_GG = 4096, 4608, 5120, 6144
OFF_MX, OFF_MZ = 7168, 8192
N_BIG = 9216
SM_LRF, SM_LRB, SM_GATES = 0, 16, 32

VMEM_LIMIT = 56 * 1024 * 1024


def _cparams(sem):
    return pltpu.CompilerParams(dimension_semantics=sem, vmem_limit_bytes=VMEM_LIMIT)


def _rms(x, g):
    return x * lax.rsqrt(jnp.mean(x * x, axis=-1, keepdims=True) + EPS) * g


def _log_sigmoid(x):
    return jnp.minimum(x, 0.0) - jnp.log(1.0 + jnp.exp(-jnp.abs(x)))


def _dot(a, b):
    return jnp.dot(a, b, preferred_element_type=F32)


def _dot_nt(a, b):
    return lax.dot_general(a, b, (((1,), (1,)), ((), ())), preferred_element_type=F32)


def _dot_tn(a, b):
    return lax.dot_general(a, b, (((0,), (0,)), ((), ())), preferred_element_type=F32)


def _order_mask(c, rev):
    t = lax.broadcasted_iota(jnp.int32, (c, c), 0)
    s = lax.broadcasted_iota(jnp.int32, (c, c), 1)
    return (s >= t) if rev else (s <= t)


def _split3(x):
    hi = x.astype(BF16)
    r1 = x - hi.astype(F32)
    mid = r1.astype(BF16)
    return hi, mid, (r1 - mid.astype(F32)).astype(BF16)


def _cumsum_mm(mask_bf16, x):
    hi, mid, lo = _split3(x)
    return _dot(mask_bf16, hi) + _dot(mask_bf16, mid) + _dot(mask_bf16, lo)


def _in_proj_kernel(x_ref, g_ref, wbig_ref, wsm_ref, big_ref, sm_ref, xn_ref):
    @pl.when(pl.program_id(1) == 0)
    def _():
        xn = _rms(x_ref[...], g_ref[...]).astype(BF16)
        xn_ref[...] = xn
        sm_ref[...] = _dot(xn, wsm_ref[...])
    big_ref[...] = _dot(xn_ref[...], wbig_ref[...]).astype(BF16)


def _in_proj(x2, g, w_big, w_small, tm, tn):
    t, d = x2.shape
    n = w_big.shape[1]
    return pl.pallas_call(
        _in_proj_kernel,
        grid=(t // tm, n // tn),
        in_specs=[
            pl.BlockSpec((tm, d), lambda i, j: (i, 0)),
            pl.BlockSpec((1, d), lambda i, j: (0, 0)),
            pl.BlockSpec((d, tn), lambda i, j: (0, j)),
            pl.BlockSpec((d, LANES), lambda i, j: (0, 0)),
        ],
        out_specs=[
            pl.BlockSpec((tm, tn), lambda i, j: (i, j)),
            pl.BlockSpec((tm, LANES), lambda i, j: (i, 0)),
        ],
        out_shape=[jax.ShapeDtypeStruct((t, n), BF16), jax.ShapeDtypeStruct((t, LANES), F32)],
        scratch_shapes=[pltpu.VMEM((tm, d), BF16)],
        compiler_params=_cparams(("parallel", "arbitrary")),
        name="in_proj",
    )(x2, g, w_big, w_small)


def _gla_scores_any_decay(bh, qh, kh, kh_b, rev, c):
    sb = GLA_SUB
    n_sub = c // sb
    col = lax.broadcasted_iota(jnp.int32, (sb, c), 1)
    trow = lax.broadcasted_iota(jnp.int32, (sb, c), 0)
    slabs = [(qh[i * sb:(i + 1) * sb]
              * jnp.exp2(bh[i * sb:(i + 1) * sb] - bh[i * sb + s:i * sb + s + 1, :])).astype(BF16)
             for i in range(n_sub) for s in range(sb)]
    g = _dot_nt(jnp.concatenate(slabs, axis=0), kh_b)
    rows = []
    for i in range(n_sub):
        r0 = i * sb
        ref_row = r0 + (sb - 1 if rev else 0)
        beta = bh[ref_row:ref_row + 1, :]
        qt = (qh[r0:r0 + sb] * jnp.exp2(bh[r0:r0 + sb] - beta)).astype(BF16)
        kt = (kh * jnp.exp2(beta - bh)).astype(BF16)
        a_off = _dot_nt(qt, kt)
        a_diag = jnp.zeros((sb, c), F32)
        for s in range(sb):
            g0 = (i * sb + s) * sb
            a_diag = jnp.where(col == r0 + s, g[g0:g0 + sb], a_diag)
        tr = trow + r0
        if rev:
            off_mask = col >= r0 + sb
            diag_mask = (col >= tr) & (col < r0 + sb)
        else:
            off_mask = col < r0
            diag_mask = (col <= tr) & (col >= r0)
        rows.append(jnp.where(off_mask, a_off, 0.0) + jnp.where(diag_mask, a_diag, 0.0))
    return jnp.concatenate(rows, axis=0).astype(BF16)


def _gla_kernel(q_ref, k_ref, v_ref, sm_ref, wlr_ref, blr_ref, *rest, rev, final, c, nb):
    if final:
        oprev_ref, gg_ref, gn_ref, o_ref, st_ref, a_ref = rest
    else:
        o_ref, st_ref, a_ref = rest

    @pl.when(pl.program_id(1) == 0)
    def _():
        st_ref[...] = jnp.zeros_like(st_ref)

    mask = _order_mask(c, rev)
    mask_b = jnp.where(mask, 1.0, 0.0).astype(BF16)
    last = 0 if rev else c - 1

    for bi in range(nb):
        x = _dot(sm_ref[bi].astype(BF16), wlr_ref[...]) + blr_ref[...]
        la = _log_sigmoid(x) * (1.0 / GLA_TAU)
        b = _cumsum_mm(mask_b, la) * LOG2_E
        tot = b[last:last + 1, :]
        q = q_ref[bi].astype(F32) * (GLA_DK ** -0.5)
        k = k_ref[bi].astype(F32)
        q_in = (q * jnp.exp2(b)).astype(BF16)
        k_dec = (k * jnp.exp2(tot - b)).astype(BF16)
        e_tot = jnp.exp2(tot)

        mild = jnp.min(tot) > -GLA_SAFE_LOG2

        @pl.when(mild)
        def _():
            q_up = (q * jnp.exp2(b - tot)).astype(BF16)
            for h in range(GLA_HEADS):
                ks = slice(h * GLA_DK, (h + 1) * GLA_DK)
                a_ref[bi, h] = jnp.where(mask, _dot_nt(q_up[:, ks], k_dec[:, ks]), 0.0).astype(BF16)

        @pl.when(jnp.logical_not(mild))
        def _():
            for h in range(GLA_HEADS):
                ks = slice(h * GLA_DK, (h + 1) * GLA_DK)
                a_ref[bi, h] = _gla_scores_any_decay(b[:, ks], q[:, ks], k[:, ks], k_ref[bi, :, ks], rev, c)

        for h in range(GLA_HEADS):
            ks = slice(h * GLA_DK, (h + 1) * GLA_DK)
            vs = slice(h * GLA_DV, (h + 1) * GLA_DV)
            vh = v_ref[bi, :, vs]
            st = st_ref[bi, h]
            o_inter = _dot_nt(q_in[:, ks], st.astype(BF16))
            st_ref[bi, h] = st * e_tot[:, ks] + _dot_tn(vh, k_dec[:, ks])
            o = o_inter + _dot(a_ref[bi, h], vh)
            if final:
                o = o + oprev_ref[bi, :, vs]
                y = _rms(o, gn_ref[:, vs])
                gg = gg_ref[bi, :, vs].astype(F32)
                o_ref[bi, :, vs] = (y * (gg * jax.nn.sigmoid(gg))).astype(o_ref.dtype)
            else:
                o_ref[bi, :, vs] = o.astype(o_ref.dtype)


def _gla_scan(big3, sm3, wlr, blr, rev, prev=None, gn=None):
    bsz, seq, _ = big3.shape
    c = GLA_CHUNK
    n = seq // c
    nb = GLA_SEQS if bsz % GLA_SEQS == 0 else 1
    final = prev is not None

    def cm(ci):
        return (n - 1 - ci) if rev else ci

    in_specs = [
        pl.BlockSpec((nb, c, GLA_K), lambda b, ci: (b, cm(ci), OFF_Q // GLA_K)),
        pl.BlockSpec((nb, c, GLA_K), lambda b, ci: (b, cm(ci), OFF_K // GLA_K)),
        pl.BlockSpec((nb, c, GLA_V), lambda b, ci: (b, cm(ci), OFF_V // GLA_V)),
        pl.BlockSpec((nb, c, LANES), lambda b, ci: (b, cm(ci), 0)),
        pl.BlockSpec((LANES, GLA_K), lambda b, ci: (0, 0)),
        pl.BlockSpec((1, GLA_K), lambda b, ci: (0, 0)),
    ]
    args = [big3, big3, big3, sm3, wlr, blr]
    if final:
        in_specs += [
            pl.BlockSpec((nb, c, GLA_V), lambda b, ci: (b, cm(ci), 0)),
            pl.BlockSpec((nb, c, GLA_V), lambda b, ci: (b, cm(ci), OFF_GG // GLA_V)),
            pl.BlockSpec((1, GLA_V), lambda b, ci: (0, 0)),
        ]
        args += [prev, big3, gn]
    return pl.pallas_call(
        functools.partial(_gla_kernel, rev=rev, final=final, c=c, nb=nb),
        grid=(bsz // nb, n),
        in_specs=in_specs,
        out_specs=pl.BlockSpec((nb, c, GLA_V), lambda b, ci: (b, cm(ci), 0)),
        out_shape=jax.ShapeDtypeStruct((bsz, seq, GLA_V), BF16 if final else F32),
        scratch_shapes=[pltpu.VMEM((nb, GLA_HEADS, GLA_DV, GLA_DK), F32),
                        pltpu.VMEM((nb, GLA_HEADS, c, c), BF16)],
        compiler_params=_cparams(("parallel", "arbitrary")),
        name="gla_bwd" if rev else "gla_fwd",
    )(*args)


def _mlstm_pre_kernel(cur_ref, prev_ref, next_ref, cw_ref, cb_ref, wq_ref, wk_ref, wv_ref,
                      q_ref, k_ref, v_ref, *, tm):
    i = pl.program_id(2)
    cur_b = cur_ref[0]
    cur = cur_b.astype(F32)
    halo = CONV_WIDTH // 2
    prev = jnp.where(i > 0, prev_ref[0].astype(F32), 0.0)
    nxt = jnp.where(i < pl.num_programs(2) - 1, next_ref[0].astype(F32), 0.0)
    ext = jnp.concatenate([prev[CONV_HALO - 8:], cur, nxt[:8]], axis=0)
    acc = jnp.zeros_like(cur) + cb_ref[...]
    for w in range(CONV_WIDTH):
        off = 8 - halo + w
        acc = acc + ext[off:off + tm] * cw_ref[w:w + 1, :]
    xc = (acc * jax.nn.sigmoid(acc)).astype(BF16)
    q_ref[0] = _dot(xc, wq_ref[0]).astype(BF16)
    k_ref[0] = (_dot(xc, wk_ref[0]) * (MLSTM_DH ** -0.5)).astype(BF16)
    v_ref[0] = _dot(cur_b, wv_ref[0]).astype(BF16)


def _mlstm_pre(big3, cw, cb, wq, wk, wv, tm):
    bsz, seq, _ = big3.shape
    dh = MLSTM_DH
    nh = tm // CONV_HALO
    n_halo = seq // CONV_HALO
    c0 = OFF_MX // dh
    out = jax.ShapeDtypeStruct((bsz, seq, MLSTM_W), BF16)
    ospec = pl.BlockSpec((1, tm, dh), lambda b, h, i: (b, i, h))
    wspec = pl.BlockSpec((1, dh, dh), lambda b, h, i: (h, 0, 0))
    return pl.pallas_call(
        functools.partial(_mlstm_pre_kernel, tm=tm),
        grid=(bsz, MLSTM_HEADS, seq // tm),
        in_specs=[
            pl.BlockSpec((1, tm, dh), lambda b, h, i: (b, i, c0 + h)),
            pl.BlockSpec((1, CONV_HALO, dh), lambda b, h, i: (b, jnp.maximum(i * nh - 1, 0), c0 + h)),
            pl.BlockSpec((1, CONV_HALO, dh), lambda b, h, i: (b, jnp.minimum((i + 1) * nh, n_halo - 1), c0 + h)),
            pl.BlockSpec((8, dh), lambda b, h, i: (0, h)),
            pl.BlockSpec((1, dh), lambda b, h, i: (0, h)),
            wspec, wspec, wspec,
        ],
        out_specs=[ospec, ospec, ospec],
        out_shape=[out, out, out],
        compiler_params=_cparams(("parallel", "parallel", "parallel")),
        name="mlstm_pre",
    )(big3, big3, big3, cw, cb, wq, wk, wv)


def _mlstm_kernel(q_ref, k_ref, v_ref, sm_ref, gb_ref, *rest, rev, final, c):
    if final:
        hprev_ref, mz_ref, mn_ref, o_ref, c_ref, n_ref, m_ref = rest
    else:
        o_ref, c_ref, n_ref, m_ref = rest

    @pl.when(pl.program_id(1) == 0)
    def _():
        c_ref[...] = jnp.zeros_like(c_ref)
        n_ref[...] = jnp.zeros_like(n_ref)
        m_ref[...] = jnp.zeros_like(m_ref)

    g = sm_ref[0] + gb_ref[...]
    mask = _order_mask(c, rev)
    bcum = _cumsum_mm(jnp.where(mask, 1.0, 0.0).astype(BF16), _log_sigmoid(g))
    g_t = g.T
    b_t = bcum.T
    last = 0 if rev else c - 1
    dh = MLSTM_DH

    for h in range(MLSTM_HEADS):
        ci = SM_GATES + (2 * MLSTM_HEADS if rev else 0) + h
        cf = ci + MLSTM_HEADS
        hs = slice(h * dh, (h + 1) * dh)
        i_col, b_col = g[:, ci:ci + 1], bcum[:, cf:cf + 1]
        i_row, b_row = g_t[ci:ci + 1, :], b_t[cf:cf + 1, :]
        tot = b_col[last:last + 1, :]
        m_prev = m_ref[h, 0:1, 0:1]
        qh, kh, vh = q_ref[0, :, hs], k_ref[0, :, hs], v_ref[0, :, hs]
        c_st = c_ref[h]
        n_st = n_ref[h, 0:1, :]

        dmat = jnp.where(mask, b_col - b_row + i_row, -jnp.inf)
        inter_log = b_col + m_prev
        m_t = jnp.maximum(inter_log, jnp.max(dmat, axis=-1, keepdims=True))
        w_intra = jnp.exp(dmat - m_t)
        w_inter = jnp.exp(inter_log - m_t)
        s = _dot_nt(qh, kh) * w_intra
        num = w_inter * _dot(qh, c_st.astype(BF16)) + _dot(s.astype(BF16), vh)
        den = (w_inter * jnp.sum(qh.astype(F32) * n_st, axis=-1, keepdims=True)
               + jnp.sum(s, axis=-1, keepdims=True))
        hh = num / jnp.maximum(jnp.abs(den), jnp.exp(-m_t))

        upd_col = tot - b_col + i_col
        upd_row = tot - b_row + i_row
        m_new = jnp.maximum(tot + m_prev, jnp.max(upd_row, axis=-1, keepdims=True))
        w_old = jnp.exp(tot + m_prev - m_new)
        kw = kh.astype(F32) * jnp.exp(upd_col - m_new)
        c_ref[h] = w_old * c_st + _dot_tn(kw.astype(BF16), vh)
        n_ref[h] = jnp.broadcast_to(w_old * n_st + jnp.sum(kw, axis=0, keepdims=True), (8, dh))
        m_ref[h] = jnp.broadcast_to(m_new, (8, LANES))

        if final:
            hh = hh + hprev_ref[0, :, hs]
            y = _rms(hh, mn_ref[:, hs])
            o_ref[0, :, hs] = (y * jax.nn.sigmoid(mz_ref[0, :, hs].astype(F32))).astype(o_ref.dtype)
        else:
            o_ref[0, :, hs] = hh.astype(o_ref.dtype)


def _mlstm_scan(qm, km, vm, sm3, gbias, rev, prev=None, big3=None, mn=None):
    bsz, seq, w = qm.shape
    c = min(MLSTM_CHUNK, seq)
    n = seq // c
    final = prev is not None

    def cm(ci):
        return (n - 1 - ci) if rev else ci

    xspec = pl.BlockSpec((1, c, w), lambda b, ci: (b, cm(ci), 0))
    in_specs = [xspec, xspec, xspec,
                pl.BlockSpec((1, c, LANES), lambda b, ci: (b, cm(ci), 0)),
                pl.BlockSpec((1, LANES), lambda b, ci: (0, 0))]
    args = [qm, km, vm, sm3, gbias]
    if final:
        in_specs += [xspec,
                     pl.BlockSpec((1, c, w), lambda b, ci: (b, cm(ci), OFF_MZ // MLSTM_W)),
                     pl.BlockSpec((1, w), lambda b, ci: (0, 0))]
        args += [prev, big3, mn]
    return pl.pallas_call(
        functools.partial(_mlstm_kernel, rev=rev, final=final, c=c),
        grid=(bsz, n),
        in_specs=in_specs,
        out_specs=xspec,
        out_shape=jax.ShapeDtypeStruct((bsz, seq, w), BF16 if final else F32),
        scratch_shapes=[pltpu.VMEM((MLSTM_HEADS, MLSTM_DH, MLSTM_DH), F32),
                        pltpu.VMEM((MLSTM_HEADS, 8, MLSTM_DH), F32),
                        pltpu.VMEM((MLSTM_HEADS, 8, LANES), F32)],
        compiler_params=_cparams(("parallel", "arbitrary")),
        name="mlstm_bwd" if rev else "mlstm_fwd",
    )(*args)


def _merge_kernel(ya_ref, yb_ref, wa_ref, wb_ref, ga_ref, gb_ref, o_ref):
    a = _dot(ya_ref[...], wa_ref[...])
    b = _dot(yb_ref[...], wb_ref[...])
    ga = jax.nn.sigmoid(ga_ref[...].astype(F32))
    gb = jax.nn.sigmoid(gb_ref[...].astype(F32))
    o_ref[...] = (ga * a + gb * b).astype(o_ref.dtype)


def _merge(ya, yb, wa, wb, big, tm, tn):
    t, kdim = ya.shape
    n = wa.shape[1]
    assert OFF_GA % tn == 0 and OFF_GB % tn == 0
    return pl.pallas_call(
        _merge_kernel,
        grid=(t // tm, n // tn),
        in_specs=[
            pl.BlockSpec((tm, kdim), lambda i, j: (i, 0)),
            pl.BlockSpec((tm, kdim), lambda i, j: (i, 0)),
            pl.BlockSpec((kdim, tn), lambda i, j: (0, j)),
            pl.BlockSpec((kdim, tn), lambda i, j: (0, j)),
            pl.BlockSpec((tm, tn), lambda i, j: (i, OFF_GA // tn + j)),
            pl.BlockSpec((tm, tn), lambda i, j: (i, OFF_GB // tn + j)),
        ],
        out_specs=pl.BlockSpec((tm, tn), lambda i, j: (i, j)),
        out_shape=jax.ShapeDtypeStruct((t, n), BF16),
        compiler_params=_cparams(("parallel", "parallel")),
        name="merge",
    )(ya, yb, wa, wb, big, big)


def _matmul_res_kernel(a_ref, w_ref, r_ref, o_ref):
    o_ref[...] = r_ref[...] + _dot(a_ref[...], w_ref[...])


def _matmul_res(a, w, res, tm, tn):
    t, kdim = a.shape
    n = w.shape[1]
    return pl.pallas_call(
        _matmul_res_kernel,
        grid=(t // tm, n // tn),
        in_specs=[
            pl.BlockSpec((tm, kdim), lambda i, j: (i, 0)),
            pl.BlockSpec((kdim, tn), lambda i, j: (0, j)),
            pl.BlockSpec((tm, tn), lambda i, j: (i, j)),
        ],
        out_specs=pl.BlockSpec((tm, tn), lambda i, j: (i, j)),
        out_shape=jax.ShapeDtypeStruct((t, n), F32),
        compiler_params=_cparams(("parallel", "parallel")),
        name="mix_out",
    )(a, w, res)


def _kv_kernel(mem_ref, g_ref, w_ref, o_ref):
    mn = _rms(mem_ref[...], g_ref[...]).astype(BF16)
    o_ref[...] = _dot(mn, w_ref[...]).astype(BF16)


def _kv_proj(mem2, g, w, tn):
    r, d = mem2.shape
    n = w.shape[1]
    return pl.pallas_call(
        _kv_kernel,
        grid=(n // tn,),
        in_specs=[
            pl.BlockSpec((r, d), lambda j: (0, 0)),
            pl.BlockSpec((1, d), lambda j: (0, 0)),
            pl.BlockSpec((d, tn), lambda j: (0, j)),
        ],
        out_specs=pl.BlockSpec((r, tn), lambda j: (0, j)),
        out_shape=jax.ShapeDtypeStruct((r, n), BF16),
        compiler_params=_cparams(("parallel",)),
        name="kv_proj",
    )(mem2, g, w)


def _xattn_kernel(h_ref, g_ref, wq_ref, k_ref, v_ref, wo_ref, o_ref, xn_ref):
    @pl.when(pl.program_id(1) == 0)
    def _():
        x = h_ref[...]
        xn_ref[...] = _rms(x, g_ref[...]).astype(BF16)
        o_ref[...] = x
    q = _dot(xn_ref[...], wq_ref[...]).astype(BF16)
    s = _dot_nt(q, k_ref[0]) * (XATTN_DH ** -0.5)
    p = jnp.exp(s - jnp.max(s, axis=-1, keepdims=True))
    p = p / jnp.sum(p, axis=-1, keepdims=True)
    o = _dot(p.astype(BF16), v_ref[0]).astype(BF16)
    o_ref[...] += _dot(o, wo_ref[...])


def _xattn(h1, g, wq, kv3, wo, seq, tm):
    t, d = h1.shape
    n_mem = kv3.shape[1]
    dh = XATTN_DH
    per_b = seq // tm
    return pl.pallas_call(
        _xattn_kernel,
        grid=(t // tm, XATTN_HEADS),
        in_specs=[
            pl.BlockSpec((tm, d), lambda i, h: (i, 0)),
            pl.BlockSpec((1, d), lambda i, h: (0, 0)),
            pl.BlockSpec((d, dh), lambda i, h: (0, h)),
            pl.BlockSpec((1, n_mem, dh), lambda i, h: (i // per_b, 0, h)),
            pl.BlockSpec((1, n_mem, dh), lambda i, h: (i // per_b, 0, XATTN_HEADS + h)),
            pl.BlockSpec((dh, d), lambda i, h: (h, 0)),
        ],
        out_specs=pl.BlockSpec((tm, d), lambda i, h: (i, 0)),
        out_shape=jax.ShapeDtypeStruct((t, d), F32),
        scratch_shapes=[pltpu.VMEM((tm, d), BF16)],
        compiler_params=_cparams(("parallel", "arbitrary")),
        name="xattn",
    )(h1, g, wq, kv3, kv3, wo)


def _slab_store(ref, val):
    n = val.shape[0]
    for c in range(SLAB):
        ref[pl.ds(c, n, stride=SLAB), :] = val[:, c * SLAB_W:(c + 1) * SLAB_W].astype(ref.dtype)


def _slab_load(buf, slot, first, n):
    return jnp.concatenate(
        [buf[slot, pl.ds(first * SLAB_PITCH + c, n, stride=SLAB_PITCH), :] for c in range(SLAB)], axis=1)


def _slab_copy(src_hbm, dst_buf, sem, slot, src_slab, dst_slab):
    return pltpu.make_async_copy(src_hbm.at[pl.ds(src_slab * SLAB, SLAB)],
                                 dst_buf.at[slot, pl.ds(dst_slab * SLAB_PITCH, SLAB)], sem.at[slot])


def _zero_after(v):
    bits = lax.bitcast_convert_type(v[-1:, -1:], jnp.uint32)
    return ((bits >> 16) >> 16).astype(jnp.int32)[0, 0]


def _slab_wait(src_hbm, dst_buf, sem, slot, n):
    pltpu.make_async_copy(src_hbm.at[pl.ds(0, n * SLAB)], dst_buf.at[slot, pl.ds(0, n * SLAB)], sem.at[slot]).wait()


def _router_kernel(h_ref, g_ref, wr_ref, br_ref, xn_ref, ids_ref, wts_ref, cnt_ref, carry_ref):
    @pl.when(pl.program_id(0) == 0)
    def _():
        carry_ref[...] = jnp.zeros_like(carry_ref)

    xn = _rms(h_ref[...], g_ref[...])
    _slab_store(xn_ref, xn)
    hi = xn.astype(BF16)
    lo = (xn - hi.astype(F32)).astype(BF16)
    lg = _dot(hi, wr_ref[0]) + _dot(hi, wr_ref[1]) + _dot(lo, wr_ref[0]) + br_ref[...]
    lane = lax.broadcasted_iota(jnp.int32, lg.shape, 1).astype(F32)
    ninf = -jnp.inf
    big_lane = float(LANES)

    def first_max(v):
        mx = jnp.max(v, axis=-1, keepdims=True)
        return mx, jnp.min(jnp.where(v == mx, lane, big_lane), axis=-1, keepdims=True)

    gl = jnp.where(lane < N_GROUPS, lg, ninf)
    gmax, g_sel = first_max(gl)
    p_g = 1.0 / jnp.sum(jnp.exp(gl - gmax), axis=-1, keepdims=True)
    lo_l = N_GROUPS + EXPERTS_PER_GROUP * g_sel
    el = jnp.where((lane >= lo_l) & (lane < lo_l + EXPERTS_PER_GROUP), lg, ninf)
    emax, i1 = first_max(el)
    max2, i2 = first_max(jnp.where(lane == i1, ninf, el))
    e2 = jnp.exp(max2 - emax)
    w1 = p_g / (1.0 + e2)
    w2 = p_g * e2 / (1.0 + e2)
    e1, e2 = i1 - N_GROUPS, i2 - N_GROUPS
    oh1, oh2 = lane == e1, lane == e2
    oh = jnp.where(oh1, 1.0, 0.0) + jnp.where(oh2, 1.0, 0.0)
    tm = lg.shape[0]
    earlier = lax.broadcasted_iota(jnp.int32, (tm, tm), 1) < lax.broadcasted_iota(jnp.int32, (tm, tm), 0)
    carry = carry_ref[0:1, :]
    before = _dot(jnp.where(earlier, 1.0, 0.0).astype(BF16), oh.astype(BF16)) + carry
    r1 = jnp.sum(jnp.where(oh1, before, 0.0), axis=-1, keepdims=True)
    r2 = jnp.sum(jnp.where(oh2, before, 0.0), axis=-1, keepdims=True)
    counts = carry + jnp.sum(oh, axis=0, keepdims=True)
    carry_ref[...] = jnp.broadcast_to(counts, carry_ref.shape)
    cnt_ref[...] = jnp.broadcast_to(counts, cnt_ref.shape).astype(jnp.int32)
    ids = jnp.where(lane == 0.0, e1, jnp.where(lane == 1.0, e2, jnp.where(lane == 2.0, r1,
                                                                         jnp.where(lane == 3.0, r2, 0.0))))
    ids_ref[...] = ids.astype(jnp.int32)
    wts_ref[...] = jnp.where(lane == 0.0, w1, jnp.where(lane == 1.0, w2, 0.0))


def _router(h2, g, wr, br, tm):
    t, d = h2.shape
    return pl.pallas_call(
        _router_kernel,
        grid=(t // tm,),
        in_specs=[
            pl.BlockSpec((tm, d), lambda i: (i, 0)),
            pl.BlockSpec((1, d), lambda i: (0, 0)),
            pl.BlockSpec((2, d, LANES), lambda i: (0, 0, 0)),
            pl.BlockSpec((1, LANES), lambda i: (0, 0)),
        ],
        out_specs=[
            pl.BlockSpec((tm * SLAB, SLAB_W), lambda i: (i, 0)),
            pl.BlockSpec((tm, LANES), lambda i: (i, 0)),
            pl.BlockSpec((tm, LANES), lambda i: (i, 0)),
            pl.BlockSpec((8, LANES), lambda i: (0, 0)),
        ],
        out_shape=[jax.ShapeDtypeStruct((t * SLAB, SLAB_W), SLAB_DTYPE),
                   jax.ShapeDtypeStruct((t, LANES), jnp.int32),
                   jax.ShapeDtypeStruct((t, LANES), F32),
                   jax.ShapeDtypeStruct((8, LANES), jnp.int32)],
        scratch_shapes=[pltpu.VMEM((8, LANES), F32)],
        compiler_params=_cparams(("arbitrary",)),
        name="router",
    )(h2, g, wr, br)


def _experts_kernel(blk_e_ref, n_used_ref, next_e_ref, wslot_ref, tok0_ref, tok1_ref, tok2_ref, x_hbm,
                    wg_hbm, wu_hbm, wd_hbm, y_ref, xbuf, wg_f, wu_f, wd_f, wg_b, wu_b, wd_b, sem, wsem, *, rows):
    j = pl.program_id(0)
    n_used = n_used_ref[0]
    used = j < n_used
    slot = j % GATHER_SLOTS
    slot1 = (j + 1) % GATHER_SLOTS
    slot2 = (j + 2) % GATHER_SLOTS

    @pl.when(j == 0)
    def _():
        def start(r, carry):
            _slab_copy(x_hbm, xbuf, sem, 0, tok0_ref[0, 0, r], r).start()
            _slab_copy(x_hbm, xbuf, sem, 1, tok1_ref[0, 0, r], r).start()
            return carry
        lax.fori_loop(0, rows, start, 0)

    @pl.when(jnp.logical_not(used))
    def _():
        @pl.when(j == n_used)
        def _():
            _slab_wait(x_hbm, xbuf, sem, slot, rows)
            _slab_wait(x_hbm, xbuf, sem, slot1, rows)
        y_ref[...] = jnp.zeros_like(y_ref)

    def weight_copies(e, ws):
        return [pltpu.make_async_copy(src.at[e], dst.at[ws], wsem.at[ws])
                for src, dst in ((wg_hbm, wg_f), (wu_hbm, wu_f), (wd_hbm, wd_f))]

    @pl.when(j == 0)
    def _():
        for cp in weight_copies(blk_e_ref[0], 0):
            cp.start()

    @pl.when(used)
    def _():
        _slab_wait(x_hbm, xbuf, sem, slot, rows)

        @pl.when((j == 0) | (blk_e_ref[j] != blk_e_ref[jnp.maximum(j - 1, 0)]))
        def _():
            ws = wslot_ref[j]
            for cp in weight_copies(0, ws):
                cp.wait()

            @pl.when(next_e_ref[j] >= 0)
            def _():
                for cp in weight_copies(next_e_ref[j], 1 - ws):
                    cp.start()

            wg_b[...] = wg_f[ws].astype(BF16)
            wu_b[...] = wu_f[ws].astype(BF16)
            wd_b[...] = wd_f[ws].astype(BF16)

        batch = rows // GATHER_BATCHES
        issued = [0]

        def gather_batch(after):
            zero = 0 if after is None else _zero_after(after)
            for r in range(issued[0] * batch, (issued[0] + 1) * batch):
                _slab_copy(x_hbm, xbuf, sem, slot2, tok2_ref[0, 0, r] + zero, r).start()
            issued[0] += 1

        xb = _slab_load(xbuf, slot, 0, rows).astype(BF16)
        gather_batch(None)
        de = wg_b.shape[1]
        halves = [slice(0, de // 2), slice(de // 2, de)]
        gate, up = [], []
        for cs in halves:
            gate.append(_dot(xb, wg_b[:, cs]))
            gather_batch(gate[-1])
        for cs in halves:
            up.append(_dot(xb, wu_b[:, cs]))
            gather_batch(up[-1])
        gate = jnp.concatenate(gate, axis=1)
        act = (gate * jax.nn.sigmoid(gate) * jnp.concatenate(up, axis=1)).astype(BF16)
        wo = de
        per = wo // SLAB_W
        for c in range(wd_b.shape[1] // wo):
            yc = _dot(act, wd_b[:, c * wo:(c + 1) * wo])
            for cc in range(per):
                y_ref[pl.ds(c * per + cc, rows, stride=SLAB), :] = (
                    yc[:, cc * SLAB_W:(cc + 1) * SLAB_W].astype(y_ref.dtype))
            if issued[0] < GATHER_BATCHES:
                gather_batch(yc)
        assert issued[0] == GATHER_BATCHES

        @pl.when(j == pl.num_programs(0) - 1)
        def _():
            _slab_wait(x_hbm, xbuf, sem, slot1, rows)
            _slab_wait(x_hbm, xbuf, sem, slot2, rows)


def _experts(blk_e, n_used, next_e, wslot, row_tok3, xn_slab, wg, wu, wd):
    n_blocks, _, rows = row_tok3.shape
    _, d, de = wg.shape

    def tok_spec(ahead):
        return pl.BlockSpec((1, 1, rows), lambda j, *_: (jnp.minimum(j + ahead, n_blocks - 1), 0, 0),
                            memory_space=pltpu.SMEM)

    hbm = pl.BlockSpec(memory_space=pl.ANY)
    grid_spec = pltpu.PrefetchScalarGridSpec(
        num_scalar_prefetch=4,
        grid=(n_blocks,),
        in_specs=[tok_spec(0), tok_spec(1), tok_spec(2), hbm, hbm, hbm, hbm],
        out_specs=pl.BlockSpec((rows * SLAB, SLAB_W), lambda j, *_: (j, 0)),
        scratch_shapes=[pltpu.VMEM((GATHER_SLOTS, rows * SLAB_PITCH, SLAB_W), SLAB_DTYPE),
                        pltpu.VMEM((2, d, de), F32), pltpu.VMEM((2, d, de), F32), pltpu.VMEM((2, de, d), F32),
                        pltpu.VMEM((d, de), BF16), pltpu.VMEM((d, de), BF16), pltpu.VMEM((de, d), BF16),
                        pltpu.SemaphoreType.DMA((GATHER_SLOTS,)), pltpu.SemaphoreType.DMA((2,))],
    )
    return pl.pallas_call(
        functools.partial(_experts_kernel, rows=rows),
        grid_spec=grid_spec,
        out_shape=jax.ShapeDtypeStruct((n_blocks * rows * SLAB, SLAB_W), SLAB_DTYPE),
        compiler_params=_cparams(("arbitrary",)),
        name="experts",
    )(blk_e, n_used, next_e, wslot, row_tok3, row_tok3, row_tok3, xn_slab, wg, wu, wd)


def _combine_kernel(pos0_ref, pos1_ref, pos2_ref, y_hbm, wts_ref, h_ref, g_ref, o_ref, ybuf, sem, *, tc):
    j = pl.program_id(0)
    slot = j % GATHER_SLOTS
    slot1 = (j + 1) % GATHER_SLOTS
    slot2 = (j + 2) % GATHER_SLOTS

    @pl.when(j == 0)
    def _():
        def start(r, carry):
            for s, idx in ((0, pos0_ref), (1, pos1_ref)):
                _slab_copy(y_hbm, ybuf, sem, s, idx[0, 0, 2 * r], r).start()
                _slab_copy(y_hbm, ybuf, sem, s, idx[0, 0, 2 * r + 1], tc + r).start()
            return carry
        lax.fori_loop(0, tc, start, 0)

    _slab_wait(y_hbm, ybuf, sem, slot, 2 * tc)
    for r in range(tc):
        _slab_copy(y_hbm, ybuf, sem, slot2, pos2_ref[0, 0, 2 * r], r).start(priority=0)
        _slab_copy(y_hbm, ybuf, sem, slot2, pos2_ref[0, 0, 2 * r + 1], tc + r).start(priority=1)
    w = wts_ref[...]
    y = (_slab_load(ybuf, slot, 0, tc).astype(F32) * w[:, 0:1]
         + _slab_load(ybuf, slot, tc, tc).astype(F32) * w[:, 1:2])
    o_ref[...] = _rms(h_ref[...] + y, g_ref[...])

    @pl.when(j == pl.num_programs(0) - 1)
    def _():
        _slab_wait(y_hbm, ybuf, sem, slot1, 2 * tc)
        _slab_wait(y_hbm, ybuf, sem, slot2, 2 * tc)


def _combine(pos3, y_slab, wts, h2, g, tc):
    t, d = h2.shape
    n = t // tc

    def pos_spec(ahead):
        return pl.BlockSpec((1, 1, 2 * tc), lambda i: (jnp.minimum(i + ahead, n - 1), 0, 0),
                            memory_space=pltpu.SMEM)

    return pl.pallas_call(
        functools.partial(_combine_kernel, tc=tc),
        grid=(n,),
        in_specs=[
            pos_spec(0), pos_spec(1), pos_spec(2),
            pl.BlockSpec(memory_space=pl.ANY),
            pl.BlockSpec((tc, LANES), lambda i: (i, 0)),
            pl.BlockSpec((tc, d), lambda i: (i, 0)),
            pl.BlockSpec((1, d), lambda i: (0, 0)),
        ],
        out_specs=pl.BlockSpec((tc, d), lambda i: (i, 0)),
        out_shape=jax.ShapeDtypeStruct((t, d), F32),
        scratch_shapes=[pltpu.VMEM((GATHER_SLOTS, 2 * tc * SLAB_PITCH, SLAB_W), SLAB_DTYPE),
                        pltpu.SemaphoreType.DMA((GATHER_SLOTS,))],
        compiler_params=_cparams(("arbitrary",)),
        name="combine",
    )(pos3, pos3, pos3, y_slab, wts, h2, g)


def _dispatch_plan(ids, counts, rows):
    flat_e = ids[:, 0:2].reshape(-1)
    rank = ids[:, 2:4].reshape(-1)
    n_asg = flat_e.shape[0]
    experts = jnp.arange(N_EXPERTS, dtype=jnp.int32)
    padded = ((counts + rows - 1) // rows) * rows
    pends = jnp.cumsum(padded)
    pstarts = pends - padded
    pos = jnp.sum(jnp.where(flat_e[:, None] == experts[None, :], pstarts[None, :], 0), axis=1) + rank
    n_rows = n_asg + N_EXPERTS * rows
    n_blocks = n_rows // rows
    row_tok = jnp.zeros((n_rows,), jnp.int32).at[pos].set(jnp.arange(n_asg, dtype=jnp.int32) // 2)
    starts = jnp.arange(n_blocks, dtype=jnp.int32) * rows
    blk_e = jnp.minimum(jnp.sum((pends[None, :] <= starts[:, None]).astype(jnp.int32), axis=1), N_EXPERTS - 1)
    n_used = (pends[-1] // rows).astype(jnp.int32).reshape(1)
    has = counts > 0
    succ = jnp.min(jnp.where((experts[None, :] > experts[:, None]) & has[None, :], experts[None, :], N_EXPERTS),
                   axis=1)
    succ = jnp.where(succ < N_EXPERTS, succ, -1)
    before = jnp.sum((has[None, :] & (experts[None, :] < blk_e[:, None])).astype(jnp.int32), axis=1)
    return (pos.astype(jnp.int32), row_tok.reshape(n_blocks, 1, rows), blk_e.astype(jnp.int32), n_used,
            succ[blk_e].astype(jnp.int32), (before % 2).astype(jnp.int32))


def _tile(n, pref):
    return pref if n % pref == 0 else n


def _layer(h, mem, p, l):
    bsz, seq, d = h.shape
    t = bsz * seq
    x2 = h.reshape(t, d)
    row = lambda v: v.reshape(1, -1)
    tm = _tile(t, 1024)

    w_in = p['w_in'][l]
    w_big = jnp.concatenate([w_in[:, 5168:9264], w_in[:, 0:3072], w_in[:, 3104:5152]], axis=1).astype(BF16)
    w_small = jnp.concatenate([w_in[:, 3072:3104], w_in[:, 5152:5168],
                               jnp.zeros((d, LANES - 48), F32)], axis=1).astype(BF16)
    big, small = _in_proj(x2, row(p['norm_mix'][l]), w_big, w_small, tm, 2304)
    big3 = big.reshape(bsz, seq, N_BIG)
    sm3 = small.reshape(bsz, seq, LANES)

    def lr_pad(w, off):
        return jnp.zeros((LANES, GLA_K), F32).at[off:off + GLA_RANK].set(w).astype(BF16)

    o_f = _gla_scan(big3, sm3, lr_pad(p['gla_w_lr_f'][l], SM_LRF), row(p['gla_b_lr_f'][l]), rev=False)
    y_a = _gla_scan(big3, sm3, lr_pad(p['gla_w_lr_b'][l], SM_LRB), row(p['gla_b_lr_b'][l]), rev=True,
                    prev=o_f, gn=row(p['gla_norm'][l]))

    cw = jnp.zeros((8, MLSTM_W), F32).at[:CONV_WIDTH].set(p['conv_w'][l].reshape(CONV_WIDTH, MLSTM_W))
    qm, km, vm = _mlstm_pre(big3, cw, row(p['conv_b'][l]), p['m_wq'][l].astype(BF16),
                            p['m_wk'][l].astype(BF16), p['m_wv'][l].astype(BF16), _tile(seq, 4096))
    gbias = jnp.zeros((1, LANES), F32).at[0, SM_GATES:SM_GATES + 4 * MLSTM_HEADS].set(
        p['m_gate_bias'][l].reshape(-1))
    h_f = _mlstm_scan(qm, km, vm, sm3, gbias, rev=False)
    y_b = _mlstm_scan(qm, km, vm, sm3, gbias, rev=True, prev=h_f, big3=big3, mn=row(p['m_norm'][l]))

    merged = _merge(y_a.reshape(t, GLA_V), y_b.reshape(t, MLSTM_W), p['w_branch_a'][l].astype(BF16),
                    p['w_branch_b'][l].astype(BF16), big, tm, 2048)
    h1 = _matmul_res(merged, p['w_mix_out'][l].astype(BF16), x2, tm, 1024)

    n_mem = mem.shape[1]
    kv = _kv_proj(mem.reshape(bsz * n_mem, d), row(p['norm_mem'][l]), p['w_xkv'][l].astype(BF16), 1024)
    h2 = _xattn(h1, row(p['norm_xattn'][l]), p['w_xq'][l].astype(BF16), kv.reshape(bsz, n_mem, 2 * d),
                p['w_xo'][l].astype(BF16), seq, _tile(seq, 1024))

    wr = jnp.concatenate([p['w_group'][l], p['w_router'][l].transpose(1, 0, 2).reshape(d, N_EXPERTS),
                          jnp.zeros((d, LANES - N_GROUPS - N_EXPERTS), F32)], axis=1)
    wr_hi = wr.astype(BF16)
    wr2 = jnp.stack([wr_hi, (wr - wr_hi.astype(F32)).astype(BF16)])
    br = jnp.concatenate([p['b_group'][l], p['b_router'][l].reshape(-1),
                          jnp.zeros((LANES - N_GROUPS - N_EXPERTS,), F32)]).reshape(1, LANES)
    xn3, ids, wts, cnt = _router(h2, row(p['norm_ffn'][l]), wr2, br, tm)
    pos, row_tok3, blk_e, n_used, next_e, wslot = _dispatch_plan(ids, cnt[0, :N_EXPERTS], MOE_ROWS)
    y_rows = _experts(blk_e, n_used, next_e, wslot, row_tok3, xn3, p['w_gate'][l], p['w_up'][l], p['w_down'][l])
    tc = _tile(t, COMBINE_TOK)
    return pos.reshape(t // tc, 1, 2 * tc), y_rows, wts, h2, tc


def kernel(x, mem, norm_mix, w_in, gla_w_lr_f, gla_b_lr_f, gla_w_lr_b, gla_b_lr_b, gla_norm, conv_w, conv_b, m_wq, m_wk, m_wv, m_gate_bias, m_norm, w_branch_a, w_branch_b, w_mix_out, norm_xattn, norm_mem, w_xq, w_xkv, w_xo, norm_ffn, w_group, b_group, w_router, b_router, w_gate, w_up, w_down, norm_final):
    p = dict(norm_mix=norm_mix, w_in=w_in, gla_w_lr_f=gla_w_lr_f, gla_b_lr_f=gla_b_lr_f, gla_w_lr_b=gla_w_lr_b,
             gla_b_lr_b=gla_b_lr_b, gla_norm=gla_norm, conv_w=conv_w, conv_b=conv_b, m_wq=m_wq, m_wk=m_wk,
             m_wv=m_wv, m_gate_bias=m_gate_bias, m_norm=m_norm, w_branch_a=w_branch_a, w_branch_b=w_branch_b,
             w_mix_out=w_mix_out, norm_xattn=norm_xattn, norm_mem=norm_mem, w_xq=w_xq, w_xkv=w_xkv, w_xo=w_xo,
             norm_ffn=norm_ffn, w_group=w_group, b_group=b_group, w_router=w_router, b_router=b_router,
             w_gate=w_gate, w_up=w_up, w_down=w_down)
    bsz, seq, d = x.shape
    depth = norm_mix.shape[0]
    assert depth == 1, "the final norm is fused into the last layer's combine step"
    pos3, y_rows, wts, h2, tc = _layer(x, mem, p, 0)
    out = _combine(pos3, y_rows, wts, h2, norm_final.reshape(1, d), tc)
    return out.reshape(bsz, seq, d)
```

```python
import functools

import jax
import jax.numpy as jnp
from jax import lax
from jax.experimental import pallas as pl
from jax.experimental.pallas import tpu as pltpu

F32 = jnp.float32
BF16 = jnp.bfloat16

EPS = 1e-6
LOG2_E = 1.4426950408889634
D_MODEL = 2048

GLA_HEADS = 4
GLA_DK = 128
GLA_DV = 256
GLA_K = GLA_HEADS * GLA_DK
GLA_V = GLA_HEADS * GLA_DV
GLA_RANK = 16
GLA_TAU = 16.0
GLA_CHUNK = 64
GLA_SUB = 16
GLA_SEQS = 4
GLA_SAFE_LOG2 = 60.0

MLSTM_HEADS = 4
MLSTM_DH = 256
MLSTM_W = MLSTM_HEADS * MLSTM_DH
CONV_WIDTH = 5
MLSTM_CHUNK = 256
MLSTM_SEQS = 1
CONV_HALO = 16

XATTN_HEADS = 4
XATTN_DH = D_MODEL // XATTN_HEADS

N_GROUPS = 4
EXPERTS_PER_GROUP = 8
N_EXPERTS = N_GROUPS * EXPERTS_PER_GROUP
D_EXPERT = 512
MOE_ROWS = 256
COMBINE_TOK = 256
ROUTE_COLS = 8

LANES = 128
SLAB_W = LANES
SLAB = D_MODEL // SLAB_W
SLAB_PITCH = SLAB + 8
SLAB_DTYPE = F32
GATHER_SLOTS = 3
GATHER_BATCHES = 8

OFF_GA, OFF_GB = 0, 2048
OFF_Q, OFF_K, OFF_V, OFF_GG = 4096, 4608, 5120, 6144
OFF_MX, OFF_MZ = 7168, 8192
N_BIG = 9216
SM_LRF, SM_LRB, SM_GATES = 0, 16, 32

VMEM_LIMIT = 56 * 1024 * 1024


def _cparams(sem):
    return pltpu.CompilerParams(dimension_semantics=sem, vmem_limit_bytes=VMEM_LIMIT)


def _rms(x, g):
    return x * lax.rsqrt(jnp.mean(x * x, axis=-1, keepdims=True) + EPS) * g


def _log_sigmoid(x):
    return jnp.minimum(x, 0.0) - jnp.log(1.0 + jnp.exp(-jnp.abs(x)))


def _dot(a, b):
    return jnp.dot(a, b, preferred_element_type=F32)


def _dot_nt(a, b):
    return lax.dot_general(a, b, (((1,), (1,)), ((), ())), preferred_element_type=F32)


def _dot_tn(a, b):
    return lax.dot_general(a, b, (((0,), (0,)), ((), ())), preferred_element_type=F32)


def _order_mask(c, rev):
    t = lax.broadcasted_iota(jnp.int32, (c, c), 0)
    s = lax.broadcasted_iota(jnp.int32, (c, c), 1)
    return (s >= t) if rev else (s <= t)


def _split3(x):
    hi = x.astype(BF16)
    r1 = x - hi.astype(F32)
    mid = r1.astype(BF16)
    return hi, mid, (r1 - mid.astype(F32)).astype(BF16)


def _cumsum_mm(mask_bf16, x):
    hi, mid, lo = _split3(x)
    return _dot(mask_bf16, hi) + _dot(mask_bf16, mid) + _dot(mask_bf16, lo)


def _in_proj_kernel(x_ref, g_ref, wbig_ref, wsm_ref, big_ref, sm_ref, xn_ref):
    @pl.when(pl.program_id(1) == 0)
    def _():
        xn = _rms(x_ref[...], g_ref[...]).astype(BF16)
        xn_ref[...] = xn
        sm_ref[...] = _dot(xn, wsm_ref[...])
    big_ref[...] = _dot(xn_ref[...], wbig_ref[...]).astype(BF16)


def _in_proj(x2, g, w_big, w_small, tm, tn):
    t, d = x2.shape
    n = w_big.shape[1]
    return pl.pallas_call(
        _in_proj_kernel,
        grid=(t // tm, n // tn),
        in_specs=[
            pl.BlockSpec((tm, d), lambda i, j: (i, 0)),
            pl.BlockSpec((1, d), lambda i, j: (0, 0)),
            pl.BlockSpec((d, tn), lambda i, j: (0, j)),
            pl.BlockSpec((d, LANES), lambda i, j: (0, 0)),
        ],
        out_specs=[
            pl.BlockSpec((tm, tn), lambda i, j: (i, j)),
            pl.BlockSpec((tm, LANES), lambda i, j: (i, 0)),
        ],
        out_shape=[jax.ShapeDtypeStruct((t, n), BF16), jax.ShapeDtypeStruct((t, LANES), F32)],
        scratch_shapes=[pltpu.VMEM((tm, d), BF16)],
        compiler_params=_cparams(("parallel", "arbitrary")),
        name="in_proj",
    )(x2, g, w_big, w_small)


def _gla_scores_any_decay(bh, qh, kh, kh_b, rev, c):
    sb = GLA_SUB
    n_sub = c // sb
    col = lax.broadcasted_iota(jnp.int32, (sb, c), 1)
    trow = lax.broadcasted_iota(jnp.int32, (sb, c), 0)
    slabs = [(qh[i * sb:(i + 1) * sb]
              * jnp.exp2(bh[i * sb:(i + 1) * sb] - bh[i * sb + s:i * sb + s + 1, :])).astype(BF16)
             for i in range(n_sub) for s in range(sb)]
    g = _dot_nt(jnp.concatenate(slabs, axis=0), kh_b)
    rows = []
    for i in range(n_sub):
        r0 = i * sb
        ref_row = r0 + (sb - 1 if rev else 0)
        beta = bh[ref_row:ref_row + 1, :]
        qt = (qh[r0:r0 + sb] * jnp.exp2(bh[r0:r0 + sb] - beta)).astype(BF16)
        kt = (kh * jnp.exp2(beta - bh)).astype(BF16)
        a_off = _dot_nt(qt, kt)
        a_diag = jnp.zeros((sb, c), F32)
        for s in range(sb):
            g0 = (i * sb + s) * sb
            a_diag = jnp.where(col == r0 + s, g[g0:g0 + sb], a_diag)
        tr = trow + r0
        if rev:
            off_mask = col >= r0 + sb
            diag_mask = (col >= tr) & (col < r0 + sb)
        else:
            off_mask = col < r0
            diag_mask = (col <= tr) & (col >= r0)
        rows.append(jnp.where(off_mask, a_off, 0.0) + jnp.where(diag_mask, a_diag, 0.0))
    return jnp.concatenate(rows, axis=0).astype(BF16)


def _gla_kernel(q_ref, k_ref, v_ref, sm_ref, wlr_ref, blr_ref, *rest, rev, final, c, nb):
    if final:
        oprev_ref, gg_ref, gn_ref, o_ref, st_ref, a_ref = rest
    else:
        o_ref, st_ref, a_ref = rest

    @pl.when(pl.program_id(1) == 0)
    def _():
        st_ref[...] = jnp.zeros_like(st_ref)

    mask = _order_mask(c, rev)
    mask_b = jnp.where(mask, 1.0, 0.0).astype(BF16)
    last = 0 if rev else c - 1

    for bi in range(nb):
        x = _dot(sm_ref[bi].astype(BF16), wlr_ref[...]) + blr_ref[...]
        la = _log_sigmoid(x) * (1.0 / GLA_TAU)
        b = _cumsum_mm(mask_b, la) * LOG2_E
        tot = b[last:last + 1, :]
        q = q_ref[bi].astype(F32) * (GLA_DK ** -0.5)
        k = k_ref[bi].astype(F32)
        q_in = (q * jnp.exp2(b)).astype(BF16)
        k_dec = (k * jnp.exp2(tot - b)).astype(BF16)
        e_tot = jnp.exp2(tot)

        mild = jnp.min(tot) > -GLA_SAFE_LOG2

        @pl.when(mild)
        def _():
            q_up = (q * jnp.exp2(b - tot)).astype(BF16)
            for h in range(GLA_HEADS):
                ks = slice(h * GLA_DK, (h + 1) * GLA_DK)
                a_ref[bi, h] = jnp.where(mask, _dot_nt(q_up[:, ks], k_dec[:, ks]), 0.0).astype(BF16)

        @pl.when(jnp.logical_not(mild))
        def _():
            for h in range(GLA_HEADS):
                ks = slice(h * GLA_DK, (h + 1) * GLA_DK)
                a_ref[bi, h] = _gla_scores_any_decay(b[:, ks], q[:, ks], k[:, ks], k_ref[bi, :, ks], rev, c)

        for h in range(GLA_HEADS):
            ks = slice(h * GLA_DK, (h + 1) * GLA_DK)
            vs = slice(h * GLA_DV, (h + 1) * GLA_DV)
            vh = v_ref[bi, :, vs]
            st = st_ref[bi, h]
            o_inter = _dot_nt(q_in[:, ks], st.astype(BF16))
            st_ref[bi, h] = st * e_tot[:, ks] + _dot_tn(vh, k_dec[:, ks])
            o = o_inter + _dot(a_ref[bi, h], vh)
            if final:
                o = o + oprev_ref[bi, :, vs]
                y = _rms(o, gn_ref[:, vs])
                gg = gg_ref[bi, :, vs].astype(F32)
                o_ref[bi, :, vs] = (y * (gg * jax.nn.sigmoid(gg))).astype(o_ref.dtype)
            else:
                o_ref[bi, :, vs] = o.astype(o_ref.dtype)


def _gla_scan(big3, sm3, wlr, blr, rev, prev=None, gn=None):
    bsz, seq, _ = big3.shape
    c = GLA_CHUNK
    n = seq // c
    nb = GLA_SEQS if bsz % GLA_SEQS == 0 else 1
    final = prev is not None

    def cm(ci):
        return (n - 1 - ci) if rev else ci

    in_specs = [
        pl.BlockSpec((nb, c, GLA_K), lambda b, ci: (b, cm(ci), OFF_Q // GLA_K)),
        pl.BlockSpec((nb, c, GLA_K), lambda b, ci: (b, cm(ci), OFF_K // GLA_K)),
        pl.BlockSpec((nb, c, GLA_V), lambda b, ci: (b, cm(ci), OFF_V // GLA_V)),
        pl.BlockSpec((nb, c, LANES), lambda b, ci: (b, cm(ci), 0)),
        pl.BlockSpec((LANES, GLA_K), lambda b, ci: (0, 0)),
        pl.BlockSpec((1, GLA_K), lambda b, ci: (0, 0)),
    ]
    args = [big3, big3, big3, sm3, wlr, blr]
    if final:
        in_specs += [
            pl.BlockSpec((nb, c, GLA_V), lambda b, ci: (b, cm(ci), 0)),
            pl.BlockSpec((nb, c, GLA_V), lambda b, ci: (b, cm(ci), OFF_GG // GLA_V)),
            pl.BlockSpec((1, GLA_V), lambda b, ci: (0, 0)),
        ]
        args += [prev, big3, gn]
    return pl.pallas_call(
        functools.partial(_gla_kernel, rev=rev, final=final, c=c, nb=nb),
        grid=(bsz // nb, n),
        in_specs=in_specs,
        out_specs=pl.BlockSpec((nb, c, GLA_V), lambda b, ci: (b, cm(ci), 0)),
        out_shape=jax.ShapeDtypeStruct((bsz, seq, GLA_V), BF16 if final else F32),
        scratch_shapes=[pltpu.VMEM((nb, GLA_HEADS, GLA_DV, GLA_DK), F32),
                        pltpu.VMEM((nb, GLA_HEADS, c, c), BF16)],
        compiler_params=_cparams(("parallel", "arbitrary")),
        name="gla_bwd" if rev else "gla_fwd",
    )(*args)


def _mlstm_pre_kernel(cur_ref, prev_ref, next_ref, cw_ref, cb_ref, wq_ref, wk_ref, wv_ref,
                      q_ref, k_ref, v_ref, *, tm):
    i = pl.program_id(2)
    cur_b = cur_ref[0]
    cur = cur_b.astype(F32)
    halo = CONV_WIDTH // 2
    prev = jnp.where(i > 0, prev_ref[0].astype(F32), 0.0)
    nxt = jnp.where(i < pl.num_programs(2) - 1, next_ref[0].astype(F32), 0.0)
    ext = jnp.concatenate([prev[CONV_HALO - 8:], cur, nxt[:8]], axis=0)
    acc = jnp.zeros_like(cur) + cb_ref[...]
    for w in range(CONV_WIDTH):
        off = 8 - halo + w
        acc = acc + ext[off:off + tm] * cw_ref[w:w + 1, :]
    xc = (acc * jax.nn.sigmoid(acc)).astype(BF16)
    q_ref[0] = _dot(xc, wq_ref[0]).astype(BF16)
    k_ref[0] = (_dot(xc, wk_ref[0]) * (MLSTM_DH ** -0.5)).astype(BF16)
    v_ref[0] = _dot(cur_b, wv_ref[0]).astype(BF16)


def _mlstm_pre(big3, cw, cb, wq, wk, wv, tm):
    bsz, seq, _ = big3.shape
    dh = MLSTM_DH
    nh = tm // CONV_HALO
    n_halo = seq // CONV_HALO
    c0 = OFF_MX // dh
    out = jax.ShapeDtypeStruct((bsz, seq, MLSTM_W), BF16)
    ospec = pl.BlockSpec((1, tm, dh), lambda b, h, i: (b, i, h))
    wspec = pl.BlockSpec((1, dh, dh), lambda b, h, i: (h, 0, 0))
    return pl.pallas_call(
        functools.partial(_mlstm_pre_kernel, tm=tm),
        grid=(bsz, MLSTM_HEADS, seq // tm),
        in_specs=[
            pl.BlockSpec((1, tm, dh), lambda b, h, i: (b, i, c0 + h)),
            pl.BlockSpec((1, CONV_HALO, dh), lambda b, h, i: (b, jnp.maximum(i * nh - 1, 0), c0 + h)),
            pl.BlockSpec((1, CONV_HALO, dh), lambda b, h, i: (b, jnp.minimum((i + 1) * nh, n_halo - 1), c0 + h)),
            pl.BlockSpec((8, dh), lambda b, h, i: (0, h)),
            pl.BlockSpec((1, dh), lambda b, h, i: (0, h)),
            wspec, wspec, wspec,
        ],
        out_specs=[ospec, ospec, ospec],
        out_shape=[out, out, out],
        compiler_params=_cparams(("parallel", "parallel", "parallel")),
        name="mlstm_pre",
    )(big3, big3, big3, cw, cb, wq, wk, wv)


def _mlstm_kernel(q_ref, k_ref, v_ref, sm_ref, gb_ref, *rest, rev, final, c, nb):
    if final:
        hprev_ref, mz_ref, mn_ref, o_ref, c_ref, n_ref, m_ref = rest
    else:
        o_ref, c_ref, n_ref, m_ref = rest

    @pl.when(pl.program_id(1) == 0)
    def _():
        c_ref[...] = jnp.zeros_like(c_ref)
        n_ref[...] = jnp.zeros_like(n_ref)
        m_ref[...] = jnp.zeros_like(m_ref)

    mask = _order_mask(c, rev)
    mask_b = jnp.where(mask, 1.0, 0.0).astype(BF16)
    last = 0 if rev else c - 1
    dh = MLSTM_DH

    for bi in range(nb):
        g = sm_ref[bi] + gb_ref[...]
        bcum = _cumsum_mm(mask_b, _log_sigmoid(g))
        g_t = g.T
        b_t = bcum.T

        for h in range(MLSTM_HEADS):
            ci = SM_GATES + (2 * MLSTM_HEADS if rev else 0) + h
            cf = ci + MLSTM_HEADS
            hs = slice(h * dh, (h + 1) * dh)
            i_col, b_col = g[:, ci:ci + 1], bcum[:, cf:cf + 1]
            i_row, b_row = g_t[ci:ci + 1, :], b_t[cf:cf + 1, :]
            tot = b_col[last:last + 1, :]
            m_prev = m_ref[bi, h, 0:1, 0:1]
            qh, kh, vh = q_ref[bi, :, hs], k_ref[bi, :, hs], v_ref[bi, :, hs]
            c_st = c_ref[bi, h]
            n_st = n_ref[bi, h, 0:1, :]

            dmat = jnp.where(mask, b_col - b_row + i_row, -jnp.inf)
            inter_log = b_col + m_prev
            m_t = jnp.maximum(inter_log, jnp.max(dmat, axis=-1, keepdims=True))
            w_intra = jnp.exp(dmat - m_t)
            w_inter = jnp.exp(inter_log - m_t)
            s = _dot_nt(qh, kh) * w_intra
            num = w_inter * _dot(qh, c_st.astype(BF16)) + _dot(s.astype(BF16), vh)
            den = (w_inter * jnp.sum(qh.astype(F32) * n_st, axis=-1, keepdims=True)
                   + jnp.sum(s, axis=-1, keepdims=True))
            hh = num / jnp.maximum(jnp.abs(den), jnp.exp(-m_t))

            upd_col = tot - b_col + i_col
            upd_row = tot - b_row + i_row
            m_new = jnp.maximum(tot + m_prev, jnp.max(upd_row, axis=-1, keepdims=True))
            w_old = jnp.exp(tot + m_prev - m_new)
            kw = kh.astype(F32) * jnp.exp(upd_col - m_new)
            c_ref[bi, h] = w_old * c_st + _dot_tn(kw.astype(BF16), vh)
            n_ref[bi, h] = jnp.broadcast_to(w_old * n_st + jnp.sum(kw, axis=0, keepdims=True), (8, dh))
            m_ref[bi, h] = jnp.broadcast_to(m_new, (8, LANES))

            if final:
                hh = hh + hprev_ref[bi, :, hs]
                y = _rms(hh, mn_ref[:, hs])
                o_ref[bi, :, hs] = (y * jax.nn.sigmoid(mz_ref[bi, :, hs].astype(F32))).astype(o_ref.dtype)
            else:
                o_ref[bi, :, hs] = hh.astype(o_ref.dtype)


def _mlstm_scan(qm, km, vm, sm3, gbias, rev, prev=None, big3=None, mn=None):
    bsz, seq, w = qm.shape
    c = min(MLSTM_CHUNK, seq)
    n = seq // c
    nb = MLSTM_SEQS if bsz % MLSTM_SEQS == 0 else 1
    final = prev is not None

    def cm(ci):
        return (n - 1 - ci) if rev else ci

    xspec = pl.BlockSpec((nb, c, w), lambda b, ci: (b, cm(ci), 0))
    in_specs = [xspec, xspec, xspec,
                pl.BlockSpec((nb, c, LANES), lambda b, ci: (b, cm(ci), 0)),
                pl.BlockSpec((1, LANES), lambda b, ci: (0, 0))]
    args = [qm, km, vm, sm3, gbias]
    if final:
        in_specs += [xspec,
                     pl.BlockSpec((nb, c, w), lambda b, ci: (b, cm(ci), OFF_MZ // MLSTM_W)),
                     pl.BlockSpec((1, w), lambda b, ci: (0, 0))]
        args += [prev, big3, mn]
    return pl.pallas_call(
        functools.partial(_mlstm_kernel, rev=rev, final=final, c=c, nb=nb),
        grid=(bsz // nb, n),
        in_specs=in_specs,
        out_specs=xspec,
        out_shape=jax.ShapeDtypeStruct((bsz, seq, w), BF16 if final else F32),
        scratch_shapes=[pltpu.VMEM((nb, MLSTM_HEADS, MLSTM_DH, MLSTM_DH), F32),
                        pltpu.VMEM((nb, MLSTM_HEADS, 8, MLSTM_DH), F32),
                        pltpu.VMEM((nb, MLSTM_HEADS, 8, LANES), F32)],
        compiler_params=_cparams(("parallel", "arbitrary")),
        name="mlstm_bwd" if rev else "mlstm_fwd",
    )(*args)


def _merge_kernel(ya_ref, yb_ref, wa_ref, wb_ref, ga_ref, gb_ref, o_ref):
    a = _dot(ya_ref[...], wa_ref[...])
    b = _dot(yb_ref[...], wb_ref[...])
    ga = jax.nn.sigmoid(ga_ref[...].astype(F32))
    gb = jax.nn.sigmoid(gb_ref[...].astype(F32))
    o_ref[...] = (ga * a + gb * b).astype(o_ref.dtype)


def _merge(ya, yb, wa, wb, big, tm, tn):
    t, kdim = ya.shape
    n = wa.shape[1]
    assert OFF_GA % tn == 0 and OFF_GB % tn == 0
    return pl.pallas_call(
        _merge_kernel,
        grid=(t // tm, n // tn),
        in_specs=[
            pl.BlockSpec((tm, kdim), lambda i, j: (i, 0)),
            pl.BlockSpec((tm, kdim), lambda i, j: (i, 0)),
            pl.BlockSpec((kdim, tn), lambda i, j: (0, j)),
            pl.BlockSpec((kdim, tn), lambda i, j: (0, j)),
            pl.BlockSpec((tm, tn), lambda i, j: (i, OFF_GA // tn + j)),
            pl.BlockSpec((tm, tn), lambda i, j: (i, OFF_GB // tn + j)),
        ],
        out_specs=pl.BlockSpec((tm, tn), lambda i, j: (i, j)),
        out_shape=jax.ShapeDtypeStruct((t, n), BF16),
        compiler_params=_cparams(("parallel", "parallel")),
        name="merge",
    )(ya, yb, wa, wb, big, big)


def _merge_mix_kernel(ya_ref, yb_ref, wa_ref, wb_ref, ga_ref, gb_ref, wo_ref, x_ref, o_ref):
    a = _dot(ya_ref[...], wa_ref[...])
    b = _dot(yb_ref[...], wb_ref[...])
    ga = jax.nn.sigmoid(ga_ref[...].astype(F32))
    gb = jax.nn.sigmoid(gb_ref[...].astype(F32))
    merged = (ga * a + gb * b).astype(BF16)
    o_ref[...] = x_ref[...] + _dot(merged, wo_ref[...])


def _merge_mix(ya, yb, wa, wb, big, wo, x2, tm):
    t, kdim = ya.shape
    n = wa.shape[1]

    def resident(shape):
        return pl.BlockSpec(shape, lambda i: (0, 0), pipeline_mode=pl.Buffered(1))

    return pl.pallas_call(
        _merge_mix_kernel,
        grid=(t // tm,),
        in_specs=[
            pl.BlockSpec((tm, kdim), lambda i: (i, 0)),
            pl.BlockSpec((tm, kdim), lambda i: (i, 0)),
            resident((kdim, n)),
            resident((kdim, n)),
            pl.BlockSpec((tm, n), lambda i: (i, OFF_GA // n)),
            pl.BlockSpec((tm, n), lambda i: (i, OFF_GB // n)),
            resident((n, n)),
            pl.BlockSpec((tm, n), lambda i: (i, 0)),
        ],
        out_specs=pl.BlockSpec((tm, n), lambda i: (i, 0)),
        out_shape=jax.ShapeDtypeStruct((t, n), F32),
        compiler_params=_cparams(("parallel",)),
        name="merge_mix",
    )(ya, yb, wa, wb, big, big, wo, x2)


def _matmul_res_kernel(a_ref, w_ref, r_ref, o_ref):
    o_ref[...] = r_ref[...] + _dot(a_ref[...], w_ref[...])


def _matmul_res(a, w, res, tm, tn):
    t, kdim = a.shape
    n = w.shape[1]
    return pl.pallas_call(
        _matmul_res_kernel,
        grid=(t // tm, n // tn),
        in_specs=[
            pl.BlockSpec((tm, kdim), lambda i, j: (i, 0)),
            pl.BlockSpec((kdim, tn), lambda i, j: (0, j)),
            pl.BlockSpec((tm, tn), lambda i, j: (i, j)),
        ],
        out_specs=pl.BlockSpec((tm, tn), lambda i, j: (i, j)),
        out_shape=jax.ShapeDtypeStruct((t, n), F32),
        compiler_params=_cparams(("parallel", "parallel")),
        name="mix_out",
    )(a, w, res)


def _kv_kernel(mem_ref, g_ref, w_ref, o_ref):
    mn = _rms(mem_ref[...], g_ref[...]).astype(BF16)
    o_ref[...] = _dot(mn, w_ref[...]).astype(BF16)


def _kv_proj(mem2, g, w, tn):
    r, d = mem2.shape
    n = w.shape[1]
    return pl.pallas_call(
        _kv_kernel,
        grid=(n // tn,),
        in_specs=[
            pl.BlockSpec((r, d), lambda j: (0, 0)),
            pl.BlockSpec((1, d), lambda j: (0, 0)),
            pl.BlockSpec((d, tn), lambda j: (0, j)),
        ],
        out_specs=pl.BlockSpec((r, tn), lambda j: (0, j)),
        out_shape=jax.ShapeDtypeStruct((r, n), BF16),
        compiler_params=_cparams(("parallel",)),
        name="kv_proj",
    )(mem2, g, w)


def _xattn_kernel(h_ref, g_ref, wq_ref, k_ref, v_ref, wo_ref, o_ref, xn_ref):
    @pl.when(pl.program_id(1) == 0)
    def _():
        x = h_ref[...]
        xn_ref[...] = _rms(x, g_ref[...]).astype(BF16)
        o_ref[...] = x
    q = _dot(xn_ref[...], wq_ref[...]).astype(BF16)
    s = _dot_nt(q, k_ref[0]) * (XATTN_DH ** -0.5)
    p = jnp.exp(s - jnp.max(s, axis=-1, keepdims=True))
    p = p / jnp.sum(p, axis=-1, keepdims=True)
    o = _dot(p.astype(BF16), v_ref[0]).astype(BF16)
    o_ref[...] += _dot(o, wo_ref[...])


def _xattn(h1, g, wq, kv3, wo, seq, tm):
    t, d = h1.shape
    n_mem = kv3.shape[1]
    dh = XATTN_DH
    per_b = seq // tm
    return pl.pallas_call(
        _xattn_kernel,
        grid=(t // tm, XATTN_HEADS),
        in_specs=[
            pl.BlockSpec((tm, d), lambda i, h: (i, 0)),
            pl.BlockSpec((1, d), lambda i, h: (0, 0)),
            pl.BlockSpec((d, dh), lambda i, h: (0, h)),
            pl.BlockSpec((1, n_mem, dh), lambda i, h: (i // per_b, 0, h)),
            pl.BlockSpec((1, n_mem, dh), lambda i, h: (i // per_b, 0, XATTN_HEADS + h)),
            pl.BlockSpec((dh, d), lambda i, h: (h, 0)),
        ],
        out_specs=pl.BlockSpec((tm, d), lambda i, h: (i, 0)),
        out_shape=jax.ShapeDtypeStruct((t, d), F32),
        scratch_shapes=[pltpu.VMEM((tm, d), BF16)],
        compiler_params=_cparams(("parallel", "arbitrary")),
        name="xattn",
    )(h1, g, wq, kv3, kv3, wo)


def _slab_store(ref, val):
    n = val.shape[0]
    for c in range(SLAB):
        ref[pl.ds(c, n, stride=SLAB), :] = val[:, c * SLAB_W:(c + 1) * SLAB_W].astype(ref.dtype)


def _slab_load(buf, slot, first, n):
    return jnp.concatenate(
        [buf[slot, pl.ds(first * SLAB_PITCH + c, n, stride=SLAB_PITCH), :] for c in range(SLAB)], axis=1)


def _slab_copy(src_hbm, dst_buf, sem, slot, src_slab, dst_slab):
    return pltpu.make_async_copy(src_hbm.at[pl.ds(src_slab * SLAB, SLAB)],
                                 dst_buf.at[slot, pl.ds(dst_slab * SLAB_PITCH, SLAB)], sem.at[slot])


def _zero_after(v):
    bits = lax.bitcast_convert_type(v[-1:, -1:], jnp.uint32)
    return ((bits >> 16) >> 16).astype(jnp.int32)[0, 0]


def _slab_wait(src_hbm, dst_buf, sem, slot, n):
    pltpu.make_async_copy(src_hbm.at[pl.ds(0, n * SLAB)], dst_buf.at[slot, pl.ds(0, n * SLAB)], sem.at[slot]).wait()


def _router_kernel(h_ref, g_ref, wr_ref, br_ref, xn_ref, ids_ref, wts_ref, cnt_ref, carry_ref):
    @pl.when(pl.program_id(0) == 0)
    def _():
        carry_ref[...] = jnp.zeros_like(carry_ref)

    xn = _rms(h_ref[...], g_ref[...])
    _slab_store(xn_ref, xn)
    hi = xn.astype(BF16)
    lo = (xn - hi.astype(F32)).astype(BF16)
    lg = _dot(hi, wr_ref[0]) + _dot(hi, wr_ref[1]) + _dot(lo, wr_ref[0]) + br_ref[...]
    lane = lax.broadcasted_iota(jnp.int32, lg.shape, 1).astype(F32)
    ninf = -jnp.inf
    big_lane = float(LANES)

    def first_max(v):
        mx = jnp.max(v, axis=-1, keepdims=True)
        return mx, jnp.min(jnp.where(v == mx, lane, big_lane), axis=-1, keepdims=True)

    gl = jnp.where(lane < N_GROUPS, lg, ninf)
    gmax, g_sel = first_max(gl)
    p_g = 1.0 / jnp.sum(jnp.exp(gl - gmax), axis=-1, keepdims=True)
    lo_l = N_GROUPS + EXPERTS_PER_GROUP * g_sel
    el = jnp.where((lane >= lo_l) & (lane < lo_l + EXPERTS_PER_GROUP), lg, ninf)
    emax, i1 = first_max(el)
    max2, i2 = first_max(jnp.where(lane == i1, ninf, el))
    e2 = jnp.exp(max2 - emax)
    w1 = p_g / (1.0 + e2)
    w2 = p_g * e2 / (1.0 + e2)
    e1, e2 = i1 - N_GROUPS, i2 - N_GROUPS
    oh1, oh2 = lane == e1, lane == e2
    oh = jnp.where(oh1, 1.0, 0.0) + jnp.where(oh2, 1.0, 0.0)
    tm = lg.shape[0]
    earlier = lax.broadcasted_iota(jnp.int32, (tm, tm), 1) < lax.broadcasted_iota(jnp.int32, (tm, tm), 0)
    carry = carry_ref[0:1, :]
    before = _dot(jnp.where(earlier, 1.0, 0.0).astype(BF16), oh.astype(BF16)) + carry
    r1 = jnp.sum(jnp.where(oh1, before, 0.0), axis=-1, keepdims=True)
    r2 = jnp.sum(jnp.where(oh2, before, 0.0), axis=-1, keepdims=True)
    counts = carry + jnp.sum(oh, axis=0, keepdims=True)
    carry_ref[...] = jnp.broadcast_to(counts, carry_ref.shape)
    cnt_ref[...] = jnp.broadcast_to(counts, cnt_ref.shape).astype(jnp.int32)
    ids = jnp.where(lane == 0.0, e1, jnp.where(lane == 1.0, e2, jnp.where(lane == 2.0, r1,
                                                                         jnp.where(lane == 3.0, r2, 0.0))))
    ids_ref[...] = ids.astype(jnp.int32)[:, 0:ROUTE_COLS]
    wts_ref[...] = jnp.where(lane == 0.0, w1, jnp.where(lane == 1.0, w2, 0.0))


def _router(h2, g, wr, br, tm):
    t, d = h2.shape
    return pl.pallas_call(
        _router_kernel,
        grid=(t // tm,),
        in_specs=[
            pl.BlockSpec((tm, d), lambda i: (i, 0)),
            pl.BlockSpec((1, d), lambda i: (0, 0)),
            pl.BlockSpec((2, d, LANES), lambda i: (0, 0, 0)),
            pl.BlockSpec((1, LANES), lambda i: (0, 0)),
        ],
        out_specs=[
            pl.BlockSpec((tm * SLAB, SLAB_W), lambda i: (i, 0)),
            pl.BlockSpec((tm, ROUTE_COLS), lambda i: (i, 0)),
            pl.BlockSpec((tm, LANES), lambda i: (i, 0)),
            pl.BlockSpec((8, LANES), lambda i: (0, 0)),
        ],
        out_shape=[jax.ShapeDtypeStruct((t * SLAB, SLAB_W), SLAB_DTYPE),
                   jax.ShapeDtypeStruct((t, ROUTE_COLS), jnp.int32),
                   jax.ShapeDtypeStruct((t, LANES), F32),
                   jax.ShapeDtypeStruct((8, LANES), jnp.int32)],
        scratch_shapes=[pltpu.VMEM((8, LANES), F32)],
        compiler_params=_cparams(("arbitrary",)),
        name="router",
    )(h2, g, wr, br)


def _experts_kernel(blk_e_ref, n_used_ref, next_e_ref, wslot_ref, tok0_ref, tok1_ref, tok2_ref, x_hbm,
                    wg_hbm, wu_hbm, wd_hbm, y_ref, xbuf, wg_f, wu_f, wd_f, wg_b, wu_b, wd_b, sem, wsem, *, rows):
    j = pl.program_id(0)
    n_used = n_used_ref[0]
    used = j < n_used
    slot = j % GATHER_SLOTS
    slot1 = (j + 1) % GATHER_SLOTS
    slot2 = (j + 2) % GATHER_SLOTS

    @pl.when(j == 0)
    def _():
        def start(r, carry):
            _slab_copy(x_hbm, xbuf, sem, 0, tok0_ref[0, 0, r], r).start()
            _slab_copy(x_hbm, xbuf, sem, 1, tok1_ref[0, 0, r], r).start()
            return carry
        lax.fori_loop(0, rows, start, 0)

    @pl.when(jnp.logical_not(used))
    def _():
        @pl.when(j == n_used)
        def _():
            _slab_wait(x_hbm, xbuf, sem, slot, rows)
            _slab_wait(x_hbm, xbuf, sem, slot1, rows)
        y_ref[...] = jnp.zeros_like(y_ref)

    def weight_copies(e, ws):
        return [pltpu.make_async_copy(src.at[e], dst.at[ws], wsem.at[ws])
                for src, dst in ((wg_hbm, wg_f), (wu_hbm, wu_f), (wd_hbm, wd_f))]

    @pl.when(j == 0)
    def _():
        for cp in weight_copies(blk_e_ref[0], 0):
            cp.start()

    @pl.when(used)
    def _():
        _slab_wait(x_hbm, xbuf, sem, slot, rows)

        @pl.when((j == 0) | (blk_e_ref[j] != blk_e_ref[jnp.maximum(j - 1, 0)]))
        def _():
            ws = wslot_ref[j]
            for cp in weight_copies(0, ws):
                cp.wait()

            @pl.when(next_e_ref[j] >= 0)
            def _():
                for cp in weight_copies(next_e_ref[j], 1 - ws):
                    cp.start()

            wg_b[...] = wg_f[ws].astype(BF16)
            wu_b[...] = wu_f[ws].astype(BF16)
            wd_b[...] = wd_f[ws].astype(BF16)

        batch = rows // GATHER_BATCHES
        issued = [0]

        def gather_batch(after):
            zero = 0 if after is None else _zero_after(after)
            for r in range(issued[0] * batch, (issued[0] + 1) * batch):
                _slab_copy(x_hbm, xbuf, sem, slot2, tok2_ref[0, 0, r] + zero, r).start()
            issued[0] += 1

        xb = _slab_load(xbuf, slot, 0, rows).astype(BF16)
        gather_batch(None)
        de = wg_b.shape[1]
        halves = [slice(0, de // 2), slice(de // 2, de)]
        gate, up = [], []
        for cs in halves:
            gate.append(_dot(xb, wg_b[:, cs]))
            gather_batch(gate[-1])
        for cs in halves:
            up.append(_dot(xb, wu_b[:, cs]))
            gather_batch(up[-1])
        gate = jnp.concatenate(gate, axis=1)
        act = (gate * jax.nn.sigmoid(gate) * jnp.concatenate(up, axis=1)).astype(BF16)
        wo = de
        per = wo // SLAB_W
        for c in range(wd_b.shape[1] // wo):
            yc = _dot(act, wd_b[:, c * wo:(c + 1) * wo])
            for cc in range(per):
                y_ref[pl.ds(c * per + cc, rows, stride=SLAB), :] = (
                    yc[:, cc * SLAB_W:(cc + 1) * SLAB_W].astype(y_ref.dtype))
            if issued[0] < GATHER_BATCHES:
                gather_batch(yc)
        assert issued[0] == GATHER_BATCHES

        @pl.when(j == pl.num_programs(0) - 1)
        def _():
            _slab_wait(x_hbm, xbuf, sem, slot1, rows)
            _slab_wait(x_hbm, xbuf, sem, slot2, rows)


def _experts(blk_e, n_used, next_e, wslot, row_tok3, xn_slab, wg, wu, wd):
    n_blocks, _, rows = row_tok3.shape
    _, d, de = wg.shape

    def tok_spec(ahead):
        return pl.BlockSpec((1, 1, rows), lambda j, *_: (jnp.minimum(j + ahead, n_blocks - 1), 0, 0),
                            memory_space=pltpu.SMEM)

    hbm = pl.BlockSpec(memory_space=pl.ANY)
    grid_spec = pltpu.PrefetchScalarGridSpec(
        num_scalar_prefetch=4,
        grid=(n_blocks,),
        in_specs=[tok_spec(0), tok_spec(1), tok_spec(2), hbm, hbm, hbm, hbm],
        out_specs=pl.BlockSpec((rows * SLAB, SLAB_W), lambda j, *_: (j, 0)),
        scratch_shapes=[pltpu.VMEM((GATHER_SLOTS, rows * SLAB_PITCH, SLAB_W), SLAB_DTYPE),
                        pltpu.VMEM((2, d, de), F32), pltpu.VMEM((2, d, de), F32), pltpu.VMEM((2, de, d), F32),
                        pltpu.VMEM((d, de), BF16), pltpu.VMEM((d, de), BF16), pltpu.VMEM((de, d), BF16),
                        pltpu.SemaphoreType.DMA((GATHER_SLOTS,)), pltpu.SemaphoreType.DMA((2,))],
    )
    return pl.pallas_call(
        functools.partial(_experts_kernel, rows=rows),
        grid_spec=grid_spec,
        out_shape=jax.ShapeDtypeStruct((n_blocks * rows * SLAB, SLAB_W), SLAB_DTYPE),
        compiler_params=_cparams(("arbitrary",)),
        name="experts",
    )(blk_e, n_used, next_e, wslot, row_tok3, row_tok3, row_tok3, xn_slab, wg, wu, wd)


def _combine_kernel(pos0_ref, pos1_ref, pos2_ref, y_hbm, wts_ref, h_ref, g_ref, o_ref, ybuf, sem, *, tc):
    j = pl.program_id(0)
    slot = j % GATHER_SLOTS
    slot1 = (j + 1) % GATHER_SLOTS
    slot2 = (j + 2) % GATHER_SLOTS

    @pl.when(j == 0)
    def _():
        def start(r, carry):
            for s, idx in ((0, pos0_ref), (1, pos1_ref)):
                _slab_copy(y_hbm, ybuf, sem, s, idx[0, 0, 2 * r], r).start()
                _slab_copy(y_hbm, ybuf, sem, s, idx[0, 0, 2 * r + 1], tc + r).start()
            return carry
        lax.fori_loop(0, tc, start, 0)

    _slab_wait(y_hbm, ybuf, sem, slot, 2 * tc)
    for r in range(tc):
        _slab_copy(y_hbm, ybuf, sem, slot2, pos2_ref[0, 0, 2 * r], r).start(priority=0)
        _slab_copy(y_hbm, ybuf, sem, slot2, pos2_ref[0, 0, 2 * r + 1], tc + r).start(priority=1)
    w = wts_ref[...]
    y = (_slab_load(ybuf, slot, 0, tc).astype(F32) * w[:, 0:1]
         + _slab_load(ybuf, slot, tc, tc).astype(F32) * w[:, 1:2])
    o_ref[...] = _rms(h_ref[...] + y, g_ref[...])

    @pl.when(j == pl.num_programs(0) - 1)
    def _():
        _slab_wait(y_hbm, ybuf, sem, slot1, 2 * tc)
        _slab_wait(y_hbm, ybuf, sem, slot2, 2 * tc)


def _combine(pos3, y_slab, wts, h2, g, tc):
    t, d = h2.shape
    n = t // tc

    def pos_spec(ahead):
        return pl.BlockSpec((1, 1, 2 * tc), lambda i: (jnp.minimum(i + ahead, n - 1), 0, 0),
                            memory_space=pltpu.SMEM)

    return pl.pallas_call(
        functools.partial(_combine_kernel, tc=tc),
        grid=(n,),
        in_specs=[
            pos_spec(0), pos_spec(1), pos_spec(2),
            pl.BlockSpec(memory_space=pl.ANY),
            pl.BlockSpec((tc, LANES), lambda i: (i, 0)),
            pl.BlockSpec((tc, d), lambda i: (i, 0)),
            pl.BlockSpec((1, d), lambda i: (0, 0)),
        ],
        out_specs=pl.BlockSpec((tc, d), lambda i: (i, 0)),
        out_shape=jax.ShapeDtypeStruct((t, d), F32),
        scratch_shapes=[pltpu.VMEM((GATHER_SLOTS, 2 * tc * SLAB_PITCH, SLAB_W), SLAB_DTYPE),
                        pltpu.SemaphoreType.DMA((GATHER_SLOTS,))],
        compiler_params=_cparams(("arbitrary",)),
        name="combine",
    )(pos3, pos3, pos3, y_slab, wts, h2, g)


def _dispatch_plan(ids, counts, rows):
    flat_e = ids[:, 0:2].reshape(-1)
    rank = ids[:, 2:4].reshape(-1)
    n_asg = flat_e.shape[0]
    experts = jnp.arange(N_EXPERTS, dtype=jnp.int32)
    padded = ((counts + rows - 1) // rows) * rows
    pends = jnp.cumsum(padded)
    pstarts = pends - padded
    pos = jnp.sum(jnp.where(flat_e[:, None] == experts[None, :], pstarts[None, :], 0), axis=1) + rank
    n_rows = n_asg + N_EXPERTS * rows
    n_blocks = n_rows // rows
    row_tok = jnp.zeros((n_rows,), jnp.int32).at[pos].set(jnp.arange(n_asg, dtype=jnp.int32) // 2)
    starts = jnp.arange(n_blocks, dtype=jnp.int32) * rows
    blk_e = jnp.minimum(jnp.sum((pends[None, :] <= starts[:, None]).astype(jnp.int32), axis=1), N_EXPERTS - 1)
    n_used = (pends[-1] // rows).astype(jnp.int32).reshape(1)
    has = counts > 0
    succ = jnp.min(jnp.where((experts[None, :] > experts[:, None]) & has[None, :], experts[None, :], N_EXPERTS),
                   axis=1)
    succ = jnp.where(succ < N_EXPERTS, succ, -1)
    before = jnp.sum((has[None, :] & (experts[None, :] < blk_e[:, None])).astype(jnp.int32), axis=1)
    return (pos.astype(jnp.int32), row_tok.reshape(n_blocks, 1, rows), blk_e.astype(jnp.int32), n_used,
            succ[blk_e].astype(jnp.int32), (before % 2).astype(jnp.int32))


def _tile(n, pref):
    return pref if n % pref == 0 else n


def _layer(h, mem, p, l):
    bsz, seq, d = h.shape
    t = bsz * seq
    x2 = h.reshape(t, d)
    row = lambda v: v.reshape(1, -1)
    tm = _tile(t, 1024)

    w_in = p['w_in'][l]
    w_big = jnp.concatenate([w_in[:, 5168:9264], w_in[:, 0:3072], w_in[:, 3104:5152]], axis=1).astype(BF16)
    w_small = jnp.concatenate([w_in[:, 3072:3104], w_in[:, 5152:5168],
                               jnp.zeros((d, LANES - 48), F32)], axis=1).astype(BF16)
    big, small = _in_proj(x2, row(p['norm_mix'][l]), w_big, w_small, tm, 2304)
    big3 = big.reshape(bsz, seq, N_BIG)
    sm3 = small.reshape(bsz, seq, LANES)

    def lr_pad(w, off):
        return jnp.zeros((LANES, GLA_K), F32).at[off:off + GLA_RANK].set(w).astype(BF16)

    o_f = _gla_scan(big3, sm3, lr_pad(p['gla_w_lr_f'][l], SM_LRF), row(p['gla_b_lr_f'][l]), rev=False)
    y_a = _gla_scan(big3, sm3, lr_pad(p['gla_w_lr_b'][l], SM_LRB), row(p['gla_b_lr_b'][l]), rev=True,
                    prev=o_f, gn=row(p['gla_norm'][l]))

    cw = jnp.zeros((8, MLSTM_W), F32).at[:CONV_WIDTH].set(p['conv_w'][l].reshape(CONV_WIDTH, MLSTM_W))
    qm, km, vm = _mlstm_pre(big3, cw, row(p['conv_b'][l]), p['m_wq'][l].astype(BF16),
                            p['m_wk'][l].astype(BF16), p['m_wv'][l].astype(BF16), _tile(seq, 4096))
    gbias = jnp.zeros((1, LANES), F32).at[0, SM_GATES:SM_GATES + 4 * MLSTM_HEADS].set(
        p['m_gate_bias'][l].reshape(-1))
    h_f = _mlstm_scan(qm, km, vm, sm3, gbias, rev=False)
    y_b = _mlstm_scan(qm, km, vm, sm3, gbias, rev=True, prev=h_f, big3=big3, mn=row(p['m_norm'][l]))

    assert OFF_GA % d == 0 and OFF_GB % d == 0
    h1 = _merge_mix(y_a.reshape(t, GLA_V), y_b.reshape(t, MLSTM_W), p['w_branch_a'][l].astype(BF16),
                    p['w_branch_b'][l].astype(BF16), big, p['w_mix_out'][l].astype(BF16), x2, _tile(t, 512))

    n_mem = mem.shape[1]
    kv = _kv_proj(mem.reshape(bsz * n_mem, d), row(p['norm_mem'][l]), p['w_xkv'][l].astype(BF16), 1024)
    h2 = _xattn(h1, row(p['norm_xattn'][l]), p['w_xq'][l].astype(BF16), kv.reshape(bsz, n_mem, 2 * d),
                p['w_xo'][l].astype(BF16), seq, _tile(seq, 1024))

    wr = jnp.concatenate([p['w_group'][l], p['w_router'][l].transpose(1, 0, 2).reshape(d, N_EXPERTS),
                          jnp.zeros((d, LANES - N_GROUPS - N_EXPERTS), F32)], axis=1)
    wr_hi = wr.astype(BF16)
    wr2 = jnp.stack([wr_hi, (wr - wr_hi.astype(F32)).astype(BF16)])
    br = jnp.concatenate([p['b_group'][l], p['b_router'][l].reshape(-1),
                          jnp.zeros((LANES - N_GROUPS - N_EXPERTS,), F32)]).reshape(1, LANES)
    xn3, ids, wts, cnt = _router(h2, row(p['norm_ffn'][l]), wr2, br, tm)
    pos, row_tok3, blk_e, n_used, next_e, wslot = _dispatch_plan(ids, cnt[0, :N_EXPERTS], MOE_ROWS)
    y_rows = _experts(blk_e, n_used, next_e, wslot, row_tok3, xn3, p['w_gate'][l], p['w_up'][l], p['w_down'][l])
    tc = _tile(t, COMBINE_TOK)
    return pos.reshape(t // tc, 1, 2 * tc), y_rows, wts, h2, tc


def kernel(x, mem, norm_mix, w_in, gla_w_lr_f, gla_b_lr_f, gla_w_lr_b, gla_b_lr_b, gla_norm, conv_w, conv_b, m_wq, m_wk, m_wv, m_gate_bias, m_norm, w_branch_a, w_branch_b, w_mix_out, norm_xattn, norm_mem, w_xq, w_xkv, w_xo, norm_ffn, w_group, b_group, w_router, b_router, w_gate, w_up, w_down, norm_final):
    p = dict(norm_mix=norm_mix, w_in=w_in, gla_w_lr_f=gla_w_lr_f, gla_b_lr_f=gla_b_lr_f, gla_w_lr_b=gla_w_lr_b,
             gla_b_lr_b=gla_b_lr_b, gla_norm=gla_norm, conv_w=conv_w, conv_b=conv_b, m_wq=m_wq, m_wk=m_wk,
             m_wv=m_wv, m_gate_bias=m_gate_bias, m_norm=m_norm, w_branch_a=w_branch_a, w_branch_b=w_branch_b,
             w_mix_out=w_mix_out, norm_xattn=norm_xattn, norm_mem=norm_mem, w_xq=w_xq, w_xkv=w_xkv, w_xo=w_xo,
             norm_ffn=norm_ffn, w_group=w_group, b_group=b_group, w_router=w_router, b_router=b_router,
             w_gate=w_gate, w_up=w_up, w_down=w_down)
    bsz, seq, d = x.shape
    depth = norm_mix.shape[0]
    assert depth == 1, "the final norm is fused into the last layer's combine step"
    pos3, y_rows, wts, h2, tc = _layer(x, mem, p, 0)
    out = _combine(pos3, y_rows, wts, h2, norm_final.reshape(1, d), tc)
    return out.reshape(bsz, seq, d)
```

```python
import functools

import jax
import jax.numpy as jnp
from jax import lax
from jax.experimental import pallas as pl
from jax.experimental.pallas import tpu as pltpu

F32 = jnp.float32
BF16 = jnp.bfloat16

EPS = 1e-6
LOG2_E = 1.4426950408889634
D_MODEL = 2048

GLA_HEADS = 4
GLA_DK = 128
GLA_DV = 256
GLA_K = GLA_HEADS * GLA_DK
GLA_V = GLA_HEADS * GLA_DV
GLA_RANK = 16
GLA_TAU = 16.0
GLA_CHUNK = 64
GLA_SUB = 16
GLA_SEQS = 4
GLA_SAFE_LOG2 = 60.0

MLSTM_HEADS = 4
MLSTM_DH = 256
MLSTM_W = MLSTM_HEADS * MLSTM_DH
CONV_WIDTH = 5
MLSTM_CHUNK = 256
MLSTM_SEQS = 1
CONV_HALO = 16

XATTN_HEADS = 4
XATTN_DH = D_MODEL // XATTN_HEADS

N_GROUPS = 4
EXPERTS_PER_GROUP = 8
N_EXPERTS = N_GROUPS * EXPERTS_PER_GROUP
D_EXPERT = 512
MOE_ROWS = 256
COMBINE_TOK = 256
ROUTE_COLS = 8

LANES = 128
SLAB_W = LANES
SLAB = D_MODEL // SLAB_W
SLAB_PITCH = SLAB + 8
SLAB_DTYPE = F32
GATHER_SLOTS = 3
GATHER_BATCHES = 8

OFF_GA, OFF_GB = 0, 2048
OFF_Q, OFF_K, OFF_V, OFF_GG = 4096, 4608, 5120, 6144
OFF_MX, OFF_MZ = 7168, 8192
N_BIG = 9216
SM_LRF, SM_LRB, SM_GATES = 0, 16, 32

VMEM_LIMIT = 56 * 1024 * 1024


def _cparams(sem):
    return pltpu.CompilerParams(dimension_semantics=sem, vmem_limit_bytes=VMEM_LIMIT)


def _rms(x, g):
    return x * lax.rsqrt(jnp.mean(x * x, axis=-1, keepdims=True) + EPS) * g


def _log_sigmoid(x):
    return jnp.minimum(x, 0.0) - jnp.log(1.0 + jnp.exp(-jnp.abs(x)))


def _dot(a, b):
    return jnp.dot(a, b, preferred_element_type=F32)


def _dot_nt(a, b):
    return lax.dot_general(a, b, (((1,), (1,)), ((), ())), preferred_element_type=F32)


def _dot_tn(a, b):
    return lax.dot_general(a, b, (((0,), (0,)), ((), ())), preferred_element_type=F32)


def _order_mask(c, rev):
    t = lax.broadcasted_iota(jnp.int32, (c, c), 0)
    s = lax.broadcasted_iota(jnp.int32, (c, c), 1)
    return (s >= t) if rev else (s <= t)


def _split3(x):
    hi = x.astype(BF16)
    r1 = x - hi.astype(F32)
    mid = r1.astype(BF16)
    return hi, mid, (r1 - mid.astype(F32)).astype(BF16)


def _cumsum_mm(mask_bf16, x):
    hi, mid, lo = _split3(x)
    return _dot(mask_bf16, hi) + _dot(mask_bf16, mid) + _dot(mask_bf16, lo)


def _in_proj_kernel(x_ref, g_ref, wbig_ref, wsm_ref, big_ref, sm_ref, xn_ref):
    @pl.when(pl.program_id(1) == 0)
    def _():
        xn = _rms(x_ref[...], g_ref[...]).astype(BF16)
        xn_ref[...] = xn
        sm_ref[...] = _dot(xn, wsm_ref[...])
    big_ref[...] = _dot(xn_ref[...], wbig_ref[...]).astype(BF16)


def _in_proj(x2, g, w_big, w_small, tm, tn):
    t, d = x2.shape
    n = w_big.shape[1]
    return pl.pallas_call(
        _in_proj_kernel,
        grid=(t // tm, n // tn),
        in_specs=[
            pl.BlockSpec((tm, d), lambda i, j: (i, 0)),
            pl.BlockSpec((1, d), lambda i, j: (0, 0)),
            pl.BlockSpec((d, tn), lambda i, j: (0, j)),
            pl.BlockSpec((d, LANES), lambda i, j: (0, 0)),
        ],
        out_specs=[
            pl.BlockSpec((tm, tn), lambda i, j: (i, j)),
            pl.BlockSpec((tm, LANES), lambda i, j: (i, 0)),
        ],
        out_shape=[jax.ShapeDtypeStruct((t, n), BF16), jax.ShapeDtypeStruct((t, LANES), F32)],
        scratch_shapes=[pltpu.VMEM((tm, d), BF16)],
        compiler_params=_cparams(("parallel", "arbitrary")),
        name="in_proj",
    )(x2, g, w_big, w_small)


def _gla_scores_any_decay(bh, qh, kh, kh_b, rev, c):
    sb = GLA_SUB
    n_sub = c // sb
    col = lax.broadcasted_iota(jnp.int32, (sb, c), 1)
    trow = lax.broadcasted_iota(jnp.int32, (sb, c), 0)
    slabs = [(qh[i * sb:(i + 1) * sb]
              * jnp.exp2(bh[i * sb:(i + 1) * sb] - bh[i * sb + s:i * sb + s + 1, :])).astype(BF16)
             for i in range(n_sub) for s in range(sb)]
    g = _dot_nt(jnp.concatenate(slabs, axis=0), kh_b)
    rows = []
    for i in range(n_sub):
        r0 = i * sb
        ref_row = r0 + (sb - 1 if rev else 0)
        beta = bh[ref_row:ref_row + 1, :]
        qt = (qh[r0:r0 + sb] * jnp.exp2(bh[r0:r0 + sb] - beta)).astype(BF16)
        kt = (kh * jnp.exp2(beta - bh)).astype(BF16)
        a_off = _dot_nt(qt, kt)
        a_diag = jnp.zeros((sb, c), F32)
        for s in range(sb):
            g0 = (i * sb + s) * sb
            a_diag = jnp.where(col == r0 + s, g[g0:g0 + sb], a_diag)
        tr = trow + r0
        if rev:
            off_mask = col >= r0 + sb
            diag_mask = (col >= tr) & (col < r0 + sb)
        else:
            off_mask = col < r0
            diag_mask = (col <= tr) & (col >= r0)
        rows.append(jnp.where(off_mask, a_off, 0.0) + jnp.where(diag_mask, a_diag, 0.0))
    return jnp.concatenate(rows, axis=0).astype(BF16)


def _gla_kernel(q_ref, k_ref, v_ref, sm_ref, wlr_ref, blr_ref, *rest, rev, final, c, nb):
    if final:
        oprev_ref, gg_ref, gn_ref, o_ref, st_ref, a_ref = rest
    else:
        o_ref, st_ref, a_ref = rest

    @pl.when(pl.program_id(1) == 0)
    def _():
        st_ref[...] = jnp.zeros_like(st_ref)

    mask = _order_mask(c, rev)
    mask_b = jnp.where(mask, 1.0, 0.0).astype(BF16)
    last = 0 if rev else c - 1

    for bi in range(nb):
        x = _dot(sm_ref[bi].astype(BF16), wlr_ref[...]) + blr_ref[...]
        la = _log_sigmoid(x) * (1.0 / GLA_TAU)
        b = _cumsum_mm(mask_b, la) * LOG2_E
        tot = b[last:last + 1, :]
        q = q_ref[bi].astype(F32) * (GLA_DK ** -0.5)
        k = k_ref[bi].astype(F32)
        q_in = (q * jnp.exp2(b)).astype(BF16)
        k_dec = (k * jnp.exp2(tot - b)).astype(BF16)
        e_tot = jnp.exp2(tot)

        mild = jnp.min(tot) > -GLA_SAFE_LOG2

        @pl.when(mild)
        def _():
            q_up = (q * jnp.exp2(b - tot)).astype(BF16)
            for h in range(GLA_HEADS):
                ks = slice(h * GLA_DK, (h + 1) * GLA_DK)
                a_ref[bi, h] = jnp.where(mask, _dot_nt(q_up[:, ks], k_dec[:, ks]), 0.0).astype(BF16)

        @pl.when(jnp.logical_not(mild))
        def _():
            for h in range(GLA_HEADS):
                ks = slice(h * GLA_DK, (h + 1) * GLA_DK)
                a_ref[bi, h] = _gla_scores_any_decay(b[:, ks], q[:, ks], k[:, ks], k_ref[bi, :, ks], rev, c)

        for h in range(GLA_HEADS):
            ks = slice(h * GLA_DK, (h + 1) * GLA_DK)
            vs = slice(h * GLA_DV, (h + 1) * GLA_DV)
            vh = v_ref[bi, :, vs]
            st = st_ref[bi, h]
            o_inter = _dot_nt(q_in[:, ks], st.astype(BF16))
            st_ref[bi, h] = st * e_tot[:, ks] + _dot_tn(vh, k_dec[:, ks])
            o = o_inter + _dot(a_ref[bi, h], vh)
            if final:
                o = o + oprev_ref[bi, :, vs]
                y = _rms(o, gn_ref[:, vs])
                gg = gg_ref[bi, :, vs].astype(F32)
                o_ref[bi, :, vs] = (y * (gg * jax.nn.sigmoid(gg))).astype(o_ref.dtype)
            else:
                o_ref[bi, :, vs] = o.astype(o_ref.dtype)


def _gla_scan(big3, sm3, wlr, blr, rev, prev=None, gn=None):
    bsz, seq, _ = big3.shape
    c = GLA_CHUNK
    n = seq // c
    nb = GLA_SEQS if bsz % GLA_SEQS == 0 else 1
    final = prev is not None

    def cm(ci):
        return (n - 1 - ci) if rev else ci

    in_specs = [
        pl.BlockSpec((nb, c, GLA_K), lambda b, ci: (b, cm(ci), OFF_Q // GLA_K)),
        pl.BlockSpec((nb, c, GLA_K), lambda b, ci: (b, cm(ci), OFF_K // GLA_K)),
        pl.BlockSpec((nb, c, GLA_V), lambda b, ci: (b, cm(ci), OFF_V // GLA_V)),
        pl.BlockSpec((nb, c, LANES), lambda b, ci: (b, cm(ci), 0)),
        pl.BlockSpec((LANES, GLA_K), lambda b, ci: (0, 0)),
        pl.BlockSpec((1, GLA_K), lambda b, ci: (0, 0)),
    ]
    args = [big3, big3, big3, sm3, wlr, blr]
    if final:
        in_specs += [
            pl.BlockSpec((nb, c, GLA_V), lambda b, ci: (b, cm(ci), 0)),
            pl.BlockSpec((nb, c, GLA_V), lambda b, ci: (b, cm(ci), OFF_GG // GLA_V)),
            pl.BlockSpec((1, GLA_V), lambda b, ci: (0, 0)),
        ]
        args += [prev, big3, gn]
    return pl.pallas_call(
        functools.partial(_gla_kernel, rev=rev, final=final, c=c, nb=nb),
        grid=(bsz // nb, n),
        in_specs=in_specs,
        out_specs=pl.BlockSpec((nb, c, GLA_V), lambda b, ci: (b, cm(ci), 0)),
        out_shape=jax.ShapeDtypeStruct((bsz, seq, GLA_V), BF16 if final else F32),
        scratch_shapes=[pltpu.VMEM((nb, GLA_HEADS, GLA_DV, GLA_DK), F32),
                        pltpu.VMEM((nb, GLA_HEADS, c, c), BF16)],
        compiler_params=_cparams(("parallel", "arbitrary")),
        name="gla_bwd" if rev else "gla_fwd",
    )(*args)


def _mlstm_pre_kernel(cur_ref, prev_ref, next_ref, cw_ref, cb_ref, wq_ref, wk_ref, wv_ref,
                      q_ref, k_ref, v_ref, *, tm):
    i = pl.program_id(2)
    cur_b = cur_ref[0]
    cur = cur_b.astype(F32)
    halo = CONV_WIDTH // 2
    prev = jnp.where(i > 0, prev_ref[0].astype(F32), 0.0)
    nxt = jnp.where(i < pl.num_programs(2) - 1, next_ref[0].astype(F32), 0.0)
    ext = jnp.concatenate([prev[CONV_HALO - 8:], cur, nxt[:8]], axis=0)
    acc = jnp.zeros_like(cur) + cb_ref[...]
    for w in range(CONV_WIDTH):
        off = 8 - halo + w
        acc = acc + ext[off:off + tm] * cw_ref[w:w + 1, :]
    xc = (acc * jax.nn.sigmoid(acc)).astype(BF16)
    q_ref[0] = _dot(xc, wq_ref[0]).astype(BF16)
    k_ref[0] = (_dot(xc, wk_ref[0]) * (MLSTM_DH ** -0.5)).astype(BF16)
    v_ref[0] = _dot(cur_b, wv_ref[0]).astype(BF16)


def _mlstm_pre(big3, cw, cb, wq, wk, wv, tm):
    bsz, seq, _ = big3.shape
    dh = MLSTM_DH
    nh = tm // CONV_HALO
    n_halo = seq // CONV_HALO
    c0 = OFF_MX // dh
    out = jax.ShapeDtypeStruct((bsz, seq, MLSTM_W), BF16)
    ospec = pl.BlockSpec((1, tm, dh), lambda b, h, i: (b, i, h))
    wspec = pl.BlockSpec((1, dh, dh), lambda b, h, i: (h, 0, 0))
    return pl.pallas_call(
        functools.partial(_mlstm_pre_kernel, tm=tm),
        grid=(bsz, MLSTM_HEADS, seq // tm),
        in_specs=[
            pl.BlockSpec((1, tm, dh), lambda b, h, i: (b, i, c0 + h)),
            pl.BlockSpec((1, CONV_HALO, dh), lambda b, h, i: (b, jnp.maximum(i * nh - 1, 0), c0 + h)),
            pl.BlockSpec((1, CONV_HALO, dh), lambda b, h, i: (b, jnp.minimum((i + 1) * nh, n_halo - 1), c0 + h)),
            pl.BlockSpec((8, dh), lambda b, h, i: (0, h)),
            pl.BlockSpec((1, dh), lambda b, h, i: (0, h)),
            wspec, wspec, wspec,
        ],
        out_specs=[ospec, ospec, ospec],
        out_shape=[out, out, out],
        compiler_params=_cparams(("parallel", "parallel", "parallel")),
        name="mlstm_pre",
    )(big3, big3, big3, cw, cb, wq, wk, wv)


def _mlstm_kernel(q_ref, k_ref, v_ref, sm_ref, gb_ref, *rest, rev, final, c, nb):
    if final:
        hprev_ref, mz_ref, mn_ref, o_ref, c_ref, n_ref, m_ref = rest
    else:
        o_ref, c_ref, n_ref, m_ref = rest

    @pl.when(pl.program_id(1) == 0)
    def _():
        c_ref[...] = jnp.zeros_like(c_ref)
        n_ref[...] = jnp.zeros_like(n_ref)
        m_ref[...] = jnp.zeros_like(m_ref)

    mask = _order_mask(c, rev)
    mask_b = jnp.where(mask, 1.0, 0.0).astype(BF16)
    last = 0 if rev else c - 1
    dh = MLSTM_DH

    for bi in range(nb):
        g = sm_ref[bi] + gb_ref[...]
        bcum = _cumsum_mm(mask_b, _log_sigmoid(g))
        g_t = g.T
        b_t = bcum.T

        for h in range(MLSTM_HEADS):
            ci = SM_GATES + (2 * MLSTM_HEADS if rev else 0) + h
            cf = ci + MLSTM_HEADS
            hs = slice(h * dh, (h + 1) * dh)
            i_col, b_col = g[:, ci:ci + 1], bcum[:, cf:cf + 1]
            i_row, b_row = g_t[ci:ci + 1, :], b_t[cf:cf + 1, :]
            tot = b_col[last:last + 1, :]
            m_prev = m_ref[bi, h, 0:1, 0:1]
            qh, kh, vh = q_ref[bi, :, hs], k_ref[bi, :, hs], v_ref[bi, :, hs]
            c_st = c_ref[bi, h]
            n_st = n_ref[bi, h, 0:1, :]

            dmat = jnp.where(mask, b_col - b_row + i_row, -jnp.inf)
            inter_log = b_col + m_prev
            m_t = jnp.maximum(inter_log, jnp.max(dmat, axis=-1, keepdims=True))
            w_intra = jnp.exp(dmat - m_t)
            w_inter = jnp.exp(inter_log - m_t)
            s = _dot_nt(qh, kh) * w_intra
            num = w_inter * _dot(qh, c_st.astype(BF16)) + _dot(s.astype(BF16), vh)
            den = (w_inter * jnp.sum(qh.astype(F32) * n_st, axis=-1, keepdims=True)
                   + jnp.sum(s, axis=-1, keepdims=True))
            hh = num / jnp.maximum(jnp.abs(den), jnp.exp(-m_t))

            upd_col = tot - b_col + i_col
            upd_row = tot - b_row + i_row
            m_new = jnp.maximum(tot + m_prev, jnp.max(upd_row, axis=-1, keepdims=True))
            w_old = jnp.exp(tot + m_prev - m_new)
            kw = kh.astype(F32) * jnp.exp(upd_col - m_new)
            c_ref[bi, h] = w_old * c_st + _dot_tn(kw.astype(BF16), vh)
            n_ref[bi, h] = jnp.broadcast_to(w_old * n_st + jnp.sum(kw, axis=0, keepdims=True), (8, dh))
            m_ref[bi, h] = jnp.broadcast_to(m_new, (8, LANES))

            if final:
                hh = hh + hprev_ref[bi, :, hs]
                y = _rms(hh, mn_ref[:, hs])
                o_ref[bi, :, hs] = (y * jax.nn.sigmoid(mz_ref[bi, :, hs].astype(F32))).astype(o_ref.dtype)
            else:
                o_ref[bi, :, hs] = hh.astype(o_ref.dtype)


def _mlstm_scan(qm, km, vm, sm3, gbias, rev, prev=None, big3=None, mn=None):
    bsz, seq, w = qm.shape
    c = min(MLSTM_CHUNK, seq)
    n = seq // c
    nb = MLSTM_SEQS if bsz % MLSTM_SEQS == 0 else 1
    final = prev is not None

    def cm(ci):
        return (n - 1 - ci) if rev else ci

    xspec = pl.BlockSpec((nb, c, w), lambda b, ci: (b, cm(ci), 0))
    in_specs = [xspec, xspec, xspec,
                pl.BlockSpec((nb, c, LANES), lambda b, ci: (b, cm(ci), 0)),
                pl.BlockSpec((1, LANES), lambda b, ci: (0, 0))]
    args = [qm, km, vm, sm3, gbias]
    if final:
        in_specs += [xspec,
                     pl.BlockSpec((nb, c, w), lambda b, ci: (b, cm(ci), OFF_MZ // MLSTM_W)),
                     pl.BlockSpec((1, w), lambda b, ci: (0, 0))]
        args += [prev, big3, mn]
    return pl.pallas_call(
        functools.partial(_mlstm_kernel, rev=rev, final=final, c=c, nb=nb),
        grid=(bsz // nb, n),
        in_specs=in_specs,
        out_specs=xspec,
        out_shape=jax.ShapeDtypeStruct((bsz, seq, w), BF16 if final else F32),
        scratch_shapes=[pltpu.VMEM((nb, MLSTM_HEADS, MLSTM_DH, MLSTM_DH), F32),
                        pltpu.VMEM((nb, MLSTM_HEADS, 8, MLSTM_DH), F32),
                        pltpu.VMEM((nb, MLSTM_HEADS, 8, LANES), F32)],
        compiler_params=_cparams(("parallel", "arbitrary")),
        name="mlstm_bwd" if rev else "mlstm_fwd",
    )(*args)


def _merge_mix_kernel(ya_ref, yb_ref, wa_ref, wb_ref, ga_ref, gb_ref, wo_ref, x_ref, o_ref):
    a = _dot(ya_ref[...], wa_ref[...])
    b = _dot(yb_ref[...], wb_ref[...])
    ga = jax.nn.sigmoid(ga_ref[...].astype(F32))
    gb = jax.nn.sigmoid(gb_ref[...].astype(F32))
    merged = (ga * a + gb * b).astype(BF16)
    o_ref[...] = x_ref[...] + _dot(merged, wo_ref[...])


def _merge_mix(ya, yb, wa, wb, big, wo, x2, tm):
    t, kdim = ya.shape
    n = wa.shape[1]

    def resident(shape):
        return pl.BlockSpec(shape, lambda i: (0, 0), pipeline_mode=pl.Buffered(1))

    return pl.pallas_call(
        _merge_mix_kernel,
        grid=(t // tm,),
        in_specs=[
            pl.BlockSpec((tm, kdim), lambda i: (i, 0)),
            pl.BlockSpec((tm, kdim), lambda i: (i, 0)),
            resident((kdim, n)),
            resident((kdim, n)),
            pl.BlockSpec((tm, n), lambda i: (i, OFF_GA // n)),
            pl.BlockSpec((tm, n), lambda i: (i, OFF_GB // n)),
            resident((n, n)),
            pl.BlockSpec((tm, n), lambda i: (i, 0)),
        ],
        out_specs=pl.BlockSpec((tm, n), lambda i: (i, 0)),
        out_shape=jax.ShapeDtypeStruct((t, n), F32),
        compiler_params=_cparams(("parallel",)),
        name="merge_mix",
    )(ya, yb, wa, wb, big, big, wo, x2)


def _kv_kernel(mem_ref, g_ref, w_ref, o_ref):
    mn = _rms(mem_ref[...], g_ref[...]).astype(BF16)
    o_ref[...] = _dot(mn, w_ref[...]).astype(BF16)


def _kv_proj(mem2, g, w, tn):
    r, d = mem2.shape
    n = w.shape[1]
    return pl.pallas_call(
        _kv_kernel,
        grid=(n // tn,),
        in_specs=[
            pl.BlockSpec((r, d), lambda j: (0, 0)),
            pl.BlockSpec((1, d), lambda j: (0, 0)),
            pl.BlockSpec((d, tn), lambda j: (0, j)),
        ],
        out_specs=pl.BlockSpec((r, tn), lambda j: (0, j)),
        out_shape=jax.ShapeDtypeStruct((r, n), BF16),
        compiler_params=_cparams(("parallel",)),
        name="kv_proj",
    )(mem2, g, w)


def _xattn_kernel(h_ref, g_ref, wq_ref, k_ref, v_ref, wo_ref, o_ref, xn_ref, att_ref):
    hd = pl.program_id(1)

    @pl.when(hd == 0)
    def _():
        xn_ref[...] = _rms(h_ref[...], g_ref[...]).astype(BF16)

    q = _dot(xn_ref[...], wq_ref[...]).astype(BF16)
    s = _dot_nt(q, k_ref[0]) * (XATTN_DH ** -0.5)
    p = jnp.exp(s - jnp.max(s, axis=-1, keepdims=True))
    p = p / jnp.sum(p, axis=-1, keepdims=True)
    att_ref[hd] = _dot(p.astype(BF16), v_ref[0]).astype(BF16)

    @pl.when(hd == XATTN_HEADS - 1)
    def _():
        acc = h_ref[...]
        for h in range(XATTN_HEADS):
            acc = acc + _dot(att_ref[h], wo_ref[h * XATTN_DH:(h + 1) * XATTN_DH, :])
        o_ref[...] = acc


def _xattn(h1, g, wq, kv3, wo, seq, tm):
    t, d = h1.shape
    n_mem = kv3.shape[1]
    dh = XATTN_DH
    per_b = seq // tm
    return pl.pallas_call(
        _xattn_kernel,
        grid=(t // tm, XATTN_HEADS),
        in_specs=[
            pl.BlockSpec((tm, d), lambda i, h: (i, 0)),
            pl.BlockSpec((1, d), lambda i, h: (0, 0)),
            pl.BlockSpec((d, dh), lambda i, h: (0, h)),
            pl.BlockSpec((1, n_mem, dh), lambda i, h: (i // per_b, 0, h)),
            pl.BlockSpec((1, n_mem, dh), lambda i, h: (i // per_b, 0, XATTN_HEADS + h)),
            pl.BlockSpec((d, d), lambda i, h: (0, 0), pipeline_mode=pl.Buffered(1)),
        ],
        out_specs=pl.BlockSpec((tm, d), lambda i, h: (i, 0)),
        out_shape=jax.ShapeDtypeStruct((t, d), F32),
        scratch_shapes=[pltpu.VMEM((tm, d), BF16), pltpu.VMEM((XATTN_HEADS, tm, dh), BF16)],
        compiler_params=_cparams(("parallel", "arbitrary")),
        name="xattn",
    )(h1, g, wq, kv3, kv3, wo)


def _slab_store(ref, val):
    n = val.shape[0]
    for c in range(SLAB):
        ref[pl.ds(c, n, stride=SLAB), :] = val[:, c * SLAB_W:(c + 1) * SLAB_W].astype(ref.dtype)


def _slab_load(buf, slot, first, n):
    return jnp.concatenate(
        [buf[slot, pl.ds(first * SLAB_PITCH + c, n, stride=SLAB_PITCH), :] for c in range(SLAB)], axis=1)


def _slab_copy(src_hbm, dst_buf, sem, slot, src_slab, dst_slab):
    return pltpu.make_async_copy(src_hbm.at[pl.ds(src_slab * SLAB, SLAB)],
                                 dst_buf.at[slot, pl.ds(dst_slab * SLAB_PITCH, SLAB)], sem.at[slot])


def _zero_after(v):
    bits = lax.bitcast_convert_type(v[-1:, -1:], jnp.uint32)
    return ((bits >> 16) >> 16).astype(jnp.int32)[0, 0]


def _slab_wait(src_hbm, dst_buf, sem, slot, n):
    pltpu.make_async_copy(src_hbm.at[pl.ds(0, n * SLAB)], dst_buf.at[slot, pl.ds(0, n * SLAB)], sem.at[slot]).wait()


def _router_kernel(h_ref, g_ref, wr_ref, br_ref, xn_ref, ids_ref, wts_ref, cnt_ref, carry_ref):
    @pl.when(pl.program_id(0) == 0)
    def _():
        carry_ref[...] = jnp.zeros_like(carry_ref)

    xn = _rms(h_ref[...], g_ref[...])
    _slab_store(xn_ref, xn)
    hi = xn.astype(BF16)
    lo = (xn - hi.astype(F32)).astype(BF16)
    lg = _dot(hi, wr_ref[0]) + _dot(hi, wr_ref[1]) + _dot(lo, wr_ref[0]) + br_ref[...]
    lane = lax.broadcasted_iota(jnp.int32, lg.shape, 1).astype(F32)
    ninf = -jnp.inf
    big_lane = float(LANES)

    def first_max(v):
        mx = jnp.max(v, axis=-1, keepdims=True)
        return mx, jnp.min(jnp.where(v == mx, lane, big_lane), axis=-1, keepdims=True)

    gl = jnp.where(lane < N_GROUPS, lg, ninf)
    gmax, g_sel = first_max(gl)
    p_g = 1.0 / jnp.sum(jnp.exp(gl - gmax), axis=-1, keepdims=True)
    lo_l = N_GROUPS + EXPERTS_PER_GROUP * g_sel
    el = jnp.where((lane >= lo_l) & (lane < lo_l + EXPERTS_PER_GROUP), lg, ninf)
    emax, i1 = first_max(el)
    max2, i2 = first_max(jnp.where(lane == i1, ninf, el))
    e2 = jnp.exp(max2 - emax)
    w1 = p_g / (1.0 + e2)
    w2 = p_g * e2 / (1.0 + e2)
    e1, e2 = i1 - N_GROUPS, i2 - N_GROUPS
    oh1, oh2 = lane == e1, lane == e2
    oh = jnp.where(oh1, 1.0, 0.0) + jnp.where(oh2, 1.0, 0.0)
    tm = lg.shape[0]
    earlier = lax.broadcasted_iota(jnp.int32, (tm, tm), 1) < lax.broadcasted_iota(jnp.int32, (tm, tm), 0)
    carry = carry_ref[0:1, :]
    before = _dot(jnp.where(earlier, 1.0, 0.0).astype(BF16), oh.astype(BF16)) + carry
    r1 = jnp.sum(jnp.where(oh1, before, 0.0), axis=-1, keepdims=True)
    r2 = jnp.sum(jnp.where(oh2, before, 0.0), axis=-1, keepdims=True)
    counts = carry + jnp.sum(oh, axis=0, keepdims=True)
    carry_ref[...] = jnp.broadcast_to(counts, carry_ref.shape)
    cnt_ref[...] = jnp.broadcast_to(counts, cnt_ref.shape).astype(jnp.int32)
    ids = jnp.where(lane == 0.0, e1, jnp.where(lane == 1.0, e2, jnp.where(lane == 2.0, r1,
                                                                         jnp.where(lane == 3.0, r2, 0.0))))
    ids_ref[...] = ids.astype(jnp.int32)[:, 0:ROUTE_COLS]
    wts_ref[...] = jnp.where(lane == 0.0, w1, jnp.where(lane == 1.0, w2, 0.0))


def _router(h2, g, wr, br, tm):
    t, d = h2.shape
    return pl.pallas_call(
        _router_kernel,
        grid=(t // tm,),
        in_specs=[
            pl.BlockSpec((tm, d), lambda i: (i, 0)),
            pl.BlockSpec((1, d), lambda i: (0, 0)),
            pl.BlockSpec((2, d, LANES), lambda i: (0, 0, 0)),
            pl.BlockSpec((1, LANES), lambda i: (0, 0)),
        ],
        out_specs=[
            pl.BlockSpec((tm * SLAB, SLAB_W), lambda i: (i, 0)),
            pl.BlockSpec((tm, ROUTE_COLS), lambda i: (i, 0)),
            pl.BlockSpec((tm, LANES), lambda i: (i, 0)),
            pl.BlockSpec((8, LANES), lambda i: (0, 0)),
        ],
        out_shape=[jax.ShapeDtypeStruct((t * SLAB, SLAB_W), SLAB_DTYPE),
                   jax.ShapeDtypeStruct((t, ROUTE_COLS), jnp.int32),
                   jax.ShapeDtypeStruct((t, LANES), F32),
                   jax.ShapeDtypeStruct((8, LANES), jnp.int32)],
        scratch_shapes=[pltpu.VMEM((8, LANES), F32)],
        compiler_params=_cparams(("arbitrary",)),
        name="router",
    )(h2, g, wr, br)


def _experts_kernel(blk_e_ref, n_used_ref, next_e_ref, wslot_ref, tok0_ref, tok1_ref, tok2_ref, x_hbm,
                    wg_hbm, wu_hbm, wd_hbm, y_ref, xbuf, wg_f, wu_f, wd_f, wg_b, wu_b, wd_b, sem, wsem, *, rows):
    j = pl.program_id(0)
    n_used = n_used_ref[0]
    used = j < n_used
    slot = j % GATHER_SLOTS
    slot1 = (j + 1) % GATHER_SLOTS
    slot2 = (j + 2) % GATHER_SLOTS

    @pl.when(j == 0)
    def _():
        def start(r, carry):
            _slab_copy(x_hbm, xbuf, sem, 0, tok0_ref[0, 0, r], r).start()
            _slab_copy(x_hbm, xbuf, sem, 1, tok1_ref[0, 0, r], r).start()
            return carry
        lax.fori_loop(0, rows, start, 0)

    @pl.when(jnp.logical_not(used))
    def _():
        @pl.when(j == n_used)
        def _():
            _slab_wait(x_hbm, xbuf, sem, slot, rows)
            _slab_wait(x_hbm, xbuf, sem, slot1, rows)
        y_ref[...] = jnp.zeros_like(y_ref)

    def weight_copies(e, ws):
        return [pltpu.make_async_copy(src.at[e], dst.at[ws], wsem.at[ws])
                for src, dst in ((wg_hbm, wg_f), (wu_hbm, wu_f), (wd_hbm, wd_f))]

    @pl.when(j == 0)
    def _():
        for cp in weight_copies(blk_e_ref[0], 0):
            cp.start()

    @pl.when(used)
    def _():
        _slab_wait(x_hbm, xbuf, sem, slot, rows)

        @pl.when((j == 0) | (blk_e_ref[j] != blk_e_ref[jnp.maximum(j - 1, 0)]))
        def _():
            ws = wslot_ref[j]
            for cp in weight_copies(0, ws):
                cp.wait()

            @pl.when(next_e_ref[j] >= 0)
            def _():
                for cp in weight_copies(next_e_ref[j], 1 - ws):
                    cp.start()

            wg_b[...] = wg_f[ws].astype(BF16)
            wu_b[...] = wu_f[ws].astype(BF16)
            wd_b[...] = wd_f[ws].astype(BF16)

        batch = rows // GATHER_BATCHES
        issued = [0]

        def gather_batch(after):
            zero = 0 if after is None else _zero_after(after)
            for r in range(issued[0] * batch, (issued[0] + 1) * batch):
                _slab_copy(x_hbm, xbuf, sem, slot2, tok2_ref[0, 0, r] + zero, r).start()
            issued[0] += 1

        xb = _slab_load(xbuf, slot, 0, rows).astype(BF16)
        gather_batch(None)
        de = wg_b.shape[1]
        halves = [slice(0, de // 2), slice(de // 2, de)]
        gate, up = [], []
        for cs in halves:
            gate.append(_dot(xb, wg_b[:, cs]))
            gather_batch(gate[-1])
        for cs in halves:
            up.append(_dot(xb, wu_b[:, cs]))
            gather_batch(up[-1])
        gate = jnp.concatenate(gate, axis=1)
        act = (gate * jax.nn.sigmoid(gate) * jnp.concatenate(up, axis=1)).astype(BF16)
        wo = de
        per = wo // SLAB_W
        for c in range(wd_b.shape[1] // wo):
            yc = _dot(act, wd_b[:, c * wo:(c + 1) * wo])
            for cc in range(per):
                y_ref[pl.ds(c * per + cc, rows, stride=SLAB), :] = (
                    yc[:, cc * SLAB_W:(cc + 1) * SLAB_W].astype(y_ref.dtype))
            if issued[0] < GATHER_BATCHES:
                gather_batch(yc)
        assert issued[0] == GATHER_BATCHES

        @pl.when(j == pl.num_programs(0) - 1)
        def _():
            _slab_wait(x_hbm, xbuf, sem, slot1, rows)
            _slab_wait(x_hbm, xbuf, sem, slot2, rows)


def _experts(blk_e, n_used, next_e, wslot, row_tok3, xn_slab, wg, wu, wd):
    n_blocks, _, rows = row_tok3.shape
    _, d, de = wg.shape

    def tok_spec(ahead):
        return pl.BlockSpec((1, 1, rows), lambda j, *_: (jnp.minimum(j + ahead, n_blocks - 1), 0, 0),
                            memory_space=pltpu.SMEM)

    hbm = pl.BlockSpec(memory_space=pl.ANY)
    grid_spec = pltpu.PrefetchScalarGridSpec(
        num_scalar_prefetch=4,
        grid=(n_blocks,),
        in_specs=[tok_spec(0), tok_spec(1), tok_spec(2), hbm, hbm, hbm, hbm],
        out_specs=pl.BlockSpec((rows * SLAB, SLAB_W), lambda j, *_: (j, 0)),
        scratch_shapes=[pltpu.VMEM((GATHER_SLOTS, rows * SLAB_PITCH, SLAB_W), SLAB_DTYPE),
                        pltpu.VMEM((2, d, de), F32), pltpu.VMEM((2, d, de), F32), pltpu.VMEM((2, de, d), F32),
                        pltpu.VMEM((d, de), BF16), pltpu.VMEM((d, de), BF16), pltpu.VMEM((de, d), BF16),
                        pltpu.SemaphoreType.DMA((GATHER_SLOTS,)), pltpu.SemaphoreType.DMA((2,))],
    )
    return pl.pallas_call(
        functools.partial(_experts_kernel, rows=rows),
        grid_spec=grid_spec,
        out_shape=jax.ShapeDtypeStruct((n_blocks * rows * SLAB, SLAB_W), SLAB_DTYPE),
        compiler_params=_cparams(("arbitrary",)),
        name="experts",
    )(blk_e, n_used, next_e, wslot, row_tok3, row_tok3, row_tok3, xn_slab, wg, wu, wd)


def _combine_kernel(pos0_ref, pos1_ref, pos2_ref, y_hbm, wts_ref, h_ref, g_ref, o_ref, ybuf, sem, *, tc):
    j = pl.program_id(0)
    slot = j % GATHER_SLOTS
    slot1 = (j + 1) % GATHER_SLOTS
    slot2 = (j + 2) % GATHER_SLOTS

    @pl.when(j == 0)
    def _():
        def start(r, carry):
            for s, idx in ((0, pos0_ref), (1, pos1_ref)):
                _slab_copy(y_hbm, ybuf, sem, s, idx[0, 0, 2 * r], r).start()
                _slab_copy(y_hbm, ybuf, sem, s, idx[0, 0, 2 * r + 1], tc + r).start()
            return carry
        lax.fori_loop(0, tc, start, 0)

    _slab_wait(y_hbm, ybuf, sem, slot, 2 * tc)
    for r in range(tc):
        _slab_copy(y_hbm, ybuf, sem, slot2, pos2_ref[0, 0, 2 * r], r).start(priority=0)
        _slab_copy(y_hbm, ybuf, sem, slot2, pos2_ref[0, 0, 2 * r + 1], tc + r).start(priority=1)
    w = wts_ref[...]
    y = (_slab_load(ybuf, slot, 0, tc).astype(F32) * w[:, 0:1]
         + _slab_load(ybuf, slot, tc, tc).astype(F32) * w[:, 1:2])
    o_ref[...] = _rms(h_ref[...] + y, g_ref[...])

    @pl.when(j == pl.num_programs(0) - 1)
    def _():
        _slab_wait(y_hbm, ybuf, sem, slot1, 2 * tc)
        _slab_wait(y_hbm, ybuf, sem, slot2, 2 * tc)


def _combine(pos3, y_slab, wts, h2, g, tc):
    t, d = h2.shape
    n = t // tc

    def pos_spec(ahead):
        return pl.BlockSpec((1, 1, 2 * tc), lambda i: (jnp.minimum(i + ahead, n - 1), 0, 0),
                            memory_space=pltpu.SMEM)

    return pl.pallas_call(
        functools.partial(_combine_kernel, tc=tc),
        grid=(n,),
        in_specs=[
            pos_spec(0), pos_spec(1), pos_spec(2),
            pl.BlockSpec(memory_space=pl.ANY),
            pl.BlockSpec((tc, LANES), lambda i: (i, 0)),
            pl.BlockSpec((tc, d), lambda i: (i, 0)),
            pl.BlockSpec((1, d), lambda i: (0, 0)),
        ],
        out_specs=pl.BlockSpec((tc, d), lambda i: (i, 0)),
        out_shape=jax.ShapeDtypeStruct((t, d), F32),
        scratch_shapes=[pltpu.VMEM((GATHER_SLOTS, 2 * tc * SLAB_PITCH, SLAB_W), SLAB_DTYPE),
                        pltpu.SemaphoreType.DMA((GATHER_SLOTS,))],
        compiler_params=_cparams(("arbitrary",)),
        name="combine",
    )(pos3, pos3, pos3, y_slab, wts, h2, g)


def _dispatch_plan(ids, counts, rows):
    flat_e = ids[:, 0:2].reshape(-1)
    rank = ids[:, 2:4].reshape(-1)
    n_asg = flat_e.shape[0]
    experts = jnp.arange(N_EXPERTS, dtype=jnp.int32)
    padded = ((counts + rows - 1) // rows) * rows
    pends = jnp.cumsum(padded)
    pstarts = pends - padded
    pos = jnp.sum(jnp.where(flat_e[:, None] == experts[None, :], pstarts[None, :], 0), axis=1) + rank
    n_rows = n_asg + N_EXPERTS * rows
    n_blocks = n_rows // rows
    row_tok = jnp.zeros((n_rows,), jnp.int32).at[pos].set(jnp.arange(n_asg, dtype=jnp.int32) // 2)
    starts = jnp.arange(n_blocks, dtype=jnp.int32) * rows
    blk_e = jnp.minimum(jnp.sum((pends[None, :] <= starts[:, None]).astype(jnp.int32), axis=1), N_EXPERTS - 1)
    n_used = (pends[-1] // rows).astype(jnp.int32).reshape(1)
    has = counts > 0
    succ = jnp.min(jnp.where((experts[None, :] > experts[:, None]) & has[None, :], experts[None, :], N_EXPERTS),
                   axis=1)
    succ = jnp.where(succ < N_EXPERTS, succ, -1)
    before = jnp.sum((has[None, :] & (experts[None, :] < blk_e[:, None])).astype(jnp.int32), axis=1)
    return (pos.astype(jnp.int32), row_tok.reshape(n_blocks, 1, rows), blk_e.astype(jnp.int32), n_used,
            succ[blk_e].astype(jnp.int32), (before % 2).astype(jnp.int32))


def _tile(n, pref):
    return pref if n % pref == 0 else n


def _layer(h, mem, p, l):
    bsz, seq, d = h.shape
    t = bsz * seq
    x2 = h.reshape(t, d)
    row = lambda v: v.reshape(1, -1)
    tm = _tile(t, 1024)

    w_in = p['w_in'][l]
    w_big = jnp.concatenate([w_in[:, 5168:9264], w_in[:, 0:3072], w_in[:, 3104:5152]], axis=1).astype(BF16)
    w_small = jnp.concatenate([w_in[:, 3072:3104], w_in[:, 5152:5168],
                               jnp.zeros((d, LANES - 48), F32)], axis=1).astype(BF16)
    big, small = _in_proj(x2, row(p['norm_mix'][l]), w_big, w_small, tm, 2304)
    big3 = big.reshape(bsz, seq, N_BIG)
    sm3 = small.reshape(bsz, seq, LANES)

    def lr_pad(w, off):
        return jnp.zeros((LANES, GLA_K), F32).at[off:off + GLA_RANK].set(w).astype(BF16)

    o_f = _gla_scan(big3, sm3, lr_pad(p['gla_w_lr_f'][l], SM_LRF), row(p['gla_b_lr_f'][l]), rev=False)
    y_a = _gla_scan(big3, sm3, lr_pad(p['gla_w_lr_b'][l], SM_LRB), row(p['gla_b_lr_b'][l]), rev=True,
                    prev=o_f, gn=row(p['gla_norm'][l]))

    cw = jnp.zeros((8, MLSTM_W), F32).at[:CONV_WIDTH].set(p['conv_w'][l].reshape(CONV_WIDTH, MLSTM_W))
    qm, km, vm = _mlstm_pre(big3, cw, row(p['conv_b'][l]), p['m_wq'][l].astype(BF16),
                            p['m_wk'][l].astype(BF16), p['m_wv'][l].astype(BF16), _tile(seq, 4096))
    gbias = jnp.zeros((1, LANES), F32).at[0, SM_GATES:SM_GATES + 4 * MLSTM_HEADS].set(
        p['m_gate_bias'][l].reshape(-1))
    h_f = _mlstm_scan(qm, km, vm, sm3, gbias, rev=False)
    y_b = _mlstm_scan(qm, km, vm, sm3, gbias, rev=True, prev=h_f, big3=big3, mn=row(p['m_norm'][l]))

    assert OFF_GA % d == 0 and OFF_GB % d == 0
    h1 = _merge_mix(y_a.reshape(t, GLA_V), y_b.reshape(t, MLSTM_W), p['w_branch_a'][l].astype(BF16),
                    p['w_branch_b'][l].astype(BF16), big, p['w_mix_out'][l].astype(BF16), x2, _tile(t, 512))

    n_mem = mem.shape[1]
    kv = _kv_proj(mem.reshape(bsz * n_mem, d), row(p['norm_mem'][l]), p['w_xkv'][l].astype(BF16), 1024)
    h2 = _xattn(h1, row(p['norm_xattn'][l]), p['w_xq'][l].astype(BF16), kv.reshape(bsz, n_mem, 2 * d),
                p['w_xo'][l].astype(BF16), seq, _tile(seq, 512))

    wr = jnp.concatenate([p['w_group'][l], p['w_router'][l].transpose(1, 0, 2).reshape(d, N_EXPERTS),
                          jnp.zeros((d, LANES - N_GROUPS - N_EXPERTS), F32)], axis=1)
    wr_hi = wr.astype(BF16)
    wr2 = jnp.stack([wr_hi, (wr - wr_hi.astype(F32)).astype(BF16)])
    br = jnp.concatenate([p['b_group'][l], p['b_router'][l].reshape(-1),
                          jnp.zeros((LANES - N_GROUPS - N_EXPERTS,), F32)]).reshape(1, LANES)
    xn3, ids, wts, cnt = _router(h2, row(p['norm_ffn'][l]), wr2, br, tm)
    pos, row_tok3, blk_e, n_used, next_e, wslot = _dispatch_plan(ids, cnt[0, :N_EXPERTS], MOE_ROWS)
    y_rows = _experts(blk_e, n_used, next_e, wslot, row_tok3, xn3, p['w_gate'][l], p['w_up'][l], p['w_down'][l])
    tc = _tile(t, COMBINE_TOK)
    return pos.reshape(t // tc, 1, 2 * tc), y_rows, wts, h2, tc


def kernel(x, mem, norm_mix, w_in, gla_w_lr_f, gla_b_lr_f, gla_w_lr_b, gla_b_lr_b, gla_norm, conv_w, conv_b, m_wq, m_wk, m_wv, m_gate_bias, m_norm, w_branch_a, w_branch_b, w_mix_out, norm_xattn, norm_mem, w_xq, w_xkv, w_xo, norm_ffn, w_group, b_group, w_router, b_router, w_gate, w_up, w_down, norm_final):
    p = dict(norm_mix=norm_mix, w_in=w_in, gla_w_lr_f=gla_w_lr_f, gla_b_lr_f=gla_b_lr_f, gla_w_lr_b=gla_w_lr_b,
             gla_b_lr_b=gla_b_lr_b, gla_norm=gla_norm, conv_w=conv_w, conv_b=conv_b, m_wq=m_wq, m_wk=m_wk,
             m_wv=m_wv, m_gate_bias=m_gate_bias, m_norm=m_norm, w_branch_a=w_branch_a, w_branch_b=w_branch_b,
             w_mix_out=w_mix_out, norm_xattn=norm_xattn, norm_mem=norm_mem, w_xq=w_xq, w_xkv=w_xkv, w_xo=w_xo,
             norm_ffn=norm_ffn, w_group=w_group, b_group=b_group, w_router=w_router, b_router=b_router,
             w_gate=w_gate, w_up=w_up, w_down=w_down)
    bsz, seq, d = x.shape
    depth = norm_mix.shape[0]
    assert depth == 1, "the final norm is fused into the last layer's combine step"
    pos3, y_rows, wts, h2, tc = _layer(x, mem, p, 0)
    out = _combine(pos3, y_rows, wts, h2, norm_final.reshape(1, d), tc)
    return out.reshape(bsz, seq, d)
```

```python
import functools

import jax
import jax.numpy as jnp
from jax import lax
from jax.experimental import pallas as pl
from jax.experimental.pallas import tpu as pltpu

F32 = jnp.float32
BF16 = jnp.bfloat16

EPS = 1e-6
LOG2_E = 1.4426950408889634
D_MODEL = 2048

GLA_HEADS = 4
GLA_DK = 128
GLA_DV = 256
GLA_K = GLA_HEADS * GLA_DK
GLA_V = GLA_HEADS * GLA_DV
GLA_RANK = 16
GLA_TAU = 16.0
GLA_CHUNK = 64
GLA_SUB = 16
GLA_SEQS = 4
GLA_SAFE_LOG2 = 60.0

MLSTM_HEADS = 4
MLSTM_DH = 256
MLSTM_W = MLSTM_HEADS * MLSTM_DH
CONV_WIDTH = 5
MLSTM_CHUNK = 256
MLSTM_SEQS = 1
CONV_HALO = 16

XATTN_HEADS = 4
XATTN_DH = D_MODEL // XATTN_HEADS

N_GROUPS = 4
EXPERTS_PER_GROUP = 8
N_EXPERTS = N_GROUPS * EXPERTS_PER_GROUP
D_EXPERT = 512
MOE_ROWS = 256
COMBINE_TOK = 256
ROUTE_COLS = 8

LANES = 128
SLAB_W = LANES
SLAB = D_MODEL // SLAB_W
SLAB_PITCH = SLAB + 8
SLAB_DTYPE = F32
GATHER_SLOTS = 3
GATHER_BATCHES = 8

OFF_GA, OFF_GB = 0, 2048
OFF_Q, OFF_K, OFF_V, OFF_GG = 4096, 4608, 5120, 6144
OFF_MX, OFF_MZ = 7168, 8192
N_BIG = 9216
SM_LRF, SM_LRB, SM_GATES = 0, 16, 32

VMEM_LIMIT = 56 * 1024 * 1024

ROW_TILE = 1024
IN_PROJ_COLS = 2304
KV_COLS = 1024
MERGE_MIX_ROWS = 512
XATTN_ROWS = 1024
CONV_ROWS = 4096


def _cparams(sem):
    return pltpu.CompilerParams(dimension_semantics=sem, vmem_limit_bytes=VMEM_LIMIT)


def _rms(x, g):
    return x * lax.rsqrt(jnp.mean(x * x, axis=-1, keepdims=True) + EPS) * g


def _log_sigmoid(x):
    return jnp.minimum(x, 0.0) - jnp.log(1.0 + jnp.exp(-jnp.abs(x)))


def _dot(a, b):
    return jnp.dot(a, b, preferred_element_type=F32)


def _dot_nt(a, b):
    return lax.dot_general(a, b, (((1,), (1,)), ((), ())), preferred_element_type=F32)


def _dot_tn(a, b):
    return lax.dot_general(a, b, (((0,), (0,)), ((), ())), preferred_element_type=F32)


def _order_mask(c, rev):
    t = lax.broadcasted_iota(jnp.int32, (c, c), 0)
    s = lax.broadcasted_iota(jnp.int32, (c, c), 1)
    return (s >= t) if rev else (s <= t)


def _split3(x):
    hi = x.astype(BF16)
    r1 = x - hi.astype(F32)
    mid = r1.astype(BF16)
    return hi, mid, (r1 - mid.astype(F32)).astype(BF16)


def _cumsum_mm(mask_bf16, x):
    hi, mid, lo = _split3(x)
    return _dot(mask_bf16, hi) + _dot(mask_bf16, mid) + _dot(mask_bf16, lo)


def _in_proj_kernel(x_ref, g_ref, wbig_ref, wsm_ref, big_ref, sm_ref, xn_ref):
    @pl.when(pl.program_id(1) == 0)
    def _():
        xn = _rms(x_ref[...], g_ref[...]).astype(BF16)
        xn_ref[...] = xn
        sm_ref[...] = _dot(xn, wsm_ref[...])
    big_ref[...] = _dot(xn_ref[...], wbig_ref[...]).astype(BF16)


def _in_proj(x2, g, w_big, w_small, tm, tn):
    t, d = x2.shape
    n = w_big.shape[1]
    return pl.pallas_call(
        _in_proj_kernel,
        grid=(t // tm, n // tn),
        in_specs=[
            pl.BlockSpec((tm, d), lambda i, j: (i, 0)),
            pl.BlockSpec((1, d), lambda i, j: (0, 0)),
            pl.BlockSpec((d, tn), lambda i, j: (0, j)),
            pl.BlockSpec((d, LANES), lambda i, j: (0, 0)),
        ],
        out_specs=[
            pl.BlockSpec((tm, tn), lambda i, j: (i, j)),
            pl.BlockSpec((tm, LANES), lambda i, j: (i, 0)),
        ],
        out_shape=[jax.ShapeDtypeStruct((t, n), BF16), jax.ShapeDtypeStruct((t, LANES), F32)],
        scratch_shapes=[pltpu.VMEM((tm, d), BF16)],
        compiler_params=_cparams(("parallel", "arbitrary")),
        name="in_proj",
    )(x2, g, w_big, w_small)


def _gla_scores_any_decay(bh, qh, kh, kh_b, rev, c):
    sb = GLA_SUB
    n_sub = c // sb
    col = lax.broadcasted_iota(jnp.int32, (sb, c), 1)
    trow = lax.broadcasted_iota(jnp.int32, (sb, c), 0)
    slabs = [(qh[i * sb:(i + 1) * sb]
              * jnp.exp2(bh[i * sb:(i + 1) * sb] - bh[i * sb + s:i * sb + s + 1, :])).astype(BF16)
             for i in range(n_sub) for s in range(sb)]
    g = _dot_nt(jnp.concatenate(slabs, axis=0), kh_b)
    rows = []
    for i in range(n_sub):
        r0 = i * sb
        ref_row = r0 + (sb - 1 if rev else 0)
        beta = bh[ref_row:ref_row + 1, :]
        qt = (qh[r0:r0 + sb] * jnp.exp2(bh[r0:r0 + sb] - beta)).astype(BF16)
        kt = (kh * jnp.exp2(beta - bh)).astype(BF16)
        a_off = _dot_nt(qt, kt)
        a_diag = jnp.zeros((sb, c), F32)
        for s in range(sb):
            g0 = (i * sb + s) * sb
            a_diag = jnp.where(col == r0 + s, g[g0:g0 + sb], a_diag)
        tr = trow + r0
        if rev:
            off_mask = col >= r0 + sb
            diag_mask = (col >= tr) & (col < r0 + sb)
        else:
            off_mask = col < r0
            diag_mask = (col <= tr) & (col >= r0)
        rows.append(jnp.where(off_mask, a_off, 0.0) + jnp.where(diag_mask, a_diag, 0.0))
    return jnp.concatenate(rows, axis=0).astype(BF16)


def _gla_kernel(q_ref, k_ref, v_ref, sm_ref, wlr_ref, blr_ref, *rest, rev, final, c, nb):
    if final:
        oprev_ref, gg_ref, gn_ref, o_ref, st_ref, a_ref = rest
    else:
        o_ref, st_ref, a_ref = rest

    @pl.when(pl.program_id(1) == 0)
    def _():
        st_ref[...] = jnp.zeros_like(st_ref)

    mask = _order_mask(c, rev)
    mask_b = jnp.where(mask, 1.0, 0.0).astype(BF16)
    last = 0 if rev else c - 1

    for bi in range(nb):
        x = _dot(sm_ref[bi].astype(BF16), wlr_ref[...]) + blr_ref[...]
        la = _log_sigmoid(x) * (1.0 / GLA_TAU)
        b = _cumsum_mm(mask_b, la) * LOG2_E
        tot = b[last:last + 1, :]
        q = q_ref[bi].astype(F32) * (GLA_DK ** -0.5)
        k = k_ref[bi].astype(F32)
        q_in = (q * jnp.exp2(b)).astype(BF16)
        k_dec = (k * jnp.exp2(tot - b)).astype(BF16)
        e_tot = jnp.exp2(tot)

        mild = jnp.min(tot) > -GLA_SAFE_LOG2

        @pl.when(mild)
        def _():
            q_up = (q * jnp.exp2(b - tot)).astype(BF16)
            for h in range(GLA_HEADS):
                ks = slice(h * GLA_DK, (h + 1) * GLA_DK)
                a_ref[bi, h] = jnp.where(mask, _dot_nt(q_up[:, ks], k_dec[:, ks]), 0.0).astype(BF16)

        @pl.when(jnp.logical_not(mild))
        def _():
            for h in range(GLA_HEADS):
                ks = slice(h * GLA_DK, (h + 1) * GLA_DK)
                a_ref[bi, h] = _gla_scores_any_decay(b[:, ks], q[:, ks], k[:, ks], k_ref[bi, :, ks], rev, c)

        for h in range(GLA_HEADS):
            ks = slice(h * GLA_DK, (h + 1) * GLA_DK)
            vs = slice(h * GLA_DV, (h + 1) * GLA_DV)
            vh = v_ref[bi, :, vs]
            st = st_ref[bi, h]
            o_inter = _dot_nt(q_in[:, ks], st.astype(BF16))
            st_ref[bi, h] = st * e_tot[:, ks] + _dot_tn(vh, k_dec[:, ks])
            o = o_inter + _dot(a_ref[bi, h], vh)
            if final:
                o = o + oprev_ref[bi, :, vs]
                y = _rms(o, gn_ref[:, vs])
                gg = gg_ref[bi, :, vs].astype(F32)
                o_ref[bi, :, vs] = (y * (gg * jax.nn.sigmoid(gg))).astype(o_ref.dtype)
            else:
                o_ref[bi, :, vs] = o.astype(o_ref.dtype)


def _gla_scan(big3, sm3, wlr, blr, rev, prev=None, gn=None):
    bsz, seq, _ = big3.shape
    c = GLA_CHUNK
    n = seq // c
    nb = GLA_SEQS if bsz % GLA_SEQS == 0 else 1
    final = prev is not None

    def cm(ci):
        return (n - 1 - ci) if rev else ci

    in_specs = [
        pl.BlockSpec((nb, c, GLA_K), lambda b, ci: (b, cm(ci), OFF_Q // GLA_K)),
        pl.BlockSpec((nb, c, GLA_K), lambda b, ci: (b, cm(ci), OFF_K // GLA_K)),
        pl.BlockSpec((nb, c, GLA_V), lambda b, ci: (b, cm(ci), OFF_V // GLA_V)),
        pl.BlockSpec((nb, c, LANES), lambda b, ci: (b, cm(ci), 0)),
        pl.BlockSpec((LANES, GLA_K), lambda b, ci: (0, 0)),
        pl.BlockSpec((1, GLA_K), lambda b, ci: (0, 0)),
    ]
    args = [big3, big3, big3, sm3, wlr, blr]
    if final:
        in_specs += [
            pl.BlockSpec((nb, c, GLA_V), lambda b, ci: (b, cm(ci), 0)),
            pl.BlockSpec((nb, c, GLA_V), lambda b, ci: (b, cm(ci), OFF_GG // GLA_V)),
            pl.BlockSpec((1, GLA_V), lambda b, ci: (0, 0)),
        ]
        args += [prev, big3, gn]
    return pl.pallas_call(
        functools.partial(_gla_kernel, rev=rev, final=final, c=c, nb=nb),
        grid=(bsz // nb, n),
        in_specs=in_specs,
        out_specs=pl.BlockSpec((nb, c, GLA_V), lambda b, ci: (b, cm(ci), 0)),
        out_shape=jax.ShapeDtypeStruct((bsz, seq, GLA_V), BF16 if final else F32),
        scratch_shapes=[pltpu.VMEM((nb, GLA_HEADS, GLA_DV, GLA_DK), F32),
                        pltpu.VMEM((nb, GLA_HEADS, c, c), BF16)],
        compiler_params=_cparams(("parallel", "arbitrary")),
        name="gla_bwd" if rev else "gla_fwd",
    )(*args)


def _mlstm_pre_kernel(cur_ref, prev_ref, next_ref, cw_ref, cb_ref, wq_ref, wk_ref, wv_ref,
                      q_ref, k_ref, v_ref, *, tm):
    i = pl.program_id(2)
    cur_b = cur_ref[0]
    cur = cur_b.astype(F32)
    halo = CONV_WIDTH // 2
    prev = jnp.where(i > 0, prev_ref[0].astype(F32), 0.0)
    nxt = jnp.where(i < pl.num_programs(2) - 1, next_ref[0].astype(F32), 0.0)
    ext = jnp.concatenate([prev[CONV_HALO - 8:], cur, nxt[:8]], axis=0)
    acc = jnp.zeros_like(cur) + cb_ref[...]
    for w in range(CONV_WIDTH):
        off = 8 - halo + w
        acc = acc + ext[off:off + tm] * cw_ref[w:w + 1, :]
    xc = (acc * jax.nn.sigmoid(acc)).astype(BF16)
    q_ref[0] = _dot(xc, wq_ref[0]).astype(BF16)
    k_ref[0] = (_dot(xc, wk_ref[0]) * (MLSTM_DH ** -0.5)).astype(BF16)
    v_ref[0] = _dot(cur_b, wv_ref[0]).astype(BF16)


def _mlstm_pre(big3, cw, cb, wq, wk, wv, tm):
    bsz, seq, _ = big3.shape
    dh = MLSTM_DH
    nh = tm // CONV_HALO
    n_halo = seq // CONV_HALO
    c0 = OFF_MX // dh
    out = jax.ShapeDtypeStruct((bsz, seq, MLSTM_W), BF16)
    ospec = pl.BlockSpec((1, tm, dh), lambda b, h, i: (b, i, h))
    wspec = pl.BlockSpec((1, dh, dh), lambda b, h, i: (h, 0, 0))
    return pl.pallas_call(
        functools.partial(_mlstm_pre_kernel, tm=tm),
        grid=(bsz, MLSTM_HEADS, seq // tm),
        in_specs=[
            pl.BlockSpec((1, tm, dh), lambda b, h, i: (b, i, c0 + h)),
            pl.BlockSpec((1, CONV_HALO, dh), lambda b, h, i: (b, jnp.maximum(i * nh - 1, 0), c0 + h)),
            pl.BlockSpec((1, CONV_HALO, dh), lambda b, h, i: (b, jnp.minimum((i + 1) * nh, n_halo - 1), c0 + h)),
            pl.BlockSpec((8, dh), lambda b, h, i: (0, h)),
            pl.BlockSpec((1, dh), lambda b, h, i: (0, h)),
            wspec, wspec, wspec,
        ],
        out_specs=[ospec, ospec, ospec],
        out_shape=[out, out, out],
        compiler_params=_cparams(("parallel", "parallel", "parallel")),
        name="mlstm_pre",
    )(big3, big3, big3, cw, cb, wq, wk, wv)


def _mlstm_kernel(q_ref, k_ref, v_ref, sm_ref, gb_ref, *rest, rev, final, c, nb):
    if final:
        hprev_ref, mz_ref, mn_ref, o_ref, c_ref, n_ref, m_ref = rest
    else:
        o_ref, c_ref, n_ref, m_ref = rest

    @pl.when(pl.program_id(1) == 0)
    def _():
        c_ref[...] = jnp.zeros_like(c_ref)
        n_ref[...] = jnp.zeros_like(n_ref)
        m_ref[...] = jnp.zeros_like(m_ref)

    mask = _order_mask(c, rev)
    mask_b = jnp.where(mask, 1.0, 0.0).astype(BF16)
    last = 0 if rev else c - 1
    dh = MLSTM_DH

    for bi in range(nb):
        g = sm_ref[bi] + gb_ref[...]
        bcum = _cumsum_mm(mask_b, _log_sigmoid(g))
        g_t = g.T
        b_t = bcum.T

        for h in range(MLSTM_HEADS):
            ci = SM_GATES + (2 * MLSTM_HEADS if rev else 0) + h
            cf = ci + MLSTM_HEADS
            hs = slice(h * dh, (h + 1) * dh)
            i_col, b_col = g[:, ci:ci + 1], bcum[:, cf:cf + 1]
            i_row, b_row = g_t[ci:ci + 1, :], b_t[cf:cf + 1, :]
            tot = b_col[last:last + 1, :]
            m_prev = m_ref[bi, h, 0:1, 0:1]
            qh, kh, vh = q_ref[bi, :, hs], k_ref[bi, :, hs], v_ref[bi, :, hs]
            c_st = c_ref[bi, h]
            n_st = n_ref[bi, h, 0:1, :]

            dmat = jnp.where(mask, b_col - b_row + i_row, -jnp.inf)
            inter_log = b_col + m_prev
            m_t = jnp.maximum(inter_log, jnp.max(dmat, axis=-1, keepdims=True))
            w_intra = jnp.exp(dmat - m_t)
            w_inter = jnp.exp(inter_log - m_t)
            s = _dot_nt(qh, kh) * w_intra
            num = w_inter * _dot(qh, c_st.astype(BF16)) + _dot(s.astype(BF16), vh)
            den = (w_inter * jnp.sum(qh.astype(F32) * n_st, axis=-1, keepdims=True)
                   + jnp.sum(s, axis=-1, keepdims=True))
            hh = num / jnp.maximum(jnp.abs(den), jnp.exp(-m_t))

            upd_col = tot - b_col + i_col
            upd_row = tot - b_row + i_row
            m_new = jnp.maximum(tot + m_prev, jnp.max(upd_row, axis=-1, keepdims=True))
            w_old = jnp.exp(tot + m_prev - m_new)
            kw = kh.astype(F32) * jnp.exp(upd_col - m_new)
            c_ref[bi, h] = w_old * c_st + _dot_tn(kw.astype(BF16), vh)
            n_ref[bi, h] = jnp.broadcast_to(w_old * n_st + jnp.sum(kw, axis=0, keepdims=True), (8, dh))
            m_ref[bi, h] = jnp.broadcast_to(m_new, (8, LANES))

            if final:
                hh = hh + hprev_ref[bi, :, hs]
                y = _rms(hh, mn_ref[:, hs])
                o_ref[bi, :, hs] = (y * jax.nn.sigmoid(mz_ref[bi, :, hs].astype(F32))).astype(o_ref.dtype)
            else:
                o_ref[bi, :, hs] = hh.astype(o_ref.dtype)


def _mlstm_scan(qm, km, vm, sm3, gbias, rev, prev=None, big3=None, mn=None):
    bsz, seq, w = qm.shape
    c = min(MLSTM_CHUNK, seq)
    n = seq // c
    nb = MLSTM_SEQS if bsz % MLSTM_SEQS == 0 else 1
    final = prev is not None

    def cm(ci):
        return (n - 1 - ci) if rev else ci

    xspec = pl.BlockSpec((nb, c, w), lambda b, ci: (b, cm(ci), 0))
    in_specs = [xspec, xspec, xspec,
                pl.BlockSpec((nb, c, LANES), lambda b, ci: (b, cm(ci), 0)),
                pl.BlockSpec((1, LANES), lambda b, ci: (0, 0))]
    args = [qm, km, vm, sm3, gbias]
    if final:
        in_specs += [xspec,
                     pl.BlockSpec((nb, c, w), lambda b, ci: (b, cm(ci), OFF_MZ // MLSTM_W)),
                     pl.BlockSpec((1, w), lambda b, ci: (0, 0))]
        args += [prev, big3, mn]
    return pl.pallas_call(
        functools.partial(_mlstm_kernel, rev=rev, final=final, c=c, nb=nb),
        grid=(bsz // nb, n),
        in_specs=in_specs,
        out_specs=xspec,
        out_shape=jax.ShapeDtypeStruct((bsz, seq, w), BF16 if final else F32),
        scratch_shapes=[pltpu.VMEM((nb, MLSTM_HEADS, MLSTM_DH, MLSTM_DH), F32),
                        pltpu.VMEM((nb, MLSTM_HEADS, 8, MLSTM_DH), F32),
                        pltpu.VMEM((nb, MLSTM_HEADS, 8, LANES), F32)],
        compiler_params=_cparams(("parallel", "arbitrary")),
        name="mlstm_bwd" if rev else "mlstm_fwd",
    )(*args)


def _merge_mix_kernel(ya_ref, yb_ref, wa_ref, wb_ref, ga_ref, gb_ref, wo_ref, x_ref, o_ref):
    a = _dot(ya_ref[...], wa_ref[...])
    b = _dot(yb_ref[...], wb_ref[...])
    ga = jax.nn.sigmoid(ga_ref[...].astype(F32))
    gb = jax.nn.sigmoid(gb_ref[...].astype(F32))
    merged = (ga * a + gb * b).astype(BF16)
    o_ref[...] = x_ref[...] + _dot(merged, wo_ref[...])


def _merge_mix(ya, yb, wa, wb, big, wo, x2, tm):
    t, kdim = ya.shape
    n = wa.shape[1]

    def resident(shape):
        return pl.BlockSpec(shape, lambda i: (0, 0), pipeline_mode=pl.Buffered(1))

    return pl.pallas_call(
        _merge_mix_kernel,
        grid=(t // tm,),
        in_specs=[
            pl.BlockSpec((tm, kdim), lambda i: (i, 0)),
            pl.BlockSpec((tm, kdim), lambda i: (i, 0)),
            resident((kdim, n)),
            resident((kdim, n)),
            pl.BlockSpec((tm, n), lambda i: (i, OFF_GA // n)),
            pl.BlockSpec((tm, n), lambda i: (i, OFF_GB // n)),
            resident((n, n)),
            pl.BlockSpec((tm, n), lambda i: (i, 0)),
        ],
        out_specs=pl.BlockSpec((tm, n), lambda i: (i, 0)),
        out_shape=jax.ShapeDtypeStruct((t, n), F32),
        compiler_params=_cparams(("parallel",)),
        name="merge_mix",
    )(ya, yb, wa, wb, big, big, wo, x2)


def _kv_kernel(mem_ref, g_ref, w_ref, o_ref):
    mn = _rms(mem_ref[...], g_ref[...]).astype(BF16)
    o_ref[...] = _dot(mn, w_ref[...]).astype(BF16)


def _kv_proj(mem2, g, w, tn):
    r, d = mem2.shape
    n = w.shape[1]
    return pl.pallas_call(
        _kv_kernel,
        grid=(n // tn,),
        in_specs=[
            pl.BlockSpec((r, d), lambda j: (0, 0)),
            pl.BlockSpec((1, d), lambda j: (0, 0)),
            pl.BlockSpec((d, tn), lambda j: (0, j)),
        ],
        out_specs=pl.BlockSpec((r, tn), lambda j: (0, j)),
        out_shape=jax.ShapeDtypeStruct((r, n), BF16),
        compiler_params=_cparams(("parallel",)),
        name="kv_proj",
    )(mem2, g, w)


def _xattn_kernel(h_ref, g_ref, wq_ref, k_ref, v_ref, wo_ref, o_ref, xn_ref):
    @pl.when(pl.program_id(1) == 0)
    def _():
        x = h_ref[...]
        xn_ref[...] = _rms(x, g_ref[...]).astype(BF16)
        o_ref[...] = x
    q = _dot(xn_ref[...], wq_ref[...]).astype(BF16)
    s = _dot_nt(q, k_ref[0]) * (XATTN_DH ** -0.5)
    p = jnp.exp(s - jnp.max(s, axis=-1, keepdims=True))
    p = p / jnp.sum(p, axis=-1, keepdims=True)
    o = _dot(p.astype(BF16), v_ref[0]).astype(BF16)
    o_ref[...] += _dot(o, wo_ref[...])


def _xattn(h1, g, wq, kv3, wo, seq, tm):
    t, d = h1.shape
    n_mem = kv3.shape[1]
    dh = XATTN_DH
    per_b = seq // tm
    return pl.pallas_call(
        _xattn_kernel,
        grid=(t // tm, XATTN_HEADS),
        in_specs=[
            pl.BlockSpec((tm, d), lambda i, h: (i, 0)),
            pl.BlockSpec((1, d), lambda i, h: (0, 0)),
            pl.BlockSpec((d, dh), lambda i, h: (0, h)),
            pl.BlockSpec((1, n_mem, dh), lambda i, h: (i // per_b, 0, h)),
            pl.BlockSpec((1, n_mem, dh), lambda i, h: (i // per_b, 0, XATTN_HEADS + h)),
            pl.BlockSpec((dh, d), lambda i, h: (h, 0)),
        ],
        out_specs=pl.BlockSpec((tm, d), lambda i, h: (i, 0)),
        out_shape=jax.ShapeDtypeStruct((t, d), F32),
        scratch_shapes=[pltpu.VMEM((tm, d), BF16)],
        compiler_params=_cparams(("parallel", "arbitrary")),
        name="xattn",
    )(h1, g, wq, kv3, kv3, wo)


def _slab_store(ref, val):
    n = val.shape[0]
    for c in range(SLAB):
        ref[pl.ds(c, n, stride=SLAB), :] = val[:, c * SLAB_W:(c + 1) * SLAB_W].astype(ref.dtype)


def _slab_load(buf, slot, first, n):
    return jnp.concatenate(
        [buf[slot, pl.ds(first * SLAB_PITCH + c, n, stride=SLAB_PITCH), :] for c in range(SLAB)], axis=1)


def _slab_copy(src_hbm, dst_buf, sem, slot, src_slab, dst_slab):
    return pltpu.make_async_copy(src_hbm.at[pl.ds(src_slab * SLAB, SLAB)],
                                 dst_buf.at[slot, pl.ds(dst_slab * SLAB_PITCH, SLAB)], sem.at[slot])


def _zero_after(v):
    bits = lax.bitcast_convert_type(v[-1:, -1:], jnp.uint32)
    return ((bits >> 16) >> 16).astype(jnp.int32)[0, 0]


def _slab_wait(src_hbm, dst_buf, sem, slot, n):
    pltpu.make_async_copy(src_hbm.at[pl.ds(0, n * SLAB)], dst_buf.at[slot, pl.ds(0, n * SLAB)], sem.at[slot]).wait()


def _router_kernel(h_ref, g_ref, wr_ref, br_ref, xn_ref, ids_ref, wts_ref, cnt_ref, carry_ref):
    @pl.when(pl.program_id(0) == 0)
    def _():
        carry_ref[...] = jnp.zeros_like(carry_ref)

    xn = _rms(h_ref[...], g_ref[...])
    _slab_store(xn_ref, xn)
    hi = xn.astype(BF16)
    lo = (xn - hi.astype(F32)).astype(BF16)
    lg = _dot(hi, wr_ref[0]) + _dot(hi, wr_ref[1]) + _dot(lo, wr_ref[0]) + br_ref[...]
    lane = lax.broadcasted_iota(jnp.int32, lg.shape, 1).astype(F32)
    ninf = -jnp.inf
    big_lane = float(LANES)

    def first_max(v):
        mx = jnp.max(v, axis=-1, keepdims=True)
        return mx, jnp.min(jnp.where(v == mx, lane, big_lane), axis=-1, keepdims=True)

    gl = jnp.where(lane < N_GROUPS, lg, ninf)
    gmax, g_sel = first_max(gl)
    p_g = 1.0 / jnp.sum(jnp.exp(gl - gmax), axis=-1, keepdims=True)
    lo_l = N_GROUPS + EXPERTS_PER_GROUP * g_sel
    el = jnp.where((lane >= lo_l) & (lane < lo_l + EXPERTS_PER_GROUP), lg, ninf)
    emax, i1 = first_max(el)
    max2, i2 = first_max(jnp.where(lane == i1, ninf, el))
    e2 = jnp.exp(max2 - emax)
    w1 = p_g / (1.0 + e2)
    w2 = p_g * e2 / (1.0 + e2)
    e1, e2 = i1 - N_GROUPS, i2 - N_GROUPS
    oh1, oh2 = lane == e1, lane == e2
    oh = jnp.where(oh1, 1.0, 0.0) + jnp.where(oh2, 1.0, 0.0)
    tm = lg.shape[0]
    earlier = lax.broadcasted_iota(jnp.int32, (tm, tm), 1) < lax.broadcasted_iota(jnp.int32, (tm, tm), 0)
    carry = carry_ref[0:1, :]
    before = _dot(jnp.where(earlier, 1.0, 0.0).astype(BF16), oh.astype(BF16)) + carry
    r1 = jnp.sum(jnp.where(oh1, before, 0.0), axis=-1, keepdims=True)
    r2 = jnp.sum(jnp.where(oh2, before, 0.0), axis=-1, keepdims=True)
    counts = carry + jnp.sum(oh, axis=0, keepdims=True)
    carry_ref[...] = jnp.broadcast_to(counts, carry_ref.shape)
    cnt_ref[...] = jnp.broadcast_to(counts, cnt_ref.shape).astype(jnp.int32)
    ids = jnp.where(lane == 0.0, e1, jnp.where(lane == 1.0, e2, jnp.where(lane == 2.0, r1,
                                                                         jnp.where(lane == 3.0, r2, 0.0))))
    ids_ref[...] = ids.astype(jnp.int32)[:, 0:ROUTE_COLS]
    wts_ref[...] = jnp.where(lane == 0.0, w1, jnp.where(lane == 1.0, w2, 0.0))


def _router(h2, g, wr, br, tm):
    t, d = h2.shape
    return pl.pallas_call(
        _router_kernel,
        grid=(t // tm,),
        in_specs=[
            pl.BlockSpec((tm, d), lambda i: (i, 0)),
            pl.BlockSpec((1, d), lambda i: (0, 0)),
            pl.BlockSpec((2, d, LANES), lambda i: (0, 0, 0)),
            pl.BlockSpec((1, LANES), lambda i: (0, 0)),
        ],
        out_specs=[
            pl.BlockSpec((tm * SLAB, SLAB_W), lambda i: (i, 0)),
            pl.BlockSpec((tm, ROUTE_COLS), lambda i: (i, 0)),
            pl.BlockSpec((tm, LANES), lambda i: (i, 0)),
            pl.BlockSpec((8, LANES), lambda i: (0, 0)),
        ],
        out_shape=[jax.ShapeDtypeStruct((t * SLAB, SLAB_W), SLAB_DTYPE),
                   jax.ShapeDtypeStruct((t, ROUTE_COLS), jnp.int32),
                   jax.ShapeDtypeStruct((t, LANES), F32),
                   jax.ShapeDtypeStruct((8, LANES), jnp.int32)],
        scratch_shapes=[pltpu.VMEM((8, LANES), F32)],
        compiler_params=_cparams(("arbitrary",)),
        name="router",
    )(h2, g, wr, br)


def _experts_kernel(blk_e_ref, n_used_ref, next_e_ref, wslot_ref, tok0_ref, tok1_ref, tok2_ref, x_hbm,
                    wg_hbm, wu_hbm, wd_hbm, y_ref, xbuf, wg_f, wu_f, wd_f, wg_b, wu_b, wd_b, sem, wsem, *, rows):
    j = pl.program_id(0)
    n_used = n_used_ref[0]
    used = j < n_used
    slot = j % GATHER_SLOTS
    slot1 = (j + 1) % GATHER_SLOTS
    slot2 = (j + 2) % GATHER_SLOTS

    @pl.when(j == 0)
    def _():
        def start(r, carry):
            _slab_copy(x_hbm, xbuf, sem, 0, tok0_ref[0, 0, r], r).start()
            _slab_copy(x_hbm, xbuf, sem, 1, tok1_ref[0, 0, r], r).start()
            return carry
        lax.fori_loop(0, rows, start, 0)

    @pl.when(jnp.logical_not(used))
    def _():
        @pl.when(j == n_used)
        def _():
            _slab_wait(x_hbm, xbuf, sem, slot, rows)
            _slab_wait(x_hbm, xbuf, sem, slot1, rows)
        y_ref[...] = jnp.zeros_like(y_ref)

    def weight_copies(e, ws):
        return [pltpu.make_async_copy(src.at[e], dst.at[ws], wsem.at[ws])
                for src, dst in ((wg_hbm, wg_f), (wu_hbm, wu_f), (wd_hbm, wd_f))]

    @pl.when(j == 0)
    def _():
        for cp in weight_copies(blk_e_ref[0], 0):
            cp.start()

    @pl.when(used)
    def _():
        _slab_wait(x_hbm, xbuf, sem, slot, rows)

        @pl.when((j == 0) | (blk_e_ref[j] != blk_e_ref[jnp.maximum(j - 1, 0)]))
        def _():
            ws = wslot_ref[j]
            for cp in weight_copies(0, ws):
                cp.wait()

            @pl.when(next_e_ref[j] >= 0)
            def _():
                for cp in weight_copies(next_e_ref[j], 1 - ws):
                    cp.start()

            wg_b[...] = wg_f[ws].astype(BF16)
            wu_b[...] = wu_f[ws].astype(BF16)
            wd_b[...] = wd_f[ws].astype(BF16)

        batch = rows // GATHER_BATCHES
        issued = [0]

        def gather_batch(after):
            zero = 0 if after is None else _zero_after(after)
            for r in range(issued[0] * batch, (issued[0] + 1) * batch):
                _slab_copy(x_hbm, xbuf, sem, slot2, tok2_ref[0, 0, r] + zero, r).start()
            issued[0] += 1

        xb = _slab_load(xbuf, slot, 0, rows).astype(BF16)
        gather_batch(None)
        de = wg_b.shape[1]
        halves = [slice(0, de // 2), slice(de // 2, de)]
        gate, up = [], []
        for cs in halves:
            gate.append(_dot(xb, wg_b[:, cs]))
            gather_batch(gate[-1])
        for cs in halves:
            up.append(_dot(xb, wu_b[:, cs]))
            gather_batch(up[-1])
        gate = jnp.concatenate(gate, axis=1)
        act = (gate * jax.nn.sigmoid(gate) * jnp.concatenate(up, axis=1)).astype(BF16)
        wo = de
        per = wo // SLAB_W
        for c in range(wd_b.shape[1] // wo):
            yc = _dot(act, wd_b[:, c * wo:(c + 1) * wo])
            for cc in range(per):
                y_ref[pl.ds(c * per + cc, rows, stride=SLAB), :] = (
                    yc[:, cc * SLAB_W:(cc + 1) * SLAB_W].astype(y_ref.dtype))
            if issued[0] < GATHER_BATCHES:
                gather_batch(yc)
        assert issued[0] == GATHER_BATCHES

        @pl.when(j == pl.num_programs(0) - 1)
        def _():
            _slab_wait(x_hbm, xbuf, sem, slot1, rows)
            _slab_wait(x_hbm, xbuf, sem, slot2, rows)


def _experts(blk_e, n_used, next_e, wslot, row_tok3, xn_slab, wg, wu, wd):
    n_blocks, _, rows = row_tok3.shape
    _, d, de = wg.shape

    def tok_spec(ahead):
        return pl.BlockSpec((1, 1, rows), lambda j, *_: (jnp.minimum(j + ahead, n_blocks - 1), 0, 0),
                            memory_space=pltpu.SMEM)

    hbm = pl.BlockSpec(memory_space=pl.ANY)
    grid_spec = pltpu.PrefetchScalarGridSpec(
        num_scalar_prefetch=4,
        grid=(n_blocks,),
        in_specs=[tok_spec(0), tok_spec(1), tok_spec(2), hbm, hbm, hbm, hbm],
        out_specs=pl.BlockSpec((rows * SLAB, SLAB_W), lambda j, *_: (j, 0)),
        scratch_shapes=[pltpu.VMEM((GATHER_SLOTS, rows * SLAB_PITCH, SLAB_W), SLAB_DTYPE),
                        pltpu.VMEM((2, d, de), F32), pltpu.VMEM((2, d, de), F32), pltpu.VMEM((2, de, d), F32),
                        pltpu.VMEM((d, de), BF16), pltpu.VMEM((d, de), BF16), pltpu.VMEM((de, d), BF16),
                        pltpu.SemaphoreType.DMA((GATHER_SLOTS,)), pltpu.SemaphoreType.DMA((2,))],
    )
    return pl.pallas_call(
        functools.partial(_experts_kernel, rows=rows),
        grid_spec=grid_spec,
        out_shape=jax.ShapeDtypeStruct((n_blocks * rows * SLAB, SLAB_W), SLAB_DTYPE),
        compiler_params=_cparams(("arbitrary",)),
        name="experts",
    )(blk_e, n_used, next_e, wslot, row_tok3, row_tok3, row_tok3, xn_slab, wg, wu, wd)


def _combine_kernel(pos0_ref, pos1_ref, pos2_ref, y_hbm, wts_ref, h_ref, g_ref, o_ref, ybuf, sem, *, tc):
    j = pl.program_id(0)
    slot = j % GATHER_SLOTS
    slot1 = (j + 1) % GATHER_SLOTS
    slot2 = (j + 2) % GATHER_SLOTS

    @pl.when(j == 0)
    def _():
        def start(r, carry):
            for s, idx in ((0, pos0_ref), (1, pos1_ref)):
                _slab_copy(y_hbm, ybuf, sem, s, idx[0, 0, 2 * r], r).start()
                _slab_copy(y_hbm, ybuf, sem, s, idx[0, 0, 2 * r + 1], tc + r).start()
            return carry
        lax.fori_loop(0, tc, start, 0)

    _slab_wait(y_hbm, ybuf, sem, slot, 2 * tc)
    for r in range(tc):
        _slab_copy(y_hbm, ybuf, sem, slot2, pos2_ref[0, 0, 2 * r], r).start(priority=0)
        _slab_copy(y_hbm, ybuf, sem, slot2, pos2_ref[0, 0, 2 * r + 1], tc + r).start(priority=1)
    w = wts_ref[...]
    y = (_slab_load(ybuf, slot, 0, tc).astype(F32) * w[:, 0:1]
         + _slab_load(ybuf, slot, tc, tc).astype(F32) * w[:, 1:2])
    o_ref[...] = _rms(h_ref[...] + y, g_ref[...])

    @pl.when(j == pl.num_programs(0) - 1)
    def _():
        _slab_wait(y_hbm, ybuf, sem, slot1, 2 * tc)
        _slab_wait(y_hbm, ybuf, sem, slot2, 2 * tc)


def _combine(pos3, y_slab, wts, h2, g, tc):
    t, d = h2.shape
    n = t // tc

    def pos_spec(ahead):
        return pl.BlockSpec((1, 1, 2 * tc), lambda i: (jnp.minimum(i + ahead, n - 1), 0, 0),
                            memory_space=pltpu.SMEM)

    return pl.pallas_call(
        functools.partial(_combine_kernel, tc=tc),
        grid=(n,),
        in_specs=[
            pos_spec(0), pos_spec(1), pos_spec(2),
            pl.BlockSpec(memory_space=pl.ANY),
            pl.BlockSpec((tc, LANES), lambda i: (i, 0)),
            pl.BlockSpec((tc, d), lambda i: (i, 0)),
            pl.BlockSpec((1, d), lambda i: (0, 0)),
        ],
        out_specs=pl.BlockSpec((tc, d), lambda i: (i, 0)),
        out_shape=jax.ShapeDtypeStruct((t, d), F32),
        scratch_shapes=[pltpu.VMEM((GATHER_SLOTS, 2 * tc * SLAB_PITCH, SLAB_W), SLAB_DTYPE),
                        pltpu.SemaphoreType.DMA((GATHER_SLOTS,))],
        compiler_params=_cparams(("arbitrary",)),
        name="combine",
    )(pos3, pos3, pos3, y_slab, wts, h2, g)


def _dispatch_plan(ids, counts, rows):
    flat_e = ids[:, 0:2].reshape(-1)
    rank = ids[:, 2:4].reshape(-1)
    n_asg = flat_e.shape[0]
    experts = jnp.arange(N_EXPERTS, dtype=jnp.int32)
    padded = ((counts + rows - 1) // rows) * rows
    pends = jnp.cumsum(padded)
    pstarts = pends - padded
    pos = jnp.sum(jnp.where(flat_e[:, None] == experts[None, :], pstarts[None, :], 0), axis=1) + rank
    n_rows = n_asg + N_EXPERTS * rows
    n_blocks = n_rows // rows
    row_tok = jnp.zeros((n_rows,), jnp.int32).at[pos].set(jnp.arange(n_asg, dtype=jnp.int32) // 2)
    starts = jnp.arange(n_blocks, dtype=jnp.int32) * rows
    blk_e = jnp.minimum(jnp.sum((pends[None, :] <= starts[:, None]).astype(jnp.int32), axis=1), N_EXPERTS - 1)
    n_used = (pends[-1] // rows).astype(jnp.int32).reshape(1)
    has = counts > 0
    succ = jnp.min(jnp.where((experts[None, :] > experts[:, None]) & has[None, :], experts[None, :], N_EXPERTS),
                   axis=1)
    succ = jnp.where(succ < N_EXPERTS, succ, -1)
    before = jnp.sum((has[None, :] & (experts[None, :] < blk_e[:, None])).astype(jnp.int32), axis=1)
    return (pos.astype(jnp.int32), row_tok.reshape(n_blocks, 1, rows), blk_e.astype(jnp.int32), n_used,
            succ[blk_e].astype(jnp.int32), (before % 2).astype(jnp.int32))


def _tile(n, pref):
    return pref if n % pref == 0 else n


def _layer(h, mem, p, l):
    bsz, seq, d = h.shape
    t = bsz * seq
    x2 = h.reshape(t, d)
    row = lambda v: v.reshape(1, -1)
    tm = _tile(t, ROW_TILE)

    w_in = p['w_in'][l]
    w_big = jnp.concatenate([w_in[:, 5168:9264], w_in[:, 0:3072], w_in[:, 3104:5152]], axis=1).astype(BF16)
    w_small = jnp.concatenate([w_in[:, 3072:3104], w_in[:, 5152:5168],
                               jnp.zeros((d, LANES - 48), F32)], axis=1).astype(BF16)
    big, small = _in_proj(x2, row(p['norm_mix'][l]), w_big, w_small, tm, IN_PROJ_COLS)
    big3 = big.reshape(bsz, seq, N_BIG)
    sm3 = small.reshape(bsz, seq, LANES)

    def lr_pad(w, off):
        return jnp.zeros((LANES, GLA_K), F32).at[off:off + GLA_RANK].set(w).astype(BF16)

    o_f = _gla_scan(big3, sm3, lr_pad(p['gla_w_lr_f'][l], SM_LRF), row(p['gla_b_lr_f'][l]), rev=False)
    y_a = _gla_scan(big3, sm3, lr_pad(p['gla_w_lr_b'][l], SM_LRB), row(p['gla_b_lr_b'][l]), rev=True,
                    prev=o_f, gn=row(p['gla_norm'][l]))

    cw = jnp.zeros((8, MLSTM_W), F32).at[:CONV_WIDTH].set(p['conv_w'][l].reshape(CONV_WIDTH, MLSTM_W))
    qm, km, vm = _mlstm_pre(big3, cw, row(p['conv_b'][l]), p['m_wq'][l].astype(BF16),
                            p['m_wk'][l].astype(BF16), p['m_wv'][l].astype(BF16), _tile(seq, CONV_ROWS))
    gbias = jnp.zeros((1, LANES), F32).at[0, SM_GATES:SM_GATES + 4 * MLSTM_HEADS].set(
        p['m_gate_bias'][l].reshape(-1))
    h_f = _mlstm_scan(qm, km, vm, sm3, gbias, rev=False)
    y_b = _mlstm_scan(qm, km, vm, sm3, gbias, rev=True, prev=h_f, big3=big3, mn=row(p['m_norm'][l]))

    assert OFF_GA % d == 0 and OFF_GB % d == 0
    h1 = _merge_mix(y_a.reshape(t, GLA_V), y_b.reshape(t, MLSTM_W), p['w_branch_a'][l].astype(BF16),
                    p['w_branch_b'][l].astype(BF16), big, p['w_mix_out'][l].astype(BF16), x2,
                    _tile(t, MERGE_MIX_ROWS))

    n_mem = mem.shape[1]
    kv = _kv_proj(mem.reshape(bsz * n_mem, d), row(p['norm_mem'][l]), p['w_xkv'][l].astype(BF16), KV_COLS)
    h2 = _xattn(h1, row(p['norm_xattn'][l]), p['w_xq'][l].astype(BF16), kv.reshape(bsz, n_mem, 2 * d),
                p['w_xo'][l].astype(BF16), seq, _tile(seq, XATTN_ROWS))

    wr = jnp.concatenate([p['w_group'][l], p['w_router'][l].transpose(1, 0, 2).reshape(d, N_EXPERTS),
                          jnp.zeros((d, LANES - N_GROUPS - N_EXPERTS), F32)], axis=1)
    wr_hi = wr.astype(BF16)
    wr2 = jnp.stack([wr_hi, (wr - wr_hi.astype(F32)).astype(BF16)])
    br = jnp.concatenate([p['b_group'][l], p['b_router'][l].reshape(-1),
                          jnp.zeros((LANES - N_GROUPS - N_EXPERTS,), F32)]).reshape(1, LANES)
    xn3, ids, wts, cnt = _router(h2, row(p['norm_ffn'][l]), wr2, br, tm)
    pos, row_tok3, blk_e, n_used, next_e, wslot = _dispatch_plan(ids, cnt[0, :N_EXPERTS], MOE_ROWS)
    y_rows = _experts(blk_e, n_used, next_e, wslot, row_tok3, xn3, p['w_gate'][l], p['w_up'][l], p['w_down'][l])
    tc = _tile(t, COMBINE_TOK)
    return pos.reshape(t // tc, 1, 2 * tc), y_rows, wts, h2, tc


def kernel(x, mem, norm_mix, w_in, gla_w_lr_f, gla_b_lr_f, gla_w_lr_b, gla_b_lr_b, gla_norm, conv_w, conv_b, m_wq, m_wk, m_wv, m_gate_bias, m_norm, w_branch_a, w_branch_b, w_mix_out, norm_xattn, norm_mem, w_xq, w_xkv, w_xo, norm_ffn, w_group, b_group, w_router, b_router, w_gate, w_up, w_down, norm_final):
    p = dict(norm_mix=norm_mix, w_in=w_in, gla_w_lr_f=gla_w_lr_f, gla_b_lr_f=gla_b_lr_f, gla_w_lr_b=gla_w_lr_b,
             gla_b_lr_b=gla_b_lr_b, gla_norm=gla_norm, conv_w=conv_w, conv_b=conv_b, m_wq=m_wq, m_wk=m_wk,
             m_wv=m_wv, m_gate_bias=m_gate_bias, m_norm=m_norm, w_branch_a=w_branch_a, w_branch_b=w_branch_b,
             w_mix_out=w_mix_out, norm_xattn=norm_xattn, norm_mem=norm_mem, w_xq=w_xq, w_xkv=w_xkv, w_xo=w_xo,
             norm_ffn=norm_ffn, w_group=w_group, b_group=b_group, w_router=w_router, b_router=b_router,
             w_gate=w_gate, w_up=w_up, w_down=w_down)
    bsz, seq, d = x.shape
    depth = norm_mix.shape[0]
    assert depth == 1, "the final norm is fused into the last layer's combine step"
    pos3, y_rows, wts, h2, tc = _layer(x, mem, p, 0)
    out = _combine(pos3, y_rows, wts, h2, norm_final.reshape(1, d), tc)
    return out.reshape(bsz, seq, d)
```

```python
import functools

import jax
import jax.numpy as jnp
from jax import lax
from jax.experimental import pallas as pl
from jax.experimental.pallas import tpu as pltpu

F32 = jnp.float32
BF16 = jnp.bfloat16

EPS = 1e-6
LOG2_E = 1.4426950408889634
D_MODEL = 2048

GLA_HEADS = 4
GLA_DK = 128
GLA_DV = 256
GLA_K = GLA_HEADS * GLA_DK
GLA_V = GLA_HEADS * GLA_DV
GLA_RANK = 16
GLA_TAU = 16.0
GLA_CHUNK = 128
GLA_SUB = 16
GLA_SEQS = 4
GLA_SAFE_LOG2 = 60.0

MLSTM_HEADS = 4
MLSTM_DH = 256
MLSTM_W = MLSTM_HEADS * MLSTM_DH
CONV_WIDTH = 5
MLSTM_CHUNK = 256
MLSTM_SEQS = 1
CONV_HALO = 16

XATTN_HEADS = 4
XATTN_DH = D_MODEL // XATTN_HEADS

N_GROUPS = 4
EXPERTS_PER_GROUP = 8
N_EXPERTS = N_GROUPS * EXPERTS_PER_GROUP
D_EXPERT = 512
MOE_ROWS = 256
COMBINE_TOK = 256
ROUTE_COLS = 8

LANES = 128
SLAB_W = LANES
SLAB = D_MODEL // SLAB_W
SLAB_PITCH = SLAB + 8
SLAB_DTYPE = F32
GATHER_SLOTS = 3
GATHER_BATCHES = 8

OFF_GA, OFF_GB = 0, 2048
OFF_Q, OFF_K, OFF_V, OFF_GG = 4096, 4608, 5120, 6144
OFF_MX, OFF_MZ = 7168, 8192
N_BIG = 9216
SM_LRF, SM_LRB, SM_GATES = 0, 16, 32

VMEM_LIMIT = 56 * 1024 * 1024

ROW_TILE = 1024
IN_PROJ_COLS = 2304
KV_COLS = 1024
MERGE_MIX_ROWS = 512
XATTN_ROWS = 1024
CONV_ROWS = 4096


def _cparams(sem):
    return pltpu.CompilerParams(dimension_semantics=sem, vmem_limit_bytes=VMEM_LIMIT)


def _rms(x, g):
    return x * lax.rsqrt(jnp.mean(x * x, axis=-1, keepdims=True) + EPS) * g


def _log_sigmoid(x):
    return jnp.minimum(x, 0.0) - jnp.log(1.0 + jnp.exp(-jnp.abs(x)))


def _dot(a, b):
    return jnp.dot(a, b, preferred_element_type=F32)


def _dot_nt(a, b):
    return lax.dot_general(a, b, (((1,), (1,)), ((), ())), preferred_element_type=F32)


def _dot_tn(a, b):
    return lax.dot_general(a, b, (((0,), (0,)), ((), ())), preferred_element_type=F32)


def _order_mask(c, rev):
    t = lax.broadcasted_iota(jnp.int32, (c, c), 0)
    s = lax.broadcasted_iota(jnp.int32, (c, c), 1)
    return (s >= t) if rev else (s <= t)


def _split3(x):
    hi = x.astype(BF16)
    r1 = x - hi.astype(F32)
    mid = r1.astype(BF16)
    return hi, mid, (r1 - mid.astype(F32)).astype(BF16)


def _cumsum_mm(mask_bf16, x):
    hi, mid, lo = _split3(x)
    return _dot(mask_bf16, hi) + _dot(mask_bf16, mid) + _dot(mask_bf16, lo)


def _in_proj_kernel(x_ref, g_ref, wbig_ref, wsm_ref, big_ref, sm_ref, xn_ref):
    @pl.when(pl.program_id(1) == 0)
    def _():
        xn = _rms(x_ref[...], g_ref[...]).astype(BF16)
        xn_ref[...] = xn
        sm_ref[...] = _dot(xn, wsm_ref[...])
    big_ref[...] = _dot(xn_ref[...], wbig_ref[...]).astype(BF16)


def _in_proj(x2, g, w_big, w_small, tm, tn):
    t, d = x2.shape
    n = w_big.shape[1]
    return pl.pallas_call(
        _in_proj_kernel,
        grid=(t // tm, n // tn),
        in_specs=[
            pl.BlockSpec((tm, d), lambda i, j: (i, 0)),
            pl.BlockSpec((1, d), lambda i, j: (0, 0)),
            pl.BlockSpec((d, tn), lambda i, j: (0, j)),
            pl.BlockSpec((d, LANES), lambda i, j: (0, 0)),
        ],
        out_specs=[
            pl.BlockSpec((tm, tn), lambda i, j: (i, j)),
            pl.BlockSpec((tm, LANES), lambda i, j: (i, 0)),
        ],
        out_shape=[jax.ShapeDtypeStruct((t, n), BF16), jax.ShapeDtypeStruct((t, LANES), F32)],
        scratch_shapes=[pltpu.VMEM((tm, d), BF16)],
        compiler_params=_cparams(("parallel", "arbitrary")),
        name="in_proj",
    )(x2, g, w_big, w_small)


def _gla_scores_any_decay(bh, qh, kh, kh_b, rev, c):
    sb = GLA_SUB
    n_sub = c // sb
    col = lax.broadcasted_iota(jnp.int32, (sb, c), 1)
    trow = lax.broadcasted_iota(jnp.int32, (sb, c), 0)
    slabs = [(qh[i * sb:(i + 1) * sb]
              * jnp.exp2(bh[i * sb:(i + 1) * sb] - bh[i * sb + s:i * sb + s + 1, :])).astype(BF16)
             for i in range(n_sub) for s in range(sb)]
    g = _dot_nt(jnp.concatenate(slabs, axis=0), kh_b)
    rows = []
    for i in range(n_sub):
        r0 = i * sb
        ref_row = r0 + (sb - 1 if rev else 0)
        beta = bh[ref_row:ref_row + 1, :]
        qt = (qh[r0:r0 + sb] * jnp.exp2(bh[r0:r0 + sb] - beta)).astype(BF16)
        kt = (kh * jnp.exp2(beta - bh)).astype(BF16)
        a_off = _dot_nt(qt, kt)
        a_diag = jnp.zeros((sb, c), F32)
        for s in range(sb):
            g0 = (i * sb + s) * sb
            a_diag = jnp.where(col == r0 + s, g[g0:g0 + sb], a_diag)
        tr = trow + r0
        if rev:
            off_mask = col >= r0 + sb
            diag_mask = (col >= tr) & (col < r0 + sb)
        else:
            off_mask = col < r0
            diag_mask = (col <= tr) & (col >= r0)
        rows.append(jnp.where(off_mask, a_off, 0.0) + jnp.where(diag_mask, a_diag, 0.0))
    return jnp.concatenate(rows, axis=0).astype(BF16)


def _gla_kernel(q_ref, k_ref, v_ref, sm_ref, wlr_ref, blr_ref, *rest, rev, final, c, nb):
    if final:
        oprev_ref, gg_ref, gn_ref, o_ref, st_ref, a_ref = rest
    else:
        o_ref, st_ref, a_ref = rest

    @pl.when(pl.program_id(1) == 0)
    def _():
        st_ref[...] = jnp.zeros_like(st_ref)

    mask = _order_mask(c, rev)
    mask_b = jnp.where(mask, 1.0, 0.0).astype(BF16)
    last = 0 if rev else c - 1

    for bi in range(nb):
        x = _dot(sm_ref[bi].astype(BF16), wlr_ref[...]) + blr_ref[...]
        la = _log_sigmoid(x) * (1.0 / GLA_TAU)
        b = _cumsum_mm(mask_b, la) * LOG2_E
        tot = b[last:last + 1, :]
        q = q_ref[bi].astype(F32) * (GLA_DK ** -0.5)
        k = k_ref[bi].astype(F32)
        q_in = (q * jnp.exp2(b)).astype(BF16)
        k_dec = (k * jnp.exp2(tot - b)).astype(BF16)
        e_tot = jnp.exp2(tot)

        mild = jnp.min(tot) > -GLA_SAFE_LOG2

        @pl.when(mild)
        def _():
            q_up = (q * jnp.exp2(b - tot)).astype(BF16)
            for h in range(GLA_HEADS):
                ks = slice(h * GLA_DK, (h + 1) * GLA_DK)
                a_ref[bi, h] = jnp.where(mask, _dot_nt(q_up[:, ks], k_dec[:, ks]), 0.0).astype(BF16)

        @pl.when(jnp.logical_not(mild))
        def _():
            for h in range(GLA_HEADS):
                ks = slice(h * GLA_DK, (h + 1) * GLA_DK)
                a_ref[bi, h] = _gla_scores_any_decay(b[:, ks], q[:, ks], k[:, ks], k_ref[bi, :, ks], rev, c)

        for h in range(GLA_HEADS):
            ks = slice(h * GLA_DK, (h + 1) * GLA_DK)
            vs = slice(h * GLA_DV, (h + 1) * GLA_DV)
            vh = v_ref[bi, :, vs]
            st = st_ref[bi, h]
            o_inter = _dot_nt(q_in[:, ks], st.astype(BF16))
            st_ref[bi, h] = st * e_tot[:, ks] + _dot_tn(vh, k_dec[:, ks])
            o = o_inter + _dot(a_ref[bi, h], vh)
            if final:
                o = o + oprev_ref[bi, :, vs]
                y = _rms(o, gn_ref[:, vs])
                gg = gg_ref[bi, :, vs].astype(F32)
                o_ref[bi, :, vs] = (y * (gg * jax.nn.sigmoid(gg))).astype(o_ref.dtype)
            else:
                o_ref[bi, :, vs] = o.astype(o_ref.dtype)


def _gla_scan(big3, sm3, wlr, blr, rev, prev=None, gn=None):
    bsz, seq, _ = big3.shape
    c = GLA_CHUNK
    n = seq // c
    nb = GLA_SEQS if bsz % GLA_SEQS == 0 else 1
    final = prev is not None

    def cm(ci):
        return (n - 1 - ci) if rev else ci

    in_specs = [
        pl.BlockSpec((nb, c, GLA_K), lambda b, ci: (b, cm(ci), OFF_Q // GLA_K)),
        pl.BlockSpec((nb, c, GLA_K), lambda b, ci: (b, cm(ci), OFF_K // GLA_K)),
        pl.BlockSpec((nb, c, GLA_V), lambda b, ci: (b, cm(ci), OFF_V // GLA_V)),
        pl.BlockSpec((nb, c, LANES), lambda b, ci: (b, cm(ci), 0)),
        pl.BlockSpec((LANES, GLA_K), lambda b, ci: (0, 0)),
        pl.BlockSpec((1, GLA_K), lambda b, ci: (0, 0)),
    ]
    args = [big3, big3, big3, sm3, wlr, blr]
    if final:
        in_specs += [
            pl.BlockSpec((nb, c, GLA_V), lambda b, ci: (b, cm(ci), 0)),
            pl.BlockSpec((nb, c, GLA_V), lambda b, ci: (b, cm(ci), OFF_GG // GLA_V)),
            pl.BlockSpec((1, GLA_V), lambda b, ci: (0, 0)),
        ]
        args += [prev, big3, gn]
    return pl.pallas_call(
        functools.partial(_gla_kernel, rev=rev, final=final, c=c, nb=nb),
        grid=(bsz // nb, n),
        in_specs=in_specs,
        out_specs=pl.BlockSpec((nb, c, GLA_V), lambda b, ci: (b, cm(ci), 0)),
        out_shape=jax.ShapeDtypeStruct((bsz, seq, GLA_V), BF16 if final else F32),
        scratch_shapes=[pltpu.VMEM((nb, GLA_HEADS, GLA_DV, GLA_DK), F32),
                        pltpu.VMEM((nb, GLA_HEADS, c, c), BF16)],
        compiler_params=_cparams(("parallel", "arbitrary")),
        name="gla_bwd" if rev else "gla_fwd",
    )(*args)


def _mlstm_pre_kernel(cur_ref, prev_ref, next_ref, cw_ref, cb_ref, wq_ref, wk_ref, wv_ref,
                      q_ref, k_ref, v_ref, *, tm):
    i = pl.program_id(2)
    cur_b = cur_ref[0]
    cur = cur_b.astype(F32)
    halo = CONV_WIDTH // 2
    prev = jnp.where(i > 0, prev_ref[0].astype(F32), 0.0)
    nxt = jnp.where(i < pl.num_programs(2) - 1, next_ref[0].astype(F32), 0.0)
    ext = jnp.concatenate([prev[CONV_HALO - 8:], cur, nxt[:8]], axis=0)
    acc = jnp.zeros_like(cur) + cb_ref[...]
    for w in range(CONV_WIDTH):
        off = 8 - halo + w
        acc = acc + ext[off:off + tm] * cw_ref[w:w + 1, :]
    xc = (acc * jax.nn.sigmoid(acc)).astype(BF16)
    q_ref[0] = _dot(xc, wq_ref[0]).astype(BF16)
    k_ref[0] = (_dot(xc, wk_ref[0]) * (MLSTM_DH ** -0.5)).astype(BF16)
    v_ref[0] = _dot(cur_b, wv_ref[0]).astype(BF16)


def _mlstm_pre(big3, cw, cb, wq, wk, wv, tm):
    bsz, seq, _ = big3.shape
    dh = MLSTM_DH
    nh = tm // CONV_HALO
    n_halo = seq // CONV_HALO
    c0 = OFF_MX // dh
    out = jax.ShapeDtypeStruct((bsz, seq, MLSTM_W), BF16)
    ospec = pl.BlockSpec((1, tm, dh), lambda b, h, i: (b, i, h))
    wspec = pl.BlockSpec((1, dh, dh), lambda b, h, i: (h, 0, 0))
    return pl.pallas_call(
        functools.partial(_mlstm_pre_kernel, tm=tm),
        grid=(bsz, MLSTM_HEADS, seq // tm),
        in_specs=[
            pl.BlockSpec((1, tm, dh), lambda b, h, i: (b, i, c0 + h)),
            pl.BlockSpec((1, CONV_HALO, dh), lambda b, h, i: (b, jnp.maximum(i * nh - 1, 0), c0 + h)),
            pl.BlockSpec((1, CONV_HALO, dh), lambda b, h, i: (b, jnp.minimum((i + 1) * nh, n_halo - 1), c0 + h)),
            pl.BlockSpec((8, dh), lambda b, h, i: (0, h)),
            pl.BlockSpec((1, dh), lambda b, h, i: (0, h)),
            wspec, wspec, wspec,
        ],
        out_specs=[ospec, ospec, ospec],
        out_shape=[out, out, out],
        compiler_params=_cparams(("parallel", "parallel", "parallel")),
        name="mlstm_pre",
    )(big3, big3, big3, cw, cb, wq, wk, wv)


def _mlstm_kernel(q_ref, k_ref, v_ref, sm_ref, gb_ref, *rest, rev, final, c, nb):
    if final:
        hprev_ref, mz_ref, mn_ref, o_ref, c_ref, n_ref, m_ref = rest
    else:
        o_ref, c_ref, n_ref, m_ref = rest

    @pl.when(pl.program_id(1) == 0)
    def _():
        c_ref[...] = jnp.zeros_like(c_ref)
        n_ref[...] = jnp.zeros_like(n_ref)
        m_ref[...] = jnp.zeros_like(m_ref)

    mask = _order_mask(c, rev)
    mask_b = jnp.where(mask, 1.0, 0.0).astype(BF16)
    last = 0 if rev else c - 1
    dh = MLSTM_DH

    for bi in range(nb):
        g = sm_ref[bi] + gb_ref[...]
        bcum = _cumsum_mm(mask_b, _log_sigmoid(g))
        g_t = g.T
        b_t = bcum.T

        for h in range(MLSTM_HEADS):
            ci = SM_GATES + (2 * MLSTM_HEADS if rev else 0) + h
            cf = ci + MLSTM_HEADS
            hs = slice(h * dh, (h + 1) * dh)
            i_col, b_col = g[:, ci:ci + 1], bcum[:, cf:cf + 1]
            i_row, b_row = g_t[ci:ci + 1, :], b_t[cf:cf + 1, :]
            tot = b_col[last:last + 1, :]
            m_prev = m_ref[bi, h, 0:1, 0:1]
            qh, kh, vh = q_ref[bi, :, hs], k_ref[bi, :, hs], v_ref[bi, :, hs]
            c_st = c_ref[bi, h]
            n_st = n_ref[bi, h, 0:1, :]

            dmat = jnp.where(mask, b_col - b_row + i_row, -jnp.inf)
            inter_log = b_col + m_prev
            m_t = jnp.maximum(inter_log, jnp.max(dmat, axis=-1, keepdims=True))
            w_intra = jnp.exp(dmat - m_t)
            w_inter = jnp.exp(inter_log - m_t)
            s = _dot_nt(qh, kh) * w_intra
            num = w_inter * _dot(qh, c_st.astype(BF16)) + _dot(s.astype(BF16), vh)
            den = (w_inter * jnp.sum(qh.astype(F32) * n_st, axis=-1, keepdims=True)
                   + jnp.sum(s, axis=-1, keepdims=True))
            hh = num / jnp.maximum(jnp.abs(den), jnp.exp(-m_t))

            upd_col = tot - b_col + i_col
            upd_row = tot - b_row + i_row
            m_new = jnp.maximum(tot + m_prev, jnp.max(upd_row, axis=-1, keepdims=True))
            w_old = jnp.exp(tot + m_prev - m_new)
            kw = kh.astype(F32) * jnp.exp(upd_col - m_new)
            c_ref[bi, h] = w_old * c_st + _dot_tn(kw.astype(BF16), vh)
            n_ref[bi, h] = jnp.broadcast_to(w_old * n_st + jnp.sum(kw, axis=0, keepdims=True), (8, dh))
            m_ref[bi, h] = jnp.broadcast_to(m_new, (8, LANES))

            if final:
                hh = hh + hprev_ref[bi, :, hs]
                y = _rms(hh, mn_ref[:, hs])
                o_ref[bi, :, hs] = (y * jax.nn.sigmoid(mz_ref[bi, :, hs].astype(F32))).astype(o_ref.dtype)
            else:
                o_ref[bi, :, hs] = hh.astype(o_ref.dtype)


def _mlstm_scan(qm, km, vm, sm3, gbias, rev, prev=None, big3=None, mn=None):
    bsz, seq, w = qm.shape
    c = min(MLSTM_CHUNK, seq)
    n = seq // c
    nb = MLSTM_SEQS if bsz % MLSTM_SEQS == 0 else 1
    final = prev is not None

    def cm(ci):
        return (n - 1 - ci) if rev else ci

    xspec = pl.BlockSpec((nb, c, w), lambda b, ci: (b, cm(ci), 0))
    in_specs = [xspec, xspec, xspec,
                pl.BlockSpec((nb, c, LANES), lambda b, ci: (b, cm(ci), 0)),
                pl.BlockSpec((1, LANES), lambda b, ci: (0, 0))]
    args = [qm, km, vm, sm3, gbias]
    if final:
        in_specs += [xspec,
                     pl.BlockSpec((nb, c, w), lambda b, ci: (b, cm(ci), OFF_MZ // MLSTM_W)),
                     pl.BlockSpec((1, w), lambda b, ci: (0, 0))]
        args += [prev, big3, mn]
    return pl.pallas_call(
        functools.partial(_mlstm_kernel, rev=rev, final=final, c=c, nb=nb),
        grid=(bsz // nb, n),
        in_specs=in_specs,
        out_specs=xspec,
        out_shape=jax.ShapeDtypeStruct((bsz, seq, w), BF16 if final else F32),
        scratch_shapes=[pltpu.VMEM((nb, MLSTM_HEADS, MLSTM_DH, MLSTM_DH), F32),
                        pltpu.VMEM((nb, MLSTM_HEADS, 8, MLSTM_DH), F32),
                        pltpu.VMEM((nb, MLSTM_HEADS, 8, LANES), F32)],
        compiler_params=_cparams(("parallel", "arbitrary")),
        name="mlstm_bwd" if rev else "mlstm_fwd",
    )(*args)


def _merge_mix_kernel(ya_ref, yb_ref, wa_ref, wb_ref, ga_ref, gb_ref, wo_ref, x_ref, o_ref):
    a = _dot(ya_ref[...], wa_ref[...])
    b = _dot(yb_ref[...], wb_ref[...])
    ga = jax.nn.sigmoid(ga_ref[...].astype(F32))
    gb = jax.nn.sigmoid(gb_ref[...].astype(F32))
    merged = (ga * a + gb * b).astype(BF16)
    o_ref[...] = x_ref[...] + _dot(merged, wo_ref[...])


def _merge_mix(ya, yb, wa, wb, big, wo, x2, tm):
    t, kdim = ya.shape
    n = wa.shape[1]

    def resident(shape):
        return pl.BlockSpec(shape, lambda i: (0, 0), pipeline_mode=pl.Buffered(1))

    return pl.pallas_call(
        _merge_mix_kernel,
        grid=(t // tm,),
        in_specs=[
            pl.BlockSpec((tm, kdim), lambda i: (i, 0)),
            pl.BlockSpec((tm, kdim), lambda i: (i, 0)),
            resident((kdim, n)),
            resident((kdim, n)),
            pl.BlockSpec((tm, n), lambda i: (i, OFF_GA // n)),
            pl.BlockSpec((tm, n), lambda i: (i, OFF_GB // n)),
            resident((n, n)),
            pl.BlockSpec((tm, n), lambda i: (i, 0)),
        ],
        out_specs=pl.BlockSpec((tm, n), lambda i: (i, 0)),
        out_shape=jax.ShapeDtypeStruct((t, n), F32),
        compiler_params=_cparams(("parallel",)),
        name="merge_mix",
    )(ya, yb, wa, wb, big, big, wo, x2)


def _kv_kernel(mem_ref, g_ref, w_ref, o_ref):
    mn = _rms(mem_ref[...], g_ref[...]).astype(BF16)
    o_ref[...] = _dot(mn, w_ref[...]).astype(BF16)


def _kv_proj(mem2, g, w, tn):
    r, d = mem2.shape
    n = w.shape[1]
    return pl.pallas_call(
        _kv_kernel,
        grid=(n // tn,),
        in_specs=[
            pl.BlockSpec((r, d), lambda j: (0, 0)),
            pl.BlockSpec((1, d), lambda j: (0, 0)),
            pl.BlockSpec((d, tn), lambda j: (0, j)),
        ],
        out_specs=pl.BlockSpec((r, tn), lambda j: (0, j)),
        out_shape=jax.ShapeDtypeStruct((r, n), BF16),
        compiler_params=_cparams(("parallel",)),
        name="kv_proj",
    )(mem2, g, w)


def _xattn_kernel(h_ref, g_ref, wq_ref, k_ref, v_ref, wo_ref, o_ref, xn_ref):
    @pl.when(pl.program_id(1) == 0)
    def _():
        x = h_ref[...]
        xn_ref[...] = _rms(x, g_ref[...]).astype(BF16)
        o_ref[...] = x
    q = _dot(xn_ref[...], wq_ref[...]).astype(BF16)
    s = _dot_nt(q, k_ref[0]) * (XATTN_DH ** -0.5)
    p = jnp.exp(s - jnp.max(s, axis=-1, keepdims=True))
    p = p / jnp.sum(p, axis=-1, keepdims=True)
    o = _dot(p.astype(BF16), v_ref[0]).astype(BF16)
    o_ref[...] += _dot(o, wo_ref[...])


def _xattn(h1, g, wq, kv3, wo, seq, tm):
    t, d = h1.shape
    n_mem = kv3.shape[1]
    dh = XATTN_DH
    per_b = seq // tm
    return pl.pallas_call(
        _xattn_kernel,
        grid=(t // tm, XATTN_HEADS),
        in_specs=[
            pl.BlockSpec((tm, d), lambda i, h: (i, 0)),
            pl.BlockSpec((1, d), lambda i, h: (0, 0)),
            pl.BlockSpec((d, dh), lambda i, h: (0, h)),
            pl.BlockSpec((1, n_mem, dh), lambda i, h: (i // per_b, 0, h)),
            pl.BlockSpec((1, n_mem, dh), lambda i, h: (i // per_b, 0, XATTN_HEADS + h)),
            pl.BlockSpec((dh, d), lambda i, h: (h, 0)),
        ],
        out_specs=pl.BlockSpec((tm, d), lambda i, h: (i, 0)),
        out_shape=jax.ShapeDtypeStruct((t, d), F32),
        scratch_shapes=[pltpu.VMEM((tm, d), BF16)],
        compiler_params=_cparams(("parallel", "arbitrary")),
        name="xattn",
    )(h1, g, wq, kv3, kv3, wo)


def _slab_store(ref, val):
    n = val.shape[0]
    for c in range(SLAB):
        ref[pl.ds(c, n, stride=SLAB), :] = val[:, c * SLAB_W:(c + 1) * SLAB_W].astype(ref.dtype)


def _slab_load(buf, slot, first, n):
    return jnp.concatenate(
        [buf[slot, pl.ds(first * SLAB_PITCH + c, n, stride=SLAB_PITCH), :] for c in range(SLAB)], axis=1)


def _slab_copy(src_hbm, dst_buf, sem, slot, src_slab, dst_slab):
    return pltpu.make_async_copy(src_hbm.at[pl.ds(src_slab * SLAB, SLAB)],
                                 dst_buf.at[slot, pl.ds(dst_slab * SLAB_PITCH, SLAB)], sem.at[slot])


def _zero_after(v):
    bits = lax.bitcast_convert_type(v[-1:, -1:], jnp.uint32)
    return ((bits >> 16) >> 16).astype(jnp.int32)[0, 0]


def _slab_wait(src_hbm, dst_buf, sem, slot, n):
    pltpu.make_async_copy(src_hbm.at[pl.ds(0, n * SLAB)], dst_buf.at[slot, pl.ds(0, n * SLAB)], sem.at[slot]).wait()


def _router_kernel(h_ref, g_ref, wr_ref, br_ref, xn_ref, ids_ref, wts_ref, cnt_ref, carry_ref):
    @pl.when(pl.program_id(0) == 0)
    def _():
        carry_ref[...] = jnp.zeros_like(carry_ref)

    xn = _rms(h_ref[...], g_ref[...])
    _slab_store(xn_ref, xn)
    hi = xn.astype(BF16)
    lo = (xn - hi.astype(F32)).astype(BF16)
    lg = _dot(hi, wr_ref[0]) + _dot(hi, wr_ref[1]) + _dot(lo, wr_ref[0]) + br_ref[...]
    lane = lax.broadcasted_iota(jnp.int32, lg.shape, 1).astype(F32)
    ninf = -jnp.inf
    big_lane = float(LANES)

    def first_max(v):
        mx = jnp.max(v, axis=-1, keepdims=True)
        return mx, jnp.min(jnp.where(v == mx, lane, big_lane), axis=-1, keepdims=True)

    gl = jnp.where(lane < N_GROUPS, lg, ninf)
    gmax, g_sel = first_max(gl)
    p_g = 1.0 / jnp.sum(jnp.exp(gl - gmax), axis=-1, keepdims=True)
    lo_l = N_GROUPS + EXPERTS_PER_GROUP * g_sel
    el = jnp.where((lane >= lo_l) & (lane < lo_l + EXPERTS_PER_GROUP), lg, ninf)
    emax, i1 = first_max(el)
    max2, i2 = first_max(jnp.where(lane == i1, ninf, el))
    e2 = jnp.exp(max2 - emax)
    w1 = p_g / (1.0 + e2)
    w2 = p_g * e2 / (1.0 + e2)
    e1, e2 = i1 - N_GROUPS, i2 - N_GROUPS
    oh1, oh2 = lane == e1, lane == e2
    oh = jnp.where(oh1, 1.0, 0.0) + jnp.where(oh2, 1.0, 0.0)
    tm = lg.shape[0]
    earlier = lax.broadcasted_iota(jnp.int32, (tm, tm), 1) < lax.broadcasted_iota(jnp.int32, (tm, tm), 0)
    carry = carry_ref[0:1, :]
    before = _dot(jnp.where(earlier, 1.0, 0.0).astype(BF16), oh.astype(BF16)) + carry
    r1 = jnp.sum(jnp.where(oh1, before, 0.0), axis=-1, keepdims=True)
    r2 = jnp.sum(jnp.where(oh2, before, 0.0), axis=-1, keepdims=True)
    counts = carry + jnp.sum(oh, axis=0, keepdims=True)
    carry_ref[...] = jnp.broadcast_to(counts, carry_ref.shape)
    cnt_ref[...] = jnp.broadcast_to(counts, cnt_ref.shape).astype(jnp.int32)
    ids = jnp.where(lane == 0.0, e1, jnp.where(lane == 1.0, e2, jnp.where(lane == 2.0, r1,
                                                                         jnp.where(lane == 3.0, r2, 0.0))))
    ids_ref[...] = ids.astype(jnp.int32)[:, 0:ROUTE_COLS]
    wts_ref[...] = jnp.where(lane == 0.0, w1, jnp.where(lane == 1.0, w2, 0.0))


def _router(h2, g, wr, br, tm):
    t, d = h2.shape
    return pl.pallas_call(
        _router_kernel,
        grid=(t // tm,),
        in_specs=[
            pl.BlockSpec((tm, d), lambda i: (i, 0)),
            pl.BlockSpec((1, d), lambda i: (0, 0)),
            pl.BlockSpec((2, d, LANES), lambda i: (0, 0, 0)),
            pl.BlockSpec((1, LANES), lambda i: (0, 0)),
        ],
        out_specs=[
            pl.BlockSpec((tm * SLAB, SLAB_W), lambda i: (i, 0)),
            pl.BlockSpec((tm, ROUTE_COLS), lambda i: (i, 0)),
            pl.BlockSpec((tm, LANES), lambda i: (i, 0)),
            pl.BlockSpec((8, LANES), lambda i: (0, 0)),
        ],
        out_shape=[jax.ShapeDtypeStruct((t * SLAB, SLAB_W), SLAB_DTYPE),
                   jax.ShapeDtypeStruct((t, ROUTE_COLS), jnp.int32),
                   jax.ShapeDtypeStruct((t, LANES), F32),
                   jax.ShapeDtypeStruct((8, LANES), jnp.int32)],
        scratch_shapes=[pltpu.VMEM((8, LANES), F32)],
        compiler_params=_cparams(("arbitrary",)),
        name="router",
    )(h2, g, wr, br)


def _experts_kernel(blk_e_ref, n_used_ref, next_e_ref, wslot_ref, tok0_ref, tok1_ref, tok2_ref, x_hbm,
                    wg_hbm, wu_hbm, wd_hbm, y_ref, xbuf, wg_f, wu_f, wd_f, wg_b, wu_b, wd_b, sem, wsem, *, rows):
    j = pl.program_id(0)
    n_used = n_used_ref[0]
    used = j < n_used
    slot = j % GATHER_SLOTS
    slot1 = (j + 1) % GATHER_SLOTS
    slot2 = (j + 2) % GATHER_SLOTS

    @pl.when(j == 0)
    def _():
        def start(r, carry):
            _slab_copy(x_hbm, xbuf, sem, 0, tok0_ref[0, 0, r], r).start()
            _slab_copy(x_hbm, xbuf, sem, 1, tok1_ref[0, 0, r], r).start()
            return carry
        lax.fori_loop(0, rows, start, 0)

    @pl.when(jnp.logical_not(used))
    def _():
        @pl.when(j == n_used)
        def _():
            _slab_wait(x_hbm, xbuf, sem, slot, rows)
            _slab_wait(x_hbm, xbuf, sem, slot1, rows)
        y_ref[...] = jnp.zeros_like(y_ref)

    def weight_copies(e, ws):
        return [pltpu.make_async_copy(src.at[e], dst.at[ws], wsem.at[ws])
                for src, dst in ((wg_hbm, wg_f), (wu_hbm, wu_f), (wd_hbm, wd_f))]

    @pl.when(j == 0)
    def _():
        for cp in weight_copies(blk_e_ref[0], 0):
            cp.start()

    @pl.when(used)
    def _():
        _slab_wait(x_hbm, xbuf, sem, slot, rows)

        @pl.when((j == 0) | (blk_e_ref[j] != blk_e_ref[jnp.maximum(j - 1, 0)]))
        def _():
            ws = wslot_ref[j]
            for cp in weight_copies(0, ws):
                cp.wait()

            @pl.when(next_e_ref[j] >= 0)
            def _():
                for cp in weight_copies(next_e_ref[j], 1 - ws):
                    cp.start()

            wg_b[...] = wg_f[ws].astype(BF16)
            wu_b[...] = wu_f[ws].astype(BF16)
            wd_b[...] = wd_f[ws].astype(BF16)

        batch = rows // GATHER_BATCHES
        issued = [0]

        def gather_batch(after):
            zero = 0 if after is None else _zero_after(after)
            for r in range(issued[0] * batch, (issued[0] + 1) * batch):
                _slab_copy(x_hbm, xbuf, sem, slot2, tok2_ref[0, 0, r] + zero, r).start()
            issued[0] += 1

        xb = _slab_load(xbuf, slot, 0, rows).astype(BF16)
        gather_batch(None)
        de = wg_b.shape[1]
        halves = [slice(0, de // 2), slice(de // 2, de)]
        gate, up = [], []
        for cs in halves:
            gate.append(_dot(xb, wg_b[:, cs]))
            gather_batch(gate[-1])
        for cs in halves:
            up.append(_dot(xb, wu_b[:, cs]))
            gather_batch(up[-1])
        gate = jnp.concatenate(gate, axis=1)
        act = (gate * jax.nn.sigmoid(gate) * jnp.concatenate(up, axis=1)).astype(BF16)
        wo = de
        per = wo // SLAB_W
        for c in range(wd_b.shape[1] // wo):
            yc = _dot(act, wd_b[:, c * wo:(c + 1) * wo])
            for cc in range(per):
                y_ref[pl.ds(c * per + cc, rows, stride=SLAB), :] = (
                    yc[:, cc * SLAB_W:(cc + 1) * SLAB_W].astype(y_ref.dtype))
            if issued[0] < GATHER_BATCHES:
                gather_batch(yc)
        assert issued[0] == GATHER_BATCHES

        @pl.when(j == pl.num_programs(0) - 1)
        def _():
            _slab_wait(x_hbm, xbuf, sem, slot1, rows)
            _slab_wait(x_hbm, xbuf, sem, slot2, rows)


def _experts(blk_e, n_used, next_e, wslot, row_tok3, xn_slab, wg, wu, wd):
    n_blocks, _, rows = row_tok3.shape
    _, d, de = wg.shape

    def tok_spec(ahead):
        return pl.BlockSpec((1, 1, rows), lambda j, *_: (jnp.minimum(j + ahead, n_blocks - 1), 0, 0),
                            memory_space=pltpu.SMEM)

    hbm = pl.BlockSpec(memory_space=pl.ANY)
    grid_spec = pltpu.PrefetchScalarGridSpec(
        num_scalar_prefetch=4,
        grid=(n_blocks,),
        in_specs=[tok_spec(0), tok_spec(1), tok_spec(2), hbm, hbm, hbm, hbm],
        out_specs=pl.BlockSpec((rows * SLAB, SLAB_W), lambda j, *_: (j, 0)),
        scratch_shapes=[pltpu.VMEM((GATHER_SLOTS, rows * SLAB_PITCH, SLAB_W), SLAB_DTYPE),
                        pltpu.VMEM((2, d, de), F32), pltpu.VMEM((2, d, de), F32), pltpu.VMEM((2, de, d), F32),
                        pltpu.VMEM((d, de), BF16), pltpu.VMEM((d, de), BF16), pltpu.VMEM((de, d), BF16),
                        pltpu.SemaphoreType.DMA((GATHER_SLOTS,)), pltpu.SemaphoreType.DMA((2,))],
    )
    return pl.pallas_call(
        functools.partial(_experts_kernel, rows=rows),
        grid_spec=grid_spec,
        out_shape=jax.ShapeDtypeStruct((n_blocks * rows * SLAB, SLAB_W), SLAB_DTYPE),
        compiler_params=_cparams(("arbitrary",)),
        name="experts",
    )(blk_e, n_used, next_e, wslot, row_tok3, row_tok3, row_tok3, xn_slab, wg, wu, wd)


def _combine_kernel(pos0_ref, pos1_ref, pos2_ref, y_hbm, wts_ref, h_ref, g_ref, o_ref, ybuf, sem, *, tc):
    j = pl.program_id(0)
    slot = j % GATHER_SLOTS
    slot1 = (j + 1) % GATHER_SLOTS
    slot2 = (j + 2) % GATHER_SLOTS

    @pl.when(j == 0)
    def _():
        def start(r, carry):
            for s, idx in ((0, pos0_ref), (1, pos1_ref)):
                _slab_copy(y_hbm, ybuf, sem, s, idx[0, 0, 2 * r], r).start()
                _slab_copy(y_hbm, ybuf, sem, s, idx[0, 0, 2 * r + 1], tc + r).start()
            return carry
        lax.fori_loop(0, tc, start, 0)

    _slab_wait(y_hbm, ybuf, sem, slot, 2 * tc)
    for r in range(tc):
        _slab_copy(y_hbm, ybuf, sem, slot2, pos2_ref[0, 0, 2 * r], r).start(priority=0)
        _slab_copy(y_hbm, ybuf, sem, slot2, pos2_ref[0, 0, 2 * r + 1], tc + r).start(priority=1)
    w = wts_ref[...]
    y = (_slab_load(ybuf, slot, 0, tc).astype(F32) * w[:, 0:1]
         + _slab_load(ybuf, slot, tc, tc).astype(F32) * w[:, 1:2])
    o_ref[...] = _rms(h_ref[...] + y, g_ref[...])

    @pl.when(j == pl.num_programs(0) - 1)
    def _():
        _slab_wait(y_hbm, ybuf, sem, slot1, 2 * tc)
        _slab_wait(y_hbm, ybuf, sem, slot2, 2 * tc)


def _combine(pos3, y_slab, wts, h2, g, tc):
    t, d = h2.shape
    n = t // tc

    def pos_spec(ahead):
        return pl.BlockSpec((1, 1, 2 * tc), lambda i: (jnp.minimum(i + ahead, n - 1), 0, 0),
                            memory_space=pltpu.SMEM)

    return pl.pallas_call(
        functools.partial(_combine_kernel, tc=tc),
        grid=(n,),
        in_specs=[
            pos_spec(0), pos_spec(1), pos_spec(2),
            pl.BlockSpec(memory_space=pl.ANY),
            pl.BlockSpec((tc, LANES), lambda i: (i, 0)),
            pl.BlockSpec((tc, d), lambda i: (i, 0)),
            pl.BlockSpec((1, d), lambda i: (0, 0)),
        ],
        out_specs=pl.BlockSpec((tc, d), lambda i: (i, 0)),
        out_shape=jax.ShapeDtypeStruct((t, d), F32),
        scratch_shapes=[pltpu.VMEM((GATHER_SLOTS, 2 * tc * SLAB_PITCH, SLAB_W), SLAB_DTYPE),
                        pltpu.SemaphoreType.DMA((GATHER_SLOTS,))],
        compiler_params=_cparams(("arbitrary",)),
        name="combine",
    )(pos3, pos3, pos3, y_slab, wts, h2, g)


def _dispatch_plan(ids, counts, rows):
    flat_e = ids[:, 0:2].reshape(-1)
    rank = ids[:, 2:4].reshape(-1)
    n_asg = flat_e.shape[0]
    experts = jnp.arange(N_EXPERTS, dtype=jnp.int32)
    padded = ((counts + rows - 1) // rows) * rows
    pends = jnp.cumsum(padded)
    pstarts = pends - padded
    pos = jnp.sum(jnp.where(flat_e[:, None] == experts[None, :], pstarts[None, :], 0), axis=1) + rank
    n_rows = n_asg + N_EXPERTS * rows
    n_blocks = n_rows // rows
    row_tok = jnp.zeros((n_rows,), jnp.int32).at[pos].set(jnp.arange(n_asg, dtype=jnp.int32) // 2)
    starts = jnp.arange(n_blocks, dtype=jnp.int32) * rows
    blk_e = jnp.minimum(jnp.sum((pends[None, :] <= starts[:, None]).astype(jnp.int32), axis=1), N_EXPERTS - 1)
    n_used = (pends[-1] // rows).astype(jnp.int32).reshape(1)
    has = counts > 0
    succ = jnp.min(jnp.where((experts[None, :] > experts[:, None]) & has[None, :], experts[None, :], N_EXPERTS),
                   axis=1)
    succ = jnp.where(succ < N_EXPERTS, succ, -1)
    before = jnp.sum((has[None, :] & (experts[None, :] < blk_e[:, None])).astype(jnp.int32), axis=1)
    return (pos.astype(jnp.int32), row_tok.reshape(n_blocks, 1, rows), blk_e.astype(jnp.int32), n_used,
            succ[blk_e].astype(jnp.int32), (before % 2).astype(jnp.int32))


def _tile(n, pref):
    return pref if n % pref == 0 else n


def _layer(h, mem, p, l):
    bsz, seq, d = h.shape
    t = bsz * seq
    x2 = h.reshape(t, d)
    row = lambda v: v.reshape(1, -1)
    tm = _tile(t, ROW_TILE)

    w_in = p['w_in'][l]
    w_big = jnp.concatenate([w_in[:, 5168:9264], w_in[:, 0:3072], w_in[:, 3104:5152]], axis=1).astype(BF16)
    w_small = jnp.concatenate([w_in[:, 3072:3104], w_in[:, 5152:5168],
                               jnp.zeros((d, LANES - 48), F32)], axis=1).astype(BF16)
    big, small = _in_proj(x2, row(p['norm_mix'][l]), w_big, w_small, tm, IN_PROJ_COLS)
    big3 = big.reshape(bsz, seq, N_BIG)
    sm3 = small.reshape(bsz, seq, LANES)

    def lr_pad(w, off):
        return jnp.zeros((LANES, GLA_K), F32).at[off:off + GLA_RANK].set(w).astype(BF16)

    o_f = _gla_scan(big3, sm3, lr_pad(p['gla_w_lr_f'][l], SM_LRF), row(p['gla_b_lr_f'][l]), rev=False)
    y_a = _gla_scan(big3, sm3, lr_pad(p['gla_w_lr_b'][l], SM_LRB), row(p['gla_b_lr_b'][l]), rev=True,
                    prev=o_f, gn=row(p['gla_norm'][l]))

    cw = jnp.zeros((8, MLSTM_W), F32).at[:CONV_WIDTH].set(p['conv_w'][l].reshape(CONV_WIDTH, MLSTM_W))
    qm, km, vm = _mlstm_pre(big3, cw, row(p['conv_b'][l]), p['m_wq'][l].astype(BF16),
                            p['m_wk'][l].astype(BF16), p['m_wv'][l].astype(BF16), _tile(seq, CONV_ROWS))
    gbias = jnp.zeros((1, LANES), F32).at[0, SM_GATES:SM_GATES + 4 * MLSTM_HEADS].set(
        p['m_gate_bias'][l].reshape(-1))
    h_f = _mlstm_scan(qm, km, vm, sm3, gbias, rev=False)
    y_b = _mlstm_scan(qm, km, vm, sm3, gbias, rev=True, prev=h_f, big3=big3, mn=row(p['m_norm'][l]))

    assert OFF_GA % d == 0 and OFF_GB % d == 0
    h1 = _merge_mix(y_a.reshape(t, GLA_V), y_b.reshape(t, MLSTM_W), p['w_branch_a'][l].astype(BF16),
                    p['w_branch_b'][l].astype(BF16), big, p['w_mix_out'][l].astype(BF16), x2,
                    _tile(t, MERGE_MIX_ROWS))

    n_mem = mem.shape[1]
    kv = _kv_proj(mem.reshape(bsz * n_mem, d), row(p['norm_mem'][l]), p['w_xkv'][l].astype(BF16), KV_COLS)
    h2 = _xattn(h1, row(p['norm_xattn'][l]), p['w_xq'][l].astype(BF16), kv.reshape(bsz, n_mem, 2 * d),
                p['w_xo'][l].astype(BF16), seq, _tile(seq, XATTN_ROWS))

    wr = jnp.concatenate([p['w_group'][l], p['w_router'][l].transpose(1, 0, 2).reshape(d, N_EXPERTS),
                          jnp.zeros((d, LANES - N_GROUPS - N_EXPERTS), F32)], axis=1)
    wr_hi = wr.astype(BF16)
    wr2 = jnp.stack([wr_hi, (wr - wr_hi.astype(F32)).astype(BF16)])
    br = jnp.concatenate([p['b_group'][l], p['b_router'][l].reshape(-1),
                          jnp.zeros((LANES - N_GROUPS - N_EXPERTS,), F32)]).reshape(1, LANES)
    xn3, ids, wts, cnt = _router(h2, row(p['norm_ffn'][l]), wr2, br, tm)
    pos, row_tok3, blk_e, n_used, next_e, wslot = _dispatch_plan(ids, cnt[0, :N_EXPERTS], MOE_ROWS)
    y_rows = _experts(blk_e, n_used, next_e, wslot, row_tok3, xn3, p['w_gate'][l], p['w_up'][l], p['w_down'][l])
    tc = _tile(t, COMBINE_TOK)
    return pos.reshape(t // tc, 1, 2 * tc), y_rows, wts, h2, tc


def kernel(x, mem, norm_mix, w_in, gla_w_lr_f, gla_b_lr_f, gla_w_lr_b, gla_b_lr_b, gla_norm, conv_w, conv_b, m_wq, m_wk, m_wv, m_gate_bias, m_norm, w_branch_a, w_branch_b, w_mix_out, norm_xattn, norm_mem, w_xq, w_xkv, w_xo, norm_ffn, w_group, b_group, w_router, b_router, w_gate, w_up, w_down, norm_final):
    p = dict(norm_mix=norm_mix, w_in=w_in, gla_w_lr_f=gla_w_lr_f, gla_b_lr_f=gla_b_lr_f, gla_w_lr_b=gla_w_lr_b,
             gla_b_lr_b=gla_b_lr_b, gla_norm=gla_norm, conv_w=conv_w, conv_b=conv_b, m_wq=m_wq, m_wk=m_wk,
             m_wv=m_wv, m_gate_bias=m_gate_bias, m_norm=m_norm, w_branch_a=w_branch_a, w_branch_b=w_branch_b,
             w_mix_out=w_mix_out, norm_xattn=norm_xattn, norm_mem=norm_mem, w_xq=w_xq, w_xkv=w_xkv, w_xo=w_xo,
             norm_ffn=norm_ffn, w_group=w_group, b_group=b_group, w_router=w_router, b_router=b_router,
             w_gate=w_gate, w_up=w_up, w_down=w_down)
    bsz, seq, d = x.shape
    depth = norm_mix.shape[0]
    assert depth == 1, "the final norm is fused into the last layer's combine step"
    pos3, y_rows, wts, h2, tc = _layer(x, mem, p, 0)
    out = _combine(pos3, y_rows, wts, h2, norm_final.reshape(1, d), tc)
    return out.reshape(bsz, seq, d)
```

```python
import functools

import jax
import jax.numpy as jnp
from jax import lax
from jax.experimental import pallas as pl
from jax.experimental.pallas import tpu as pltpu

F32 = jnp.float32
BF16 = jnp.bfloat16

EPS = 1e-6
LOG2_E = 1.4426950408889634
D_MODEL = 2048

GLA_HEADS = 4
GLA_DK = 128
GLA_DV = 256
GLA_K = GLA_HEADS * GLA_DK
GLA_V = GLA_HEADS * GLA_DV
GLA_RANK = 16
GLA_TAU = 16.0
GLA_CHUNK = 256
GLA_SUB = 16
GLA_SEQS = 4
GLA_SAFE_LOG2 = 60.0

MLSTM_HEADS = 4
MLSTM_DH = 256
MLSTM_W = MLSTM_HEADS * MLSTM_DH
CONV_WIDTH = 5
MLSTM_CHUNK = 256
MLSTM_SEQS = 1
CONV_HALO = 16

XATTN_HEADS = 4
XATTN_DH = D_MODEL // XATTN_HEADS

N_GROUPS = 4
EXPERTS_PER_GROUP = 8
N_EXPERTS = N_GROUPS * EXPERTS_PER_GROUP
D_EXPERT = 512
MOE_ROWS = 256
COMBINE_TOK = 256
ROUTE_COLS = 8

LANES = 128
SLAB_W = LANES
SLAB = D_MODEL // SLAB_W
SLAB_PITCH = SLAB + 8
SLAB_DTYPE = F32
GATHER_SLOTS = 3
GATHER_BATCHES = 8

OFF_GA, OFF_GB = 0, 2048
OFF_Q, OFF_K, OFF_V, OFF_GG = 4096, 4608, 5120, 6144
OFF_MX, OFF_MZ = 7168, 8192
N_BIG = 9216
SM_LRF, SM_LRB, SM_GATES = 0, 16, 32

VMEM_LIMIT = 56 * 1024 * 1024

ROW_TILE = 1024
IN_PROJ_COLS = 2304
KV_COLS = 1024
MERGE_MIX_ROWS = 512
XATTN_ROWS = 1024
CONV_ROWS = 4096


def _cparams(sem):
    return pltpu.CompilerParams(dimension_semantics=sem, vmem_limit_bytes=VMEM_LIMIT)


def _rms(x, g):
    return x * lax.rsqrt(jnp.mean(x * x, axis=-1, keepdims=True) + EPS) * g


def _log_sigmoid(x):
    return jnp.minimum(x, 0.0) - jnp.log(1.0 + jnp.exp(-jnp.abs(x)))


def _dot(a, b):
    return jnp.dot(a, b, preferred_element_type=F32)


def _dot_nt(a, b):
    return lax.dot_general(a, b, (((1,), (1,)), ((), ())), preferred_element_type=F32)


def _dot_tn(a, b):
    return lax.dot_general(a, b, (((0,), (0,)), ((), ())), preferred_element_type=F32)


def _order_mask(c, rev):
    t = lax.broadcasted_iota(jnp.int32, (c, c), 0)
    s = lax.broadcasted_iota(jnp.int32, (c, c), 1)
    return (s >= t) if rev else (s <= t)


def _split3(x):
    hi = x.astype(BF16)
    r1 = x - hi.astype(F32)
    mid = r1.astype(BF16)
    return hi, mid, (r1 - mid.astype(F32)).astype(BF16)


def _cumsum_mm(mask_bf16, x):
    hi, mid, lo = _split3(x)
    return _dot(mask_bf16, hi) + _dot(mask_bf16, mid) + _dot(mask_bf16, lo)


def _in_proj_kernel(x_ref, g_ref, wbig_ref, wsm_ref, big_ref, sm_ref, xn_ref):
    @pl.when(pl.program_id(1) == 0)
    def _():
        xn = _rms(x_ref[...], g_ref[...]).astype(BF16)
        xn_ref[...] = xn
        sm_ref[...] = _dot(xn, wsm_ref[...])
    big_ref[...] = _dot(xn_ref[...], wbig_ref[...]).astype(BF16)


def _in_proj(x2, g, w_big, w_small, tm, tn):
    t, d = x2.shape
    n = w_big.shape[1]
    return pl.pallas_call(
        _in_proj_kernel,
        grid=(t // tm, n // tn),
        in_specs=[
            pl.BlockSpec((tm, d), lambda i, j: (i, 0)),
            pl.BlockSpec((1, d), lambda i, j: (0, 0)),
            pl.BlockSpec((d, tn), lambda i, j: (0, j)),
            pl.BlockSpec((d, LANES), lambda i, j: (0, 0)),
        ],
        out_specs=[
            pl.BlockSpec((tm, tn), lambda i, j: (i, j)),
            pl.BlockSpec((tm, LANES), lambda i, j: (i, 0)),
        ],
        out_shape=[jax.ShapeDtypeStruct((t, n), BF16), jax.ShapeDtypeStruct((t, LANES), F32)],
        scratch_shapes=[pltpu.VMEM((tm, d), BF16)],
        compiler_params=_cparams(("parallel", "arbitrary")),
        name="in_proj",
    )(x2, g, w_big, w_small)


def _gla_scores_any_decay(bh, qh, kh, kh_b, rev, c):
    sb = GLA_SUB
    n_sub = c // sb
    col = lax.broadcasted_iota(jnp.int32, (sb, c), 1)
    trow = lax.broadcasted_iota(jnp.int32, (sb, c), 0)
    slabs = [(qh[i * sb:(i + 1) * sb]
              * jnp.exp2(bh[i * sb:(i + 1) * sb] - bh[i * sb + s:i * sb + s + 1, :])).astype(BF16)
             for i in range(n_sub) for s in range(sb)]
    g = _dot_nt(jnp.concatenate(slabs, axis=0), kh_b)
    rows = []
    for i in range(n_sub):
        r0 = i * sb
        ref_row = r0 + (sb - 1 if rev else 0)
        beta = bh[ref_row:ref_row + 1, :]
        qt = (qh[r0:r0 + sb] * jnp.exp2(bh[r0:r0 + sb] - beta)).astype(BF16)
        kt = (kh * jnp.exp2(beta - bh)).astype(BF16)
        a_off = _dot_nt(qt, kt)
        a_diag = jnp.zeros((sb, c), F32)
        for s in range(sb):
            g0 = (i * sb + s) * sb
            a_diag = jnp.where(col == r0 + s, g[g0:g0 + sb], a_diag)
        tr = trow + r0
        if rev:
            off_mask = col >= r0 + sb
            diag_mask = (col >= tr) & (col < r0 + sb)
        else:
            off_mask = col < r0
            diag_mask = (col <= tr) & (col >= r0)
        rows.append(jnp.where(off_mask, a_off, 0.0) + jnp.where(diag_mask, a_diag, 0.0))
    return jnp.concatenate(rows, axis=0).astype(BF16)


def _gla_kernel(q_ref, k_ref, v_ref, sm_ref, wlr_ref, blr_ref, *rest, rev, final, c, nb):
    if final:
        oprev_ref, gg_ref, gn_ref, o_ref, st_ref, a_ref = rest
    else:
        o_ref, st_ref, a_ref = rest

    @pl.when(pl.program_id(1) == 0)
    def _():
        st_ref[...] = jnp.zeros_like(st_ref)

    mask = _order_mask(c, rev)
    mask_b = jnp.where(mask, 1.0, 0.0).astype(BF16)
    last = 0 if rev else c - 1

    for bi in range(nb):
        x = _dot(sm_ref[bi].astype(BF16), wlr_ref[...]) + blr_ref[...]
        la = _log_sigmoid(x) * (1.0 / GLA_TAU)
        b = _cumsum_mm(mask_b, la) * LOG2_E
        tot = b[last:last + 1, :]
        q = q_ref[bi].astype(F32) * (GLA_DK ** -0.5)
        k = k_ref[bi].astype(F32)
        q_in = (q * jnp.exp2(b)).astype(BF16)
        k_dec = (k * jnp.exp2(tot - b)).astype(BF16)
        e_tot = jnp.exp2(tot)

        mild = jnp.min(tot) > -GLA_SAFE_LOG2

        @pl.when(mild)
        def _():
            q_up = (q * jnp.exp2(b - tot)).astype(BF16)
            for h in range(GLA_HEADS):
                ks = slice(h * GLA_DK, (h + 1) * GLA_DK)
                a_ref[bi, h] = jnp.where(mask, _dot_nt(q_up[:, ks], k_dec[:, ks]), 0.0).astype(BF16)

        @pl.when(jnp.logical_not(mild))
        def _():
            for h in range(GLA_HEADS):
                ks = slice(h * GLA_DK, (h + 1) * GLA_DK)
                a_ref[bi, h] = _gla_scores_any_decay(b[:, ks], q[:, ks], k[:, ks], k_ref[bi, :, ks], rev, c)

        for h in range(GLA_HEADS):
            ks = slice(h * GLA_DK, (h + 1) * GLA_DK)
            vs = slice(h * GLA_DV, (h + 1) * GLA_DV)
            vh = v_ref[bi, :, vs]
            st = st_ref[bi, h]
            o_inter = _dot_nt(q_in[:, ks], st.astype(BF16))
            st_ref[bi, h] = st * e_tot[:, ks] + _dot_tn(vh, k_dec[:, ks])
            o = o_inter + _dot(a_ref[bi, h], vh)
            if final:
                o = o + oprev_ref[bi, :, vs]
                y = _rms(o, gn_ref[:, vs])
                gg = gg_ref[bi, :, vs].astype(F32)
                o_ref[bi, :, vs] = (y * (gg * jax.nn.sigmoid(gg))).astype(o_ref.dtype)
            else:
                o_ref[bi, :, vs] = o.astype(o_ref.dtype)


def _gla_scan(big3, sm3, wlr, blr, rev, prev=None, gn=None):
    bsz, seq, _ = big3.shape
    c = GLA_CHUNK
    n = seq // c
    nb = GLA_SEQS if bsz % GLA_SEQS == 0 else 1
    final = prev is not None

    def cm(ci):
        return (n - 1 - ci) if rev else ci

    in_specs = [
        pl.BlockSpec((nb, c, GLA_K), lambda b, ci: (b, cm(ci), OFF_Q // GLA_K)),
        pl.BlockSpec((nb, c, GLA_K), lambda b, ci: (b, cm(ci), OFF_K // GLA_K)),
        pl.BlockSpec((nb, c, GLA_V), lambda b, ci: (b, cm(ci), OFF_V // GLA_V)),
        pl.BlockSpec((nb, c, LANES), lambda b, ci: (b, cm(ci), 0)),
        pl.BlockSpec((LANES, GLA_K), lambda b, ci: (0, 0)),
        pl.BlockSpec((1, GLA_K), lambda b, ci: (0, 0)),
    ]
    args = [big3, big3, big3, sm3, wlr, blr]
    if final:
        in_specs += [
            pl.BlockSpec((nb, c, GLA_V), lambda b, ci: (b, cm(ci), 0)),
            pl.BlockSpec((nb, c, GLA_V), lambda b, ci: (b, cm(ci), OFF_GG // GLA_V)),
            pl.BlockSpec((1, GLA_V), lambda b, ci: (0, 0)),
        ]
        args += [prev, big3, gn]
    return pl.pallas_call(
        functools.partial(_gla_kernel, rev=rev, final=final, c=c, nb=nb),
        grid=(bsz // nb, n),
        in_specs=in_specs,
        out_specs=pl.BlockSpec((nb, c, GLA_V), lambda b, ci: (b, cm(ci), 0)),
        out_shape=jax.ShapeDtypeStruct((bsz, seq, GLA_V), BF16 if final else F32),
        scratch_shapes=[pltpu.VMEM((nb, GLA_HEADS, GLA_DV, GLA_DK), F32),
                        pltpu.VMEM((nb, GLA_HEADS, c, c), BF16)],
        compiler_params=_cparams(("parallel", "arbitrary")),
        name="gla_bwd" if rev else "gla_fwd",
    )(*args)


def _mlstm_pre_kernel(cur_ref, prev_ref, next_ref, cw_ref, cb_ref, wq_ref, wk_ref, wv_ref,
                      q_ref, k_ref, v_ref, *, tm):
    i = pl.program_id(2)
    cur_b = cur_ref[0]
    cur = cur_b.astype(F32)
    halo = CONV_WIDTH // 2
    prev = jnp.where(i > 0, prev_ref[0].astype(F32), 0.0)
    nxt = jnp.where(i < pl.num_programs(2) - 1, next_ref[0].astype(F32), 0.0)
    ext = jnp.concatenate([prev[CONV_HALO - 8:], cur, nxt[:8]], axis=0)
    acc = jnp.zeros_like(cur) + cb_ref[...]
    for w in range(CONV_WIDTH):
        off = 8 - halo + w
        acc = acc + ext[off:off + tm] * cw_ref[w:w + 1, :]
    xc = (acc * jax.nn.sigmoid(acc)).astype(BF16)
    q_ref[0] = _dot(xc, wq_ref[0]).astype(BF16)
    k_ref[0] = (_dot(xc, wk_ref[0]) * (MLSTM_DH ** -0.5)).astype(BF16)
    v_ref[0] = _dot(cur_b, wv_ref[0]).astype(BF16)


def _mlstm_pre(big3, cw, cb, wq, wk, wv, tm):
    bsz, seq, _ = big3.shape
    dh = MLSTM_DH
    nh = tm // CONV_HALO
    n_halo = seq // CONV_HALO
    c0 = OFF_MX // dh
    out = jax.ShapeDtypeStruct((bsz, seq, MLSTM_W), BF16)
    ospec = pl.BlockSpec((1, tm, dh), lambda b, h, i: (b, i, h))
    wspec = pl.BlockSpec((1, dh, dh), lambda b, h, i: (h, 0, 0))
    return pl.pallas_call(
        functools.partial(_mlstm_pre_kernel, tm=tm),
        grid=(bsz, MLSTM_HEADS, seq // tm),
        in_specs=[
            pl.BlockSpec((1, tm, dh), lambda b, h, i: (b, i, c0 + h)),
            pl.BlockSpec((1, CONV_HALO, dh), lambda b, h, i: (b, jnp.maximum(i * nh - 1, 0), c0 + h)),
            pl.BlockSpec((1, CONV_HALO, dh), lambda b, h, i: (b, jnp.minimum((i + 1) * nh, n_halo - 1), c0 + h)),
            pl.BlockSpec((8, dh), lambda b, h, i: (0, h)),
            pl.BlockSpec((1, dh), lambda b, h, i: (0, h)),
            wspec, wspec, wspec,
        ],
        out_specs=[ospec, ospec, ospec],
        out_shape=[out, out, out],
        compiler_params=_cparams(("parallel", "parallel", "parallel")),
        name="mlstm_pre",
    )(big3, big3, big3, cw, cb, wq, wk, wv)


def _mlstm_kernel(q_ref, k_ref, v_ref, sm_ref, gb_ref, *rest, rev, final, c, nb):
    if final:
        hprev_ref, mz_ref, mn_ref, o_ref, c_ref, n_ref, m_ref = rest
    else:
        o_ref, c_ref, n_ref, m_ref = rest

    @pl.when(pl.program_id(1) == 0)
    def _():
        c_ref[...] = jnp.zeros_like(c_ref)
        n_ref[...] = jnp.zeros_like(n_ref)
        m_ref[...] = jnp.zeros_like(m_ref)

    mask = _order_mask(c, rev)
    mask_b = jnp.where(mask, 1.0, 0.0).astype(BF16)
    last = 0 if rev else c - 1
    dh = MLSTM_DH

    for bi in range(nb):
        g = sm_ref[bi] + gb_ref[...]
        bcum = _cumsum_mm(mask_b, _log_sigmoid(g))
        g_t = g.T
        b_t = bcum.T

        for h in range(MLSTM_HEADS):
            ci = SM_GATES + (2 * MLSTM_HEADS if rev else 0) + h
            cf = ci + MLSTM_HEADS
            hs = slice(h * dh, (h + 1) * dh)
            i_col, b_col = g[:, ci:ci + 1], bcum[:, cf:cf + 1]
            i_row, b_row = g_t[ci:ci + 1, :], b_t[cf:cf + 1, :]
            tot = b_col[last:last + 1, :]
            m_prev = m_ref[bi, h, 0:1, 0:1]
            qh, kh, vh = q_ref[bi, :, hs], k_ref[bi, :, hs], v_ref[bi, :, hs]
            c_st = c_ref[bi, h]
            n_st = n_ref[bi, h, 0:1, :]

            dmat = jnp.where(mask, b_col - b_row + i_row, -jnp.inf)
            inter_log = b_col + m_prev
            m_t = jnp.maximum(inter_log, jnp.max(dmat, axis=-1, keepdims=True))
            w_intra = jnp.exp(dmat - m_t)
            w_inter = jnp.exp(inter_log - m_t)
            s = _dot_nt(qh, kh) * w_intra
            num = w_inter * _dot(qh, c_st.astype(BF16)) + _dot(s.astype(BF16), vh)
            den = (w_inter * jnp.sum(qh.astype(F32) * n_st, axis=-1, keepdims=True)
                   + jnp.sum(s, axis=-1, keepdims=True))
            hh = num / jnp.maximum(jnp.abs(den), jnp.exp(-m_t))

            upd_col = tot - b_col + i_col
            upd_row = tot - b_row + i_row
            m_new = jnp.maximum(tot + m_prev, jnp.max(upd_row, axis=-1, keepdims=True))
            w_old = jnp.exp(tot + m_prev - m_new)
            kw = kh.astype(F32) * jnp.exp(upd_col - m_new)
            c_ref[bi, h] = w_old * c_st + _dot_tn(kw.astype(BF16), vh)
            n_ref[bi, h] = jnp.broadcast_to(w_old * n_st + jnp.sum(kw, axis=0, keepdims=True), (8, dh))
            m_ref[bi, h] = jnp.broadcast_to(m_new, (8, LANES))

            if final:
                hh = hh + hprev_ref[bi, :, hs]
                y = _rms(hh, mn_ref[:, hs])
                o_ref[bi, :, hs] = (y * jax.nn.sigmoid(mz_ref[bi, :, hs].astype(F32))).astype(o_ref.dtype)
            else:
                o_ref[bi, :, hs] = hh.astype(o_ref.dtype)


def _mlstm_scan(qm, km, vm, sm3, gbias, rev, prev=None, big3=None, mn=None):
    bsz, seq, w = qm.shape
    c = min(MLSTM_CHUNK, seq)
    n = seq // c
    nb = MLSTM_SEQS if bsz % MLSTM_SEQS == 0 else 1
    final = prev is not None

    def cm(ci):
        return (n - 1 - ci) if rev else ci

    xspec = pl.BlockSpec((nb, c, w), lambda b, ci: (b, cm(ci), 0))
    in_specs = [xspec, xspec, xspec,
                pl.BlockSpec((nb, c, LANES), lambda b, ci: (b, cm(ci), 0)),
                pl.BlockSpec((1, LANES), lambda b, ci: (0, 0))]
    args = [qm, km, vm, sm3, gbias]
    if final:
        in_specs += [xspec,
                     pl.BlockSpec((nb, c, w), lambda b, ci: (b, cm(ci), OFF_MZ // MLSTM_W)),
                     pl.BlockSpec((1, w), lambda b, ci: (0, 0))]
        args += [prev, big3, mn]
    return pl.pallas_call(
        functools.partial(_mlstm_kernel, rev=rev, final=final, c=c, nb=nb),
        grid=(bsz // nb, n),
        in_specs=in_specs,
        out_specs=xspec,
        out_shape=jax.ShapeDtypeStruct((bsz, seq, w), BF16 if final else F32),
        scratch_shapes=[pltpu.VMEM((nb, MLSTM_HEADS, MLSTM_DH, MLSTM_DH), F32),
                        pltpu.VMEM((nb, MLSTM_HEADS, 8, MLSTM_DH), F32),
                        pltpu.VMEM((nb, MLSTM_HEADS, 8, LANES), F32)],
        compiler_params=_cparams(("parallel", "arbitrary")),
        name="mlstm_bwd" if rev else "mlstm_fwd",
    )(*args)


def _merge_mix_kernel(ya_ref, yb_ref, wa_ref, wb_ref, ga_ref, gb_ref, wo_ref, x_ref, o_ref):
    a = _dot(ya_ref[...], wa_ref[...])
    b = _dot(yb_ref[...], wb_ref[...])
    ga = jax.nn.sigmoid(ga_ref[...].astype(F32))
    gb = jax.nn.sigmoid(gb_ref[...].astype(F32))
    merged = (ga * a + gb * b).astype(BF16)
    o_ref[...] = x_ref[...] + _dot(merged, wo_ref[...])


def _merge_mix(ya, yb, wa, wb, big, wo, x2, tm):
    t, kdim = ya.shape
    n = wa.shape[1]

    def resident(shape):
        return pl.BlockSpec(shape, lambda i: (0, 0), pipeline_mode=pl.Buffered(1))

    return pl.pallas_call(
        _merge_mix_kernel,
        grid=(t // tm,),
        in_specs=[
            pl.BlockSpec((tm, kdim), lambda i: (i, 0)),
            pl.BlockSpec((tm, kdim), lambda i: (i, 0)),
            resident((kdim, n)),
            resident((kdim, n)),
            pl.BlockSpec((tm, n), lambda i: (i, OFF_GA // n)),
            pl.BlockSpec((tm, n), lambda i: (i, OFF_GB // n)),
            resident((n, n)),
            pl.BlockSpec((tm, n), lambda i: (i, 0)),
        ],
        out_specs=pl.BlockSpec((tm, n), lambda i: (i, 0)),
        out_shape=jax.ShapeDtypeStruct((t, n), F32),
        compiler_params=_cparams(("parallel",)),
        name="merge_mix",
    )(ya, yb, wa, wb, big, big, wo, x2)


def _kv_kernel(mem_ref, g_ref, w_ref, o_ref):
    mn = _rms(mem_ref[...], g_ref[...]).astype(BF16)
    o_ref[...] = _dot(mn, w_ref[...]).astype(BF16)


def _kv_proj(mem2, g, w, tn):
    r, d = mem2.shape
    n = w.shape[1]
    return pl.pallas_call(
        _kv_kernel,
        grid=(n // tn,),
        in_specs=[
            pl.BlockSpec((r, d), lambda j: (0, 0)),
            pl.BlockSpec((1, d), lambda j: (0, 0)),
            pl.BlockSpec((d, tn), lambda j: (0, j)),
        ],
        out_specs=pl.BlockSpec((r, tn), lambda j: (0, j)),
        out_shape=jax.ShapeDtypeStruct((r, n), BF16),
        compiler_params=_cparams(("parallel",)),
        name="kv_proj",
    )(mem2, g, w)


def _xattn_kernel(h_ref, g_ref, wq_ref, k_ref, v_ref, wo_ref, o_ref, xn_ref):
    @pl.when(pl.program_id(1) == 0)
    def _():
        x = h_ref[...]
        xn_ref[...] = _rms(x, g_ref[...]).astype(BF16)
        o_ref[...] = x
    q = _dot(xn_ref[...], wq_ref[...]).astype(BF16)
    s = _dot_nt(q, k_ref[0]) * (XATTN_DH ** -0.5)
    p = jnp.exp(s - jnp.max(s, axis=-1, keepdims=True))
    p = p / jnp.sum(p, axis=-1, keepdims=True)
    o = _dot(p.astype(BF16), v_ref[0]).astype(BF16)
    o_ref[...] += _dot(o, wo_ref[...])


def _xattn(h1, g, wq, kv3, wo, seq, tm):
    t, d = h1.shape
    n_mem = kv3.shape[1]
    dh = XATTN_DH
    per_b = seq // tm
    return pl.pallas_call(
        _xattn_kernel,
        grid=(t // tm, XATTN_HEADS),
        in_specs=[
            pl.BlockSpec((tm, d), lambda i, h: (i, 0)),
            pl.BlockSpec((1, d), lambda i, h: (0, 0)),
            pl.BlockSpec((d, dh), lambda i, h: (0, h)),
            pl.BlockSpec((1, n_mem, dh), lambda i, h: (i // per_b, 0, h)),
            pl.BlockSpec((1, n_mem, dh), lambda i, h: (i // per_b, 0, XATTN_HEADS + h)),
            pl.BlockSpec((dh, d), lambda i, h: (h, 0)),
        ],
        out_specs=pl.BlockSpec((tm, d), lambda i, h: (i, 0)),
        out_shape=jax.ShapeDtypeStruct((t, d), F32),
        scratch_shapes=[pltpu.VMEM((tm, d), BF16)],
        compiler_params=_cparams(("parallel", "arbitrary")),
        name="xattn",
    )(h1, g, wq, kv3, kv3, wo)


def _slab_store(ref, val):
    n = val.shape[0]
    for c in range(SLAB):
        ref[pl.ds(c, n, stride=SLAB), :] = val[:, c * SLAB_W:(c + 1) * SLAB_W].astype(ref.dtype)


def _slab_load(buf, slot, first, n):
    return jnp.concatenate(
        [buf[slot, pl.ds(first * SLAB_PITCH + c, n, stride=SLAB_PITCH), :] for c in range(SLAB)], axis=1)


def _slab_copy(src_hbm, dst_buf, sem, slot, src_slab, dst_slab):
    return pltpu.make_async_copy(src_hbm.at[pl.ds(src_slab * SLAB, SLAB)],
                                 dst_buf.at[slot, pl.ds(dst_slab * SLAB_PITCH, SLAB)], sem.at[slot])


def _zero_after(v):
    bits = lax.bitcast_convert_type(v[-1:, -1:], jnp.uint32)
    return ((bits >> 16) >> 16).astype(jnp.int32)[0, 0]


def _slab_wait(src_hbm, dst_buf, sem, slot, n):
    pltpu.make_async_copy(src_hbm.at[pl.ds(0, n * SLAB)], dst_buf.at[slot, pl.ds(0, n * SLAB)], sem.at[slot]).wait()


def _router_kernel(h_ref, g_ref, wr_ref, br_ref, xn_ref, ids_ref, wts_ref, cnt_ref, carry_ref):
    @pl.when(pl.program_id(0) == 0)
    def _():
        carry_ref[...] = jnp.zeros_like(carry_ref)

    xn = _rms(h_ref[...], g_ref[...])
    _slab_store(xn_ref, xn)
    hi = xn.astype(BF16)
    lo = (xn - hi.astype(F32)).astype(BF16)
    lg = _dot(hi, wr_ref[0]) + _dot(hi, wr_ref[1]) + _dot(lo, wr_ref[0]) + br_ref[...]
    lane = lax.broadcasted_iota(jnp.int32, lg.shape, 1).astype(F32)
    ninf = -jnp.inf
    big_lane = float(LANES)

    def first_max(v):
        mx = jnp.max(v, axis=-1, keepdims=True)
        return mx, jnp.min(jnp.where(v == mx, lane, big_lane), axis=-1, keepdims=True)

    gl = jnp.where(lane < N_GROUPS, lg, ninf)
    gmax, g_sel = first_max(gl)
    p_g = 1.0 / jnp.sum(jnp.exp(gl - gmax), axis=-1, keepdims=True)
    lo_l = N_GROUPS + EXPERTS_PER_GROUP * g_sel
    el = jnp.where((lane >= lo_l) & (lane < lo_l + EXPERTS_PER_GROUP), lg, ninf)
    emax, i1 = first_max(el)
    max2, i2 = first_max(jnp.where(lane == i1, ninf, el))
    e2 = jnp.exp(max2 - emax)
    w1 = p_g / (1.0 + e2)
    w2 = p_g * e2 / (1.0 + e2)
    e1, e2 = i1 - N_GROUPS, i2 - N_GROUPS
    oh1, oh2 = lane == e1, lane == e2
    oh = jnp.where(oh1, 1.0, 0.0) + jnp.where(oh2, 1.0, 0.0)
    tm = lg.shape[0]
    earlier = lax.broadcasted_iota(jnp.int32, (tm, tm), 1) < lax.broadcasted_iota(jnp.int32, (tm, tm), 0)
    carry = carry_ref[0:1, :]
    before = _dot(jnp.where(earlier, 1.0, 0.0).astype(BF16), oh.astype(BF16)) + carry
    r1 = jnp.sum(jnp.where(oh1, before, 0.0), axis=-1, keepdims=True)
    r2 = jnp.sum(jnp.where(oh2, before, 0.0), axis=-1, keepdims=True)
    counts = carry + jnp.sum(oh, axis=0, keepdims=True)
    carry_ref[...] = jnp.broadcast_to(counts, carry_ref.shape)
    cnt_ref[...] = jnp.broadcast_to(counts, cnt_ref.shape).astype(jnp.int32)
    ids = jnp.where(lane == 0.0, e1, jnp.where(lane == 1.0, e2, jnp.where(lane == 2.0, r1,
                                                                         jnp.where(lane == 3.0, r2, 0.0))))
    ids_ref[...] = ids.astype(jnp.int32)[:, 0:ROUTE_COLS]
    wts_ref[...] = jnp.where(lane == 0.0, w1, jnp.where(lane == 1.0, w2, 0.0))


def _router(h2, g, wr, br, tm):
    t, d = h2.shape
    return pl.pallas_call(
        _router_kernel,
        grid=(t // tm,),
        in_specs=[
            pl.BlockSpec((tm, d), lambda i: (i, 0)),
            pl.BlockSpec((1, d), lambda i: (0, 0)),
            pl.BlockSpec((2, d, LANES), lambda i: (0, 0, 0)),
            pl.BlockSpec((1, LANES), lambda i: (0, 0)),
        ],
        out_specs=[
            pl.BlockSpec((tm * SLAB, SLAB_W), lambda i: (i, 0)),
            pl.BlockSpec((tm, ROUTE_COLS), lambda i: (i, 0)),
            pl.BlockSpec((tm, LANES), lambda i: (i, 0)),
            pl.BlockSpec((8, LANES), lambda i: (0, 0)),
        ],
        out_shape=[jax.ShapeDtypeStruct((t * SLAB, SLAB_W), SLAB_DTYPE),
                   jax.ShapeDtypeStruct((t, ROUTE_COLS), jnp.int32),
                   jax.ShapeDtypeStruct((t, LANES), F32),
                   jax.ShapeDtypeStruct((8, LANES), jnp.int32)],
        scratch_shapes=[pltpu.VMEM((8, LANES), F32)],
        compiler_params=_cparams(("arbitrary",)),
        name="router",
    )(h2, g, wr, br)


def _experts_kernel(blk_e_ref, n_used_ref, next_e_ref, wslot_ref, tok0_ref, tok1_ref, tok2_ref, x_hbm,
                    wg_hbm, wu_hbm, wd_hbm, y_ref, xbuf, wg_f, wu_f, wd_f, wg_b, wu_b, wd_b, sem, wsem, *, rows):
    j = pl.program_id(0)
    n_used = n_used_ref[0]
    used = j < n_used
    slot = j % GATHER_SLOTS
    slot1 = (j + 1) % GATHER_SLOTS
    slot2 = (j + 2) % GATHER_SLOTS

    @pl.when(j == 0)
    def _():
        def start(r, carry):
            _slab_copy(x_hbm, xbuf, sem, 0, tok0_ref[0, 0, r], r).start()
            _slab_copy(x_hbm, xbuf, sem, 1, tok1_ref[0, 0, r], r).start()
            return carry
        lax.fori_loop(0, rows, start, 0)

    @pl.when(jnp.logical_not(used))
    def _():
        @pl.when(j == n_used)
        def _():
            _slab_wait(x_hbm, xbuf, sem, slot, rows)
            _slab_wait(x_hbm, xbuf, sem, slot1, rows)
        y_ref[...] = jnp.zeros_like(y_ref)

    def weight_copies(e, ws):
        return [pltpu.make_async_copy(src.at[e], dst.at[ws], wsem.at[ws])
                for src, dst in ((wg_hbm, wg_f), (wu_hbm, wu_f), (wd_hbm, wd_f))]

    @pl.when(j == 0)
    def _():
        for cp in weight_copies(blk_e_ref[0], 0):
            cp.start()

    @pl.when(used)
    def _():
        _slab_wait(x_hbm, xbuf, sem, slot, rows)

        @pl.when((j == 0) | (blk_e_ref[j] != blk_e_ref[jnp.maximum(j - 1, 0)]))
        def _():
            ws = wslot_ref[j]
            for cp in weight_copies(0, ws):
                cp.wait()

            @pl.when(next_e_ref[j] >= 0)
            def _():
                for cp in weight_copies(next_e_ref[j], 1 - ws):
                    cp.start()

            wg_b[...] = wg_f[ws].astype(BF16)
            wu_b[...] = wu_f[ws].astype(BF16)
            wd_b[...] = wd_f[ws].astype(BF16)

        batch = rows // GATHER_BATCHES
        issued = [0]

        def gather_batch(after):
            zero = 0 if after is None else _zero_after(after)
            for r in range(issued[0] * batch, (issued[0] + 1) * batch):
                _slab_copy(x_hbm, xbuf, sem, slot2, tok2_ref[0, 0, r] + zero, r).start()
            issued[0] += 1

        xb = _slab_load(xbuf, slot, 0, rows).astype(BF16)
        gather_batch(None)
        de = wg_b.shape[1]
        halves = [slice(0, de // 2), slice(de // 2, de)]
        gate, up = [], []
        for cs in halves:
            gate.append(_dot(xb, wg_b[:, cs]))
            gather_batch(gate[-1])
        for cs in halves:
            up.append(_dot(xb, wu_b[:, cs]))
            gather_batch(up[-1])
        gate = jnp.concatenate(gate, axis=1)
        act = (gate * jax.nn.sigmoid(gate) * jnp.concatenate(up, axis=1)).astype(BF16)
        wo = de
        per = wo // SLAB_W
        for c in range(wd_b.shape[1] // wo):
            yc = _dot(act, wd_b[:, c * wo:(c + 1) * wo])
            for cc in range(per):
                y_ref[pl.ds(c * per + cc, rows, stride=SLAB), :] = (
                    yc[:, cc * SLAB_W:(cc + 1) * SLAB_W].astype(y_ref.dtype))
            if issued[0] < GATHER_BATCHES:
                gather_batch(yc)
        assert issued[0] == GATHER_BATCHES

        @pl.when(j == pl.num_programs(0) - 1)
        def _():
            _slab_wait(x_hbm, xbuf, sem, slot1, rows)
            _slab_wait(x_hbm, xbuf, sem, slot2, rows)


def _experts(blk_e, n_used, next_e, wslot, row_tok3, xn_slab, wg, wu, wd):
    n_blocks, _, rows = row_tok3.shape
    _, d, de = wg.shape

    def tok_spec(ahead):
        return pl.BlockSpec((1, 1, rows), lambda j, *_: (jnp.minimum(j + ahead, n_blocks - 1), 0, 0),
                            memory_space=pltpu.SMEM)

    hbm = pl.BlockSpec(memory_space=pl.ANY)
    grid_spec = pltpu.PrefetchScalarGridSpec(
        num_scalar_prefetch=4,
        grid=(n_blocks,),
        in_specs=[tok_spec(0), tok_spec(1), tok_spec(2), hbm, hbm, hbm, hbm],
        out_specs=pl.BlockSpec((rows * SLAB, SLAB_W), lambda j, *_: (j, 0)),
        scratch_shapes=[pltpu.VMEM((GATHER_SLOTS, rows * SLAB_PITCH, SLAB_W), SLAB_DTYPE),
                        pltpu.VMEM((2, d, de), F32), pltpu.VMEM((2, d, de), F32), pltpu.VMEM((2, de, d), F32),
                        pltpu.VMEM((d, de), BF16), pltpu.VMEM((d, de), BF16), pltpu.VMEM((de, d), BF16),
                        pltpu.SemaphoreType.DMA((GATHER_SLOTS,)), pltpu.SemaphoreType.DMA((2,))],
    )
    return pl.pallas_call(
        functools.partial(_experts_kernel, rows=rows),
        grid_spec=grid_spec,
        out_shape=jax.ShapeDtypeStruct((n_blocks * rows * SLAB, SLAB_W), SLAB_DTYPE),
        compiler_params=_cparams(("arbitrary",)),
        name="experts",
    )(blk_e, n_used, next_e, wslot, row_tok3, row_tok3, row_tok3, xn_slab, wg, wu, wd)


def _combine_kernel(pos0_ref, pos1_ref, pos2_ref, y_hbm, wts_ref, h_ref, g_ref, o_ref, ybuf, sem, *, tc):
    j = pl.program_id(0)
    slot = j % GATHER_SLOTS
    slot1 = (j + 1) % GATHER_SLOTS
    slot2 = (j + 2) % GATHER_SLOTS

    @pl.when(j == 0)
    def _():
        def start(r, carry):
            for s, idx in ((0, pos0_ref), (1, pos1_ref)):
                _slab_copy(y_hbm, ybuf, sem, s, idx[0, 0, 2 * r], r).start()
                _slab_copy(y_hbm, ybuf, sem, s, idx[0, 0, 2 * r + 1], tc + r).start()
            return carry
        lax.fori_loop(0, tc, start, 0)

    _slab_wait(y_hbm, ybuf, sem, slot, 2 * tc)
    for r in range(tc):
        _slab_copy(y_hbm, ybuf, sem, slot2, pos2_ref[0, 0, 2 * r], r).start(priority=0)
        _slab_copy(y_hbm, ybuf, sem, slot2, pos2_ref[0, 0, 2 * r + 1], tc + r).start(priority=1)
    w = wts_ref[...]
    y = (_slab_load(ybuf, slot, 0, tc).astype(F32) * w[:, 0:1]
         + _slab_load(ybuf, slot, tc, tc).astype(F32) * w[:, 1:2])
    o_ref[...] = _rms(h_ref[...] + y, g_ref[...])

    @pl.when(j == pl.num_programs(0) - 1)
    def _():
        _slab_wait(y_hbm, ybuf, sem, slot1, 2 * tc)
        _slab_wait(y_hbm, ybuf, sem, slot2, 2 * tc)


def _combine(pos3, y_slab, wts, h2, g, tc):
    t, d = h2.shape
    n = t // tc

    def pos_spec(ahead):
        return pl.BlockSpec((1, 1, 2 * tc), lambda i: (jnp.minimum(i + ahead, n - 1), 0, 0),
                            memory_space=pltpu.SMEM)

    return pl.pallas_call(
        functools.partial(_combine_kernel, tc=tc),
        grid=(n,),
        in_specs=[
            pos_spec(0), pos_spec(1), pos_spec(2),
            pl.BlockSpec(memory_space=pl.ANY),
            pl.BlockSpec((tc, LANES), lambda i: (i, 0)),
            pl.BlockSpec((tc, d), lambda i: (i, 0)),
            pl.BlockSpec((1, d), lambda i: (0, 0)),
        ],
        out_specs=pl.BlockSpec((tc, d), lambda i: (i, 0)),
        out_shape=jax.ShapeDtypeStruct((t, d), F32),
        scratch_shapes=[pltpu.VMEM((GATHER_SLOTS, 2 * tc * SLAB_PITCH, SLAB_W), SLAB_DTYPE),
                        pltpu.SemaphoreType.DMA((GATHER_SLOTS,))],
        compiler_params=_cparams(("arbitrary",)),
        name="combine",
    )(pos3, pos3, pos3, y_slab, wts, h2, g)


def _dispatch_plan(ids, counts, rows):
    flat_e = ids[:, 0:2].reshape(-1)
    rank = ids[:, 2:4].reshape(-1)
    n_asg = flat_e.shape[0]
    experts = jnp.arange(N_EXPERTS, dtype=jnp.int32)
    padded = ((counts + rows - 1) // rows) * rows
    pends = jnp.cumsum(padded)
    pstarts = pends - padded
    pos = jnp.sum(jnp.where(flat_e[:, None] == experts[None, :], pstarts[None, :], 0), axis=1) + rank
    n_rows = n_asg + N_EXPERTS * rows
    n_blocks = n_rows // rows
    row_tok = jnp.zeros((n_rows,), jnp.int32).at[pos].set(jnp.arange(n_asg, dtype=jnp.int32) // 2)
    starts = jnp.arange(n_blocks, dtype=jnp.int32) * rows
    blk_e = jnp.minimum(jnp.sum((pends[None, :] <= starts[:, None]).astype(jnp.int32), axis=1), N_EXPERTS - 1)
    n_used = (pends[-1] // rows).astype(jnp.int32).reshape(1)
    has = counts > 0
    succ = jnp.min(jnp.where((experts[None, :] > experts[:, None]) & has[None, :], experts[None, :], N_EXPERTS),
                   axis=1)
    succ = jnp.where(succ < N_EXPERTS, succ, -1)
    before = jnp.sum((has[None, :] & (experts[None, :] < blk_e[:, None])).astype(jnp.int32), axis=1)
    return (pos.astype(jnp.int32), row_tok.reshape(n_blocks, 1, rows), blk_e.astype(jnp.int32), n_used,
            succ[blk_e].astype(jnp.int32), (before % 2).astype(jnp.int32))


def _tile(n, pref):
    return pref if n % pref == 0 else n


def _layer(h, mem, p, l):
    bsz, seq, d = h.shape
    t = bsz * seq
    x2 = h.reshape(t, d)
    row = lambda v: v.reshape(1, -1)
    tm = _tile(t, ROW_TILE)

    w_in = p['w_in'][l]
    w_big = jnp.concatenate([w_in[:, 5168:9264], w_in[:, 0:3072], w_in[:, 3104:5152]], axis=1).astype(BF16)
    w_small = jnp.concatenate([w_in[:, 3072:3104], w_in[:, 5152:5168],
                               jnp.zeros((d, LANES - 48), F32)], axis=1).astype(BF16)
    big, small = _in_proj(x2, row(p['norm_mix'][l]), w_big, w_small, tm, IN_PROJ_COLS)
    big3 = big.reshape(bsz, seq, N_BIG)
    sm3 = small.reshape(bsz, seq, LANES)

    def lr_pad(w, off):
        return jnp.zeros((LANES, GLA_K), F32).at[off:off + GLA_RANK].set(w).astype(BF16)

    o_f = _gla_scan(big3, sm3, lr_pad(p['gla_w_lr_f'][l], SM_LRF), row(p['gla_b_lr_f'][l]), rev=False)
    y_a = _gla_scan(big3, sm3, lr_pad(p['gla_w_lr_b'][l], SM_LRB), row(p['gla_b_lr_b'][l]), rev=True,
                    prev=o_f, gn=row(p['gla_norm'][l]))

    cw = jnp.zeros((8, MLSTM_W), F32).at[:CONV_WIDTH].set(p['conv_w'][l].reshape(CONV_WIDTH, MLSTM_W))
    qm, km, vm = _mlstm_pre(big3, cw, row(p['conv_b'][l]), p['m_wq'][l].astype(BF16),
                            p['m_wk'][l].astype(BF16), p['m_wv'][l].astype(BF16), _tile(seq, CONV_ROWS))
    gbias = jnp.zeros((1, LANES), F32).at[0, SM_GATES:SM_GATES + 4 * MLSTM_HEADS].set(
        p['m_gate_bias'][l].reshape(-1))
    h_f = _mlstm_scan(qm, km, vm, sm3, gbias, rev=False)
    y_b = _mlstm_scan(qm, km, vm, sm3, gbias, rev=True, prev=h_f, big3=big3, mn=row(p['m_norm'][l]))

    assert OFF_GA % d == 0 and OFF_GB % d == 0
    h1 = _merge_mix(y_a.reshape(t, GLA_V), y_b.reshape(t, MLSTM_W), p['w_branch_a'][l].astype(BF16),
                    p['w_branch_b'][l].astype(BF16), big, p['w_mix_out'][l].astype(BF16), x2,
                    _tile(t, MERGE_MIX_ROWS))

    n_mem = mem.shape[1]
    kv = _kv_proj(mem.reshape(bsz * n_mem, d), row(p['norm_mem'][l]), p['w_xkv'][l].astype(BF16), KV_COLS)
    h2 = _xattn(h1, row(p['norm_xattn'][l]), p['w_xq'][l].astype(BF16), kv.reshape(bsz, n_mem, 2 * d),
                p['w_xo'][l].astype(BF16), seq, _tile(seq, XATTN_ROWS))

    wr = jnp.concatenate([p['w_group'][l], p['w_router'][l].transpose(1, 0, 2).reshape(d, N_EXPERTS),
                          jnp.zeros((d, LANES - N_GROUPS - N_EXPERTS), F32)], axis=1)
    wr_hi = wr.astype(BF16)
    wr2 = jnp.stack([wr_hi, (wr - wr_hi.astype(F32)).astype(BF16)])
    br = jnp.concatenate([p['b_group'][l], p['b_router'][l].reshape(-1),
                          jnp.zeros((LANES - N_GROUPS - N_EXPERTS,), F32)]).reshape(1, LANES)
    xn3, ids, wts, cnt = _router(h2, row(p['norm_ffn'][l]), wr2, br, tm)
    pos, row_tok3, blk_e, n_used, next_e, wslot = _dispatch_plan(ids, cnt[0, :N_EXPERTS], MOE_ROWS)
    y_rows = _experts(blk_e, n_used, next_e, wslot, row_tok3, xn3, p['w_gate'][l], p['w_up'][l], p['w_down'][l])
    tc = _tile(t, COMBINE_TOK)
    return pos.reshape(t // tc, 1, 2 * tc), y_rows, wts, h2, tc


def kernel(x, mem, norm_mix, w_in, gla_w_lr_f, gla_b_lr_f, gla_w_lr_b, gla_b_lr_b, gla_norm, conv_w, conv_b, m_wq, m_wk, m_wv, m_gate_bias, m_norm, w_branch_a, w_branch_b, w_mix_out, norm_xattn, norm_mem, w_xq, w_xkv, w_xo, norm_ffn, w_group, b_group, w_router, b_router, w_gate, w_up, w_down, norm_final):
    p = dict(norm_mix=norm_mix, w_in=w_in, gla_w_lr_f=gla_w_lr_f, gla_b_lr_f=gla_b_lr_f, gla_w_lr_b=gla_w_lr_b,
             gla_b_lr_b=gla_b_lr_b, gla_norm=gla_norm, conv_w=conv_w, conv_b=conv_b, m_wq=m_wq, m_wk=m_wk,
             m_wv=m_wv, m_gate_bias=m_gate_bias, m_norm=m_norm, w_branch_a=w_branch_a, w_branch_b=w_branch_b,
             w_mix_out=w_mix_out, norm_xattn=norm_xattn, norm_mem=norm_mem, w_xq=w_xq, w_xkv=w_xkv, w_xo=w_xo,
             norm_ffn=norm_ffn, w_group=w_group, b_group=b_group, w_router=w_router, b_router=b_router,
             w_gate=w_gate, w_up=w_up, w_down=w_down)
    bsz, seq, d = x.shape
    depth = norm_mix.shape[0]
    assert depth == 1, "the final norm is fused into the last layer's combine step"
    pos3, y_rows, wts, h2, tc = _layer(x, mem, p, 0)
    out = _combine(pos3, y_rows, wts, h2, norm_final.reshape(1, d), tc)
    return out.reshape(bsz, seq, d)
```
